```python
import math
import jax, jax.numpy as jnp
from jax import lax
import numpy as np


D_MODEL = 1024
BATCH = 2
SEQ = 16384
DEPTH = 2
DEC_BATCH = 8
DEC_SEQ = 2048
PAST_LEN = 128

N_MIXERS = 2
N_A_LAYERS = (DEPTH + N_MIXERS - 1) // N_MIXERS
N_B_LAYERS = DEPTH // N_MIXERS
S5_GROUP = 16
S5_GROUPS = D_MODEL // S5_GROUP
S5_STATE = 64
S5_CHUNK = 128
GLA_HEADS = 4
GLA_DK = D_MODEL // 2
GLA_DV = D_MODEL
GLA_HEAD_K = GLA_DK // GLA_HEADS
GLA_HEAD_V = GLA_DV // GLA_HEADS
GLA_GATE_RANK = 16
GLA_GATE_TAU = 16.0
GLA_CHUNK = 64
D_FF = 4 * D_MODEL
N_MOD = 6
EPS = 1e-6

kernel_name = 'hybrid_s5_gla_encoder'


def rmsnorm(x, g):
    xf = x.astype(jnp.float32)
    y = xf * lax.rsqrt(jnp.mean(xf * xf, axis=-1, keepdims=True) + EPS)
    return (y * g.astype(jnp.float32)).astype(x.dtype)


def _cplx_combine(left, right):
    a1r, a1i, b1r, b1i = left
    a2r, a2i, b2r, b2i = right
    return (a2r * a1r - a2i * a1i, a2r * a1i + a2i * a1r,
            a2r * b1r - a2i * b1i + b2r, a2r * b1i + a2i * b1r + b2i)


def s5_direction(u, lam_re, lam_im, log_dt, b_re, b_im, c_re, c_im):
    f32 = jnp.float32
    bsz, L = u.shape[0], u.shape[1]
    dt = jnp.exp(log_dt.astype(f32))[:, None]
    lr, li = lam_re.astype(f32), lam_im.astype(f32)
    mag = jnp.exp(lr * dt)
    ab_re, ab_im = mag * jnp.cos(li * dt), mag * jnp.sin(li * dt)
    den = lr * lr + li * li
    z_re = ((ab_re - 1.0) * lr + ab_im * li) / den
    z_im = (ab_im * lr - (ab_re - 1.0) * li) / den
    br, bi = b_re.astype(f32), b_im.astype(f32)
    bb_re = z_re[..., None] * br - z_im[..., None] * bi
    bb_im = z_re[..., None] * bi + z_im[..., None] * br
    cr, ci = c_re.astype(f32), c_im.astype(f32)
    n_chunks = L // S5_CHUNK
    uc = u.astype(f32).reshape(bsz, n_chunks, S5_CHUNK, S5_GROUPS, S5_GROUP).transpose(1, 0, 2, 3, 4)
    a_re = jnp.broadcast_to(ab_re, (bsz, S5_CHUNK, S5_GROUPS, S5_STATE))
    a_im = jnp.broadcast_to(ab_im, (bsz, S5_CHUNK, S5_GROUPS, S5_STATE))

    def step(carry, u_c):
        h_re, h_im = carry
        bu_re = jnp.einsum('bcgh,gph->bcgp', u_c, bb_re)
        bu_im = jnp.einsum('bcgh,gph->bcgp', u_c, bb_im)
        pr, pim, sr, si = lax.associative_scan(_cplx_combine, (a_re, a_im, bu_re, bu_im), axis=1)
        hr = pr * h_re[:, None] - pim * h_im[:, None] + sr
        hi = pr * h_im[:, None] + pim * h_re[:, None] + si
        y = jnp.einsum('bcgp,ghp->bcgh', hr, cr) - jnp.einsum('bcgp,ghp->bcgh', hi, ci)
        return (hr[:, -1], hi[:, -1]), y

    h0 = jnp.zeros((bsz, S5_GROUPS, S5_STATE), f32)
    _, ys = lax.scan(step, (h0, h0), uc)
    return ys.transpose(1, 0, 2, 3, 4).reshape(bsz, L, S5_GROUPS, S5_GROUP)


def s5_mixer(h, lam_re, lam_im, log_dt, b_re, b_im, c_re, c_im, d_skip, w_glu):
    bsz, L, _ = h.shape
    u = h.reshape(bsz, L, S5_GROUPS, S5_GROUP)
    y_f = s5_direction(u, lam_re[0], lam_im[0], log_dt[0], b_re[0], b_im[0], c_re[0], c_im[0])
    y_b = s5_direction(u[:, ::-1], lam_re[1], lam_im[1], log_dt[1], b_re[1], b_im[1], c_re[1], c_im[1])[:, ::-1]
    y = (y_f + y_b).reshape(bsz, L, D_MODEL).astype(h.dtype) + d_skip * h
    g = jax.nn.gelu(y)
    val, gate = jnp.split(g @ w_glu, 2, axis=-1)
    return val * jax.nn.sigmoid(gate)


def gla_direction(q, k, v, g):
    bsz, H, L, dk = q.shape
    dv = v.shape[-1]
    n = L // GLA_CHUNK

    def chunks(t):
        return t.reshape(bsz, H, n, GLA_CHUNK, t.shape[-1]).transpose(2, 0, 1, 3, 4)

    mask = jnp.tril(jnp.ones((GLA_CHUNK, GLA_CHUNK), dtype=bool))

    def step(S, inp):
        qc, kc, vc, gc = inp
        bcum = jnp.cumsum(gc, axis=2)
        b_last = bcum[:, :, -1:]
        q_dec = qc * jnp.exp(bcum)
        k_dec = kc * jnp.exp(-bcum)
        scores = jnp.where(mask, jnp.einsum('bhcd,bhsd->bhcs', q_dec, k_dec), 0.0)
        o = jnp.einsum('bhcs,bhsv->bhcv', scores, vc) + jnp.einsum('bhcd,bhdv->bhcv', q_dec, S)
        k_tail = kc * jnp.exp(b_last - bcum)
        S_new = jnp.exp(b_last[:, :, 0])[..., None] * S + jnp.einsum('bhcd,bhcv->bhdv', k_tail, vc)
        return S_new, o

    S0 = jnp.zeros((bsz, H, dk, dv), jnp.float32)
    _, o = lax.scan(step, S0, (chunks(q), chunks(k), chunks(v), chunks(g)))
    return o.transpose(1, 2, 0, 3, 4).reshape(bsz, H, L, dv)


def gla_mixer(h, w_in, w_a1, w_a2, b_a, norm_g, w_out):
    f32 = jnp.float32
    bsz, L, _ = h.shape
    proj = h @ w_in
    q, k, v, r = jnp.split(proj, [GLA_DK, 2 * GLA_DK, 2 * GLA_DK + GLA_DV], axis=-1)

    def heads(t, d):
        return t.reshape(bsz, L, GLA_HEADS, d).transpose(0, 2, 1, 3).astype(f32)

    q = heads(q, GLA_HEAD_K) * (GLA_HEAD_K ** -0.5)
    k = heads(k, GLA_HEAD_K)
    v = heads(v, GLA_HEAD_V)

    def log_gate(j):
        z = (h @ w_a1[j]) @ w_a2[j] + b_a[j]
        return heads(jax.nn.log_sigmoid(z.astype(f32)) / GLA_GATE_TAU, GLA_HEAD_K)

    def flip(t):
        return t[:, :, ::-1]

    o_f = gla_direction(q, k, v, log_gate(0))
    o_b = flip(gla_direction(flip(q), flip(k), flip(v), flip(log_gate(1))))
    o = o_f + o_b
    o = o * lax.rsqrt(jnp.mean(o * o, axis=-1, keepdims=True) + EPS)
    o = o.transpose(0, 2, 1, 3).reshape(bsz, L, GLA_DV) * norm_g.astype(f32)
    o = o.astype(h.dtype) * jax.nn.silu(r)
    return o @ w_out


def trunk(x, c, ada_w, ada_b, norm1_g, norm2_g,
          s5_lam_re, s5_lam_im, s5_log_dt, s5_b_re, s5_b_im, s5_c_re, s5_c_im, s5_d, s5_w_glu,
          gla_w_in, gla_w_a1, gla_w_a2, gla_b_a, gla_norm_g, gla_w_out,
          mlp_w1, mlp_w2, final_g):
    mod_all = jnp.einsum('bd,ldm->lbm', jax.nn.silu(c), ada_w) + ada_b[:, None]
    for i in range(DEPTH):
        shift1, scale1, gate1, shift2, scale2, gate2 = jnp.split(mod_all[i][:, None, :], N_MOD, axis=-1)
        h = rmsnorm(x, norm1_g[i]) * (1.0 + scale1) + shift1
        j = i // N_MIXERS
        if i % N_MIXERS == 0:
            m = s5_mixer(h, s5_lam_re[j], s5_lam_im[j], s5_log_dt[j], s5_b_re[j], s5_b_im[j],
                         s5_c_re[j], s5_c_im[j], s5_d[j], s5_w_glu[j])
        else:
            m = gla_mixer(h, gla_w_in[j], gla_w_a1[j], gla_w_a2[j], gla_b_a[j], gla_norm_g[j], gla_w_out[j])
        x = x + gate1 * m
        h = rmsnorm(x, norm2_g[i]) * (1.0 + scale2) + shift2
        x = x + gate2 * (jnp.square(jax.nn.relu(h @ mlp_w1[i])) @ mlp_w2[i])
    return rmsnorm(x, final_g)


def setup_inputs(seed: int = 0) -> dict:
    key = jax.random.key(seed)
    ks = iter(jax.random.split(key, 32))
    f32 = jnp.float32

    def nrm(shape, std):
        return std * jax.random.normal(next(ks), shape, f32)

    sa = (N_A_LAYERS, 2, S5_GROUPS, S5_STATE)
    return {
        'x_prompt': nrm((BATCH, SEQ, D_MODEL), 1.0),
        'x_sample': nrm((DEC_BATCH, DEC_SEQ, D_MODEL), 1.0),
        'c_prompt': nrm((BATCH, D_MODEL), 1.0),
        'c_sample': nrm((DEC_BATCH, D_MODEL), 1.0),
        'ada_w': nrm((DEPTH, D_MODEL, N_MOD * D_MODEL), 0.5 * D_MODEL ** -0.5),
        'ada_b': nrm((DEPTH, N_MOD * D_MODEL), 0.02),
        'norm1_g': 1.0 + nrm((DEPTH, D_MODEL), 0.02),
        'norm2_g': 1.0 + nrm((DEPTH, D_MODEL), 0.02),
        's5_lam_re': -0.5 + nrm(sa, 0.01),
        's5_lam_im': math.pi * jnp.arange(S5_STATE, dtype=f32) + nrm(sa, 0.01),
        's5_log_dt': jax.random.uniform(next(ks), (N_A_LAYERS, 2, S5_GROUPS), f32,
                                        math.log(1e-3), math.log(1e-1)),
        's5_b_re': nrm((N_A_LAYERS, 2, S5_GROUPS, S5_STATE, S5_GROUP), (2 * S5_GROUP) ** -0.5),
        's5_b_im': nrm((N_A_LAYERS, 2, S5_GROUPS, S5_STATE, S5_GROUP), (2 * S5_GROUP) ** -0.5),
        's5_c_re': nrm((N_A_LAYERS, 2, S5_GROUPS, S5_GROUP, S5_STATE), 0.7),
        's5_c_im': nrm((N_A_LAYERS, 2, S5_GROUPS, S5_GROUP, S5_STATE), 0.7),
        's5_d': nrm((N_A_LAYERS, D_MODEL), 1.0),
        's5_w_glu': nrm((N_A_LAYERS, D_MODEL, 2 * D_MODEL), D_MODEL ** -0.5),
        'gla_w_in': nrm((N_B_LAYERS, D_MODEL, 2 * GLA_DK + 2 * GLA_DV), D_MODEL ** -0.5),
        'gla_w_a1': nrm((N_B_LAYERS, 2, D_MODEL, GLA_GATE_RANK), D_MODEL ** -0.5),
        'gla_w_a2': nrm((N_B_LAYERS, 2, GLA_GATE_RANK, GLA_DK), GLA_GATE_RANK ** -0.5),
        'gla_b_a': nrm((N_B_LAYERS, 2, GLA_DK), 0.01),
        'gla_norm_g': 1.0 + nrm((N_B_LAYERS, GLA_DV), 0.02),
        'gla_w_out': nrm((N_B_LAYERS, GLA_DV, D_MODEL), GLA_DV ** -0.5),
        'mlp_w1': nrm((DEPTH, D_MODEL, D_FF), D_MODEL ** -0.5),
        'mlp_w2': nrm((DEPTH, D_FF, D_MODEL), D_FF ** -0.5),
        'final_g': 1.0 + nrm((D_MODEL,), 0.02),
    }


def reference(x_prompt, x_sample, c_prompt, c_sample, ada_w, ada_b, norm1_g, norm2_g,
              s5_lam_re, s5_lam_im, s5_log_dt, s5_b_re, s5_b_im, s5_c_re, s5_c_im, s5_d, s5_w_glu,
              gla_w_in, gla_w_a1, gla_w_a2, gla_b_a, gla_norm_g, gla_w_out,
              mlp_w1, mlp_w2, final_g):
    y_prompt = trunk(x_prompt, c_prompt, ada_w, ada_b, norm1_g, norm2_g,
                     s5_lam_re, s5_lam_im, s5_log_dt, s5_b_re, s5_b_im, s5_c_re, s5_c_im, s5_d, s5_w_glu,
                     gla_w_in, gla_w_a1, gla_w_a2, gla_b_a, gla_norm_g, gla_w_out,
                     mlp_w1, mlp_w2, final_g)
    y_sample = trunk(x_sample, c_sample, ada_w, ada_b, norm1_g, norm2_g,
                     s5_lam_re, s5_lam_im, s5_log_dt, s5_b_re, s5_b_im, s5_c_re, s5_c_im, s5_d, s5_w_glu,
                     gla_w_in, gla_w_a1, gla_w_a2, gla_b_a, gla_norm_g, gla_w_out,
                     mlp_w1, mlp_w2, final_g)
    return (y_prompt, y_sample)
```

```python
import functools
import math

import jax
import jax.numpy as jnp
from jax import lax
from jax.experimental import pallas as pl
from jax.experimental.pallas import tpu as pltpu

F32 = jnp.float32
BF16 = jnp.bfloat16

D_MODEL = 1024
S5_GROUP = 16
S5_GROUPS = D_MODEL // S5_GROUP
S5_STATE = 64
GLA_HEADS = 4
GLA_DK = D_MODEL // 2
GLA_DV = D_MODEL
GLA_HEAD_K = GLA_DK // GLA_HEADS
GLA_HEAD_V = GLA_DV // GLA_HEADS
GLA_GATE_RANK = 16
GLA_GATE_TAU = 16.0
GLA_CHUNK = 64
D_FF = 4 * D_MODEL
N_MOD = 6
EPS = 1e-6

LANES = 128
SUBLANES = 8
SEGS = LANES // S5_GROUP

S5_C = 32
S5_CW = S5_C * S5_GROUP
S5_NSTATE = 4 * S5_STATE
S5_LEVELS = 3
S5_TBK = 512

VMEM_LIMIT = 56 * 1024 * 1024


def _cparams(sem):
    return pltpu.CompilerParams(dimension_semantics=sem, vmem_limit_bytes=VMEM_LIMIT)


def _dot(a, b, precision=None):
    return jnp.dot(a, b, preferred_element_type=F32, precision=precision)


def _dot_nt(a, b, precision=None):
    return lax.dot_general(a, b, (((1,), (1,)), ((), ())),
                           preferred_element_type=F32, precision=precision)


def _dot_tn(a, b, precision=None):
    return lax.dot_general(a, b, (((0,), (0,)), ((), ())),
                           preferred_element_type=F32, precision=precision)


def _norm_mod(x, g, scale, shift):
    y = x * lax.rsqrt(jnp.mean(x * x, axis=-1, keepdims=True) + EPS)
    return (y * g) * (1.0 + scale) + shift


def _mod_kernel(c_ref, w_ref, b_ref, o_ref):
    c = c_ref[...]
    s = c * jax.nn.sigmoid(c)
    o_ref[0] = _dot(s.astype(BF16), w_ref[0].astype(BF16)) + b_ref[0]


def _modulation(c_all, ada_w, ada_b):
    depth, d, n = ada_w.shape
    rows = c_all.shape[0]
    tn = 1536
    return pl.pallas_call(
        _mod_kernel,
        grid=(depth, n // tn),
        in_specs=[
            pl.BlockSpec((rows, d), lambda l, j: (0, 0)),
            pl.BlockSpec((1, d, tn), lambda l, j: (l, 0, j)),
            pl.BlockSpec((1, 1, tn), lambda l, j: (l, 0, j)),
        ],
        out_specs=pl.BlockSpec((1, rows, tn), lambda l, j: (l, 0, j)),
        out_shape=jax.ShapeDtypeStruct((depth, rows, n), F32),
        compiler_params=_cparams(("parallel", "parallel")),
        name="adaln_mod",
    )(c_all, ada_w, ada_b.reshape(depth, 1, n))


def _seg_transpose8(arrs, seg):
    a = list(arrs)
    for d in (4, 2, 1):
        keep = (seg & d) == 0
        new = list(a)
        for i in range(SEGS):
            if i & d == 0:
                lo, hi = a[i], a[i + d]
                new[i] = jnp.where(keep, lo, pltpu.roll(hi, S5_GROUP * d, 1))
                new[i + d] = jnp.where(keep, pltpu.roll(lo, LANES - S5_GROUP * d, 1), hi)
        a = new
    return a


def _s5_pre_kernel(x_ref, g_ref, sc_ref, sh_ref, u_ref, h_scr):
    h = _norm_mod(x_ref[0], g_ref[...], sc_ref[0], sh_ref[0])
    for j in range(D_MODEL // LANES):
        h_scr[j] = h[:, LANES * j:LANES * (j + 1)]
    nk = S5_TBK // S5_C
    seg = lax.broadcasted_iota(jnp.int32, (nk, LANES), 1) // S5_GROUP
    for j in range(D_MODEL // LANES):
        for q in range(S5_C // SEGS):
            arrs = [h_scr[j, pl.ds(SEGS * q + i, nk, stride=S5_C), :] for i in range(SEGS)]
            outs = _seg_transpose8(arrs, seg)
            for gl in range(SEGS):
                u_ref[SEGS * j + gl, :, pl.ds(LANES * q, LANES)] = outs[gl].astype(BF16)


def _s5_pre(x, norm_g, scale, shift):
    b, l, d = x.shape
    nb = l // S5_TBK
    nk = S5_TBK // S5_C
    return pl.pallas_call(
        _s5_pre_kernel,
        grid=(b, nb),
        in_specs=[
            pl.BlockSpec((1, S5_TBK, d), lambda bi, i: (bi, i, 0)),
            pl.BlockSpec((1, d), lambda bi, i: (0, 0)),
            pl.BlockSpec((1, 1, d), lambda bi, i: (bi, 0, 0)),
            pl.BlockSpec((1, 1, d), lambda bi, i: (bi, 0, 0)),
        ],
        out_specs=pl.BlockSpec((S5_GROUPS, nk, S5_CW), lambda bi, i: (0, bi * nb + i, 0)),
        out_shape=jax.ShapeDtypeStruct((S5_GROUPS, b * l // S5_C, S5_CW), BF16),
        scratch_shapes=[pltpu.VMEM((d // LANES, S5_TBK, LANES), F32)],
        compiler_params=_cparams(("parallel", "parallel")),
        name="s5_pre",
    )(x, norm_g.reshape(1, d), scale, shift)


def _s5_build_kernel(rows_ref, rowsfb_ref, bcat_ref, bswp_ref, ccat_ref, cswp_ref,
                     w1_ref, w2_ref, sc_ref, ex_scr, mat_scr, kt_scr):
    c = S5_C
    cw = S5_CW
    hi = lax.Precision.HIGHEST
    nt = ((c + 1 + SUBLANES - 1) // SUBLANES) * SUBLANES
    lane1 = lax.broadcasted_iota(jnp.int32, (1, LANES), 1)
    sgn = jnp.where(lane1 < S5_STATE, -1.0, 1.0).astype(F32)
    nrow = lax.broadcasted_iota(jnp.int32, (nt, LANES), 0).astype(F32)

    for d in range(2):
        lr = rows_ref[0, d, 0:1, :]
        li = rows_ref[0, d, 1:2, :]
        dt = jnp.exp(rows_ref[0, d, 2:3, :])
        mag = jnp.exp(nrow * (lr * dt))
        ang = nrow * (li * dt)
        p_re = mag * jnp.cos(ang)
        p_im = mag * jnp.sin(ang)
        ab_re = p_re[1:2, :]
        ab_im = p_im[1:2, :]
        den = lr * lr + li * li
        z_re = ((ab_re - 1.0) * lr + ab_im * li) / den
        z_im = (ab_im * lr - (ab_re - 1.0) * li) / den
        bbar = z_re * bcat_ref[0, d] + (z_im * sgn) * bswp_ref[0, d]
        p_is = p_im * sgn
        for n in range(c + 1):
            blk = pl.ds(S5_GROUP * n, S5_GROUP)
            ex_scr[0, blk, :] = jnp.broadcast_to(p_re[n:n + 1, :], (S5_GROUP, LANES))
            ex_scr[1, blk, :] = jnp.broadcast_to(p_is[n:n + 1, :], (S5_GROUP, LANES))
            ex_scr[2, blk, :] = jnp.broadcast_to(p_re[c - n:c - n + 1, :], (S5_GROUP, LANES))
            ex_scr[3, blk, :] = jnp.broadcast_to(p_is[c - n:c - n + 1, :], (S5_GROUP, LANES))
        c_t = jnp.concatenate([ccat_ref[0, d]] * c, axis=0)
        c_s = jnp.concatenate([cswp_ref[0, d]] * c, axis=0)
        bswap = z_re * bswp_ref[0, d] - (z_im * sgn) * bcat_ref[0, d]
        b_t = jnp.concatenate([bbar] * c, axis=0)
        b_s = jnp.concatenate([bswap] * c, axis=0)
        lo = pl.ds(0, cw)
        up = pl.ds(S5_GROUP, cw)
        bneg = bbar * (-sgn)
        if d == 0:
            ct = ex_scr[0, lo, :] * c_t + ex_scr[1, lo, :] * c_s
            kt_scr[0] = _dot_nt(bneg, ct, hi)
            mat_scr[0] = ex_scr[2, up, :] * b_t + ex_scr[3, up, :] * b_s
            mat_scr[2] = ex_scr[0, up, :] * c_t + ex_scr[1, up, :] * c_s
        else:
            ct = ex_scr[2, up, :] * c_t + ex_scr[3, up, :] * c_s
            kt_scr[1] = _dot_nt(bneg, ct, hi)
            mat_scr[1] = ex_scr[0, lo, :] * b_t + ex_scr[1, lo, :] * b_s
            mat_scr[3] = ex_scr[2, lo, :] * c_t + ex_scr[3, lo, :] * c_s

    lane_cw = lax.broadcasted_iota(jnp.int32, (S5_GROUP, cw), 1)
    kf = kt_scr[0]
    kb = kt_scr[1]
    for s in range(c):
        rf = kf if s == 0 else pltpu.roll(kf, S5_GROUP * s, 1)
        rb = kb if s == c - 1 else pltpu.roll(kb, S5_GROUP * (s + 1), 1)
        blk = (jnp.where(lane_cw >= S5_GROUP * s, rf, 0.0)
               + jnp.where(lane_cw < S5_GROUP * (s + 1), rb, 0.0))
        w1_ref[0, pl.ds(S5_GROUP * s, S5_GROUP), pl.ds(0, cw)] = blk.astype(BF16)

    lane_m = lax.broadcasted_iota(jnp.int32, (cw, LANES), 1) < S5_STATE
    bmf = mat_scr[0]
    bmb = mat_scr[1]
    w1_ref[0, :, pl.ds(cw, LANES)] = jnp.where(
        lane_m, bmf, pltpu.roll(bmb, S5_STATE, 1)).astype(BF16)
    w1_ref[0, :, pl.ds(cw + LANES, LANES)] = jnp.where(
        lane_m, pltpu.roll(bmf, S5_STATE, 1), bmb).astype(BF16)
    caf = mat_scr[2]
    cab = mat_scr[3]
    w2t_re = jnp.where(lane_m, caf, pltpu.roll(cab, S5_STATE, 1))
    w2t_im = -jnp.where(lane_m, pltpu.roll(caf, S5_STATE, 1), cab)
    w2_ref[0, pl.ds(0, LANES), :] = w2t_re.T.astype(BF16)
    w2_ref[0, pl.ds(LANES, LANES), :] = w2t_im.T.astype(BF16)

    lrfb = rowsfb_ref[0, 0:1, :]
    lifb = rowsfb_ref[0, 1:2, :]
    dtfb = jnp.exp(rowsfb_ref[0, 2:3, :])
    row8 = lax.broadcasted_iota(jnp.int32, (SUBLANES, LANES), 0)
    fwd8 = lax.broadcasted_iota(jnp.int32, (SUBLANES, LANES), 1) < S5_STATE
    for lvl in range(S5_LEVELS):
        m = float(c * SUBLANES ** lvl)
        for k, dd in enumerate((1, 2, 4, 0)):
            if dd:
                n8 = jnp.full((SUBLANES, LANES), m * dd, F32)
                ok = (fwd8 & (row8 >= dd)) | (jnp.logical_not(fwd8) & (row8 < SUBLANES - dd))
            else:
                n8 = m * jnp.where(fwd8, row8 + 1, SUBLANES - row8).astype(F32)
                ok = row8 >= 0
            mag8 = jnp.exp(n8 * (lrfb * dtfb))
            ang8 = n8 * (lifb * dtfb)
            sc_ref[0, lvl * 8 + 2 * k] = jnp.where(ok, mag8 * jnp.cos(ang8), 0.0)
            sc_ref[0, lvl * 8 + 2 * k + 1] = jnp.where(ok, mag8 * jnp.sin(ang8), 0.0)


def _s5_build(lam_re, lam_im, log_dt, b_re, b_im, c_re, c_im):
    g, p = S5_GROUPS, S5_STATE

    def dup(a):
        return jnp.concatenate([a, a], axis=-1).transpose(1, 0, 2)[:, :, None, :]

    ldt = jnp.broadcast_to(log_dt.T[:, :, None, None], (g, 2, 1, 2 * p))
    rows = jnp.concatenate(
        [dup(lam_re), dup(lam_im), ldt, jnp.zeros((g, 2, SUBLANES - 3, 2 * p), F32)], axis=2)

    def fb(a):
        return jnp.concatenate([a[0], a[1]], axis=-1)[:, None, :]

    ldt2 = jnp.broadcast_to(log_dt[:, :, None], (2, g, p))
    rowsfb = jnp.concatenate(
        [fb(lam_re), fb(lam_im), fb(ldt2), jnp.zeros((g, SUBLANES - 3, 2 * p), F32)], axis=1)
    brt = b_re.transpose(1, 0, 3, 2)
    bit = b_im.transpose(1, 0, 3, 2)
    crt = c_re.transpose(1, 0, 2, 3)
    cit = c_im.transpose(1, 0, 2, 3)
    bcat = jnp.concatenate([brt, bit], axis=-1)
    bswp = jnp.concatenate([bit, brt], axis=-1)
    ccat = jnp.concatenate([crt, cit], axis=-1)
    cswp = jnp.concatenate([cit, crt], axis=-1)
    nsc = S5_LEVELS * 8
    spec4 = pl.BlockSpec((1, 2, S5_GROUP, LANES), lambda i: (i, 0, 0, 0))
    return pl.pallas_call(
        _s5_build_kernel,
        grid=(g,),
        in_specs=[
            pl.BlockSpec((1, 2, SUBLANES, LANES), lambda i: (i, 0, 0, 0)),
            pl.BlockSpec((1, SUBLANES, LANES), lambda i: (i, 0, 0)),
            spec4, spec4, spec4, spec4,
        ],
        out_specs=[
            pl.BlockSpec((1, S5_CW, S5_CW + S5_NSTATE), lambda i: (i, 0, 0)),
            pl.BlockSpec((1, S5_NSTATE, S5_CW), lambda i: (i, 0, 0)),
            pl.BlockSpec((1, nsc, SUBLANES, LANES), lambda i: (i, 0, 0, 0)),
        ],
        out_shape=[
            jax.ShapeDtypeStruct((g, S5_CW, S5_CW + S5_NSTATE), BF16),
            jax.ShapeDtypeStruct((g, S5_NSTATE, S5_CW), BF16),
            jax.ShapeDtypeStruct((g, nsc, SUBLANES, LANES), F32),
        ],
        scratch_shapes=[
            pltpu.VMEM((4, (S5_C + 1) * S5_GROUP, LANES), F32),
            pltpu.VMEM((4, S5_CW, LANES), F32),
            pltpu.VMEM((2, S5_GROUP, S5_CW), F32),
        ],
        compiler_params=_cparams(("parallel",)),
        name="s5_build",
    )(rows, rowsfb, bcat, bswp, ccat, cswp)


def _tile_scan(re, im, sc_ref, lvl, fwd):
    for k, dd in enumerate((1, 2, 4)):
        mr = sc_ref[0, lvl * 8 + 2 * k]
        mi = sc_ref[0, lvl * 8 + 2 * k + 1]
        sre = jnp.where(fwd, pltpu.roll(re, dd, 0), pltpu.roll(re, SUBLANES - dd, 0))
        sim = jnp.where(fwd, pltpu.roll(im, dd, 0), pltpu.roll(im, SUBLANES - dd, 0))
        re, im = re + mr * sre - mi * sim, im + mr * sim + mi * sre
    return re, im


def _scan_rows(bufs, lvl, base, ntiles, sc_ref, p_ref, p_base):
    re_ref, im_ref = bufs[lvl]
    fwd = lax.broadcasted_iota(jnp.int32, (SUBLANES, LANES), 1) < S5_STATE
    row = lax.broadcasted_iota(jnp.int32, (SUBLANES, LANES), 0)
    fwd1 = fwd[0:1, :]

    def carried(hre, him, cre, cim):
        pre = jnp.where(fwd, jnp.where(row == 0, cre, pltpu.roll(hre, 1, 0)),
                        jnp.where(row == SUBLANES - 1, cre, pltpu.roll(hre, SUBLANES - 1, 0)))
        pim = jnp.where(fwd, jnp.where(row == 0, cim, pltpu.roll(him, 1, 0)),
                        jnp.where(row == SUBLANES - 1, cim, pltpu.roll(him, SUBLANES - 1, 0)))
        return pre, pim

    if ntiles == 1:
        sl = pl.ds(base, SUBLANES)
        hre, him = _tile_scan(re_ref[sl, :], im_ref[sl, :], sc_ref, lvl, fwd)
        re_ref[sl, :] = hre
        im_ref[sl, :] = him
        if lvl == 0:
            zero = jnp.zeros((1, LANES), F32)
            pre, pim = carried(hre, him, zero, zero)
            p_ref[pl.ds(p_base, SUBLANES), pl.ds(0, LANES)] = pre
            p_ref[pl.ds(p_base, SUBLANES), pl.ds(LANES, LANES)] = pim
        return

    ere_ref, eim_ref = bufs[lvl + 1]
    ntn = (ntiles + SUBLANES - 1) // SUBLANES
    ere_ref[...] = jnp.zeros_like(ere_ref)
    eim_ref[...] = jnp.zeros_like(eim_ref)

    def local(i, carry):
        sl = pl.ds(pl.multiple_of(base + i * SUBLANES, SUBLANES), SUBLANES)
        hre, him = _tile_scan(re_ref[sl, :], im_ref[sl, :], sc_ref, lvl, fwd)
        re_ref[sl, :] = hre
        im_ref[sl, :] = him
        ere_ref[pl.ds(SUBLANES + i, 1), :] = jnp.where(fwd1, hre[SUBLANES - 1:SUBLANES, :], hre[0:1, :])
        eim_ref[pl.ds(SUBLANES + i, 1), :] = jnp.where(fwd1, him[SUBLANES - 1:SUBLANES, :], him[0:1, :])
        return carry

    lax.fori_loop(0, ntiles, local, 0)
    _scan_rows(bufs, lvl + 1, SUBLANES, ntn, sc_ref, None, 0)
    apr = sc_ref[0, lvl * 8 + 6]
    api = sc_ref[0, lvl * 8 + 7]

    def apply(i, carry):
        sl = pl.ds(pl.multiple_of(base + i * SUBLANES, SUBLANES), SUBLANES)
        cre = jnp.where(fwd1, ere_ref[pl.ds(SUBLANES - 1 + i, 1), :], ere_ref[pl.ds(SUBLANES + 1 + i, 1), :])
        cim = jnp.where(fwd1, eim_ref[pl.ds(SUBLANES - 1 + i, 1), :], eim_ref[pl.ds(SUBLANES + 1 + i, 1), :])
        hre = re_ref[sl, :] + apr * cre - api * cim
        him = im_ref[sl, :] + apr * cim + api * cre
        if lvl == 0:
            psl = pl.ds(pl.multiple_of(p_base + i * SUBLANES, SUBLANES), SUBLANES)
            pre, pim = carried(hre, him, cre, cim)
            p_ref[psl, pl.ds(0, LANES)] = pre
            p_ref[psl, pl.ds(LANES, LANES)] = pim
        else:
            re_ref[sl, :] = hre
            im_ref[sl, :] = him
        return carry

    lax.fori_loop(0, ntiles, apply, 0)


def _s5_core_kernel(nseq, u_ref, w1_ref, w2_ref, sc_ref, y_ref,
                    s_re, s_im, e1_re, e1_im, e2_re, e2_im, p_scr):
    rows = u_ref.shape[1]
    nk = rows // nseq
    u = u_ref[0]
    st = _dot(u, w1_ref[0, :, pl.ds(S5_CW, S5_NSTATE)])
    s_re[...] = st[:, :LANES]
    s_im[...] = st[:, LANES:]
    bufs = [(s_re, s_im), (e1_re, e1_im), (e2_re, e2_im)]
    for b in range(nseq):
        _scan_rows(bufs, 0, b * nk, nk // SUBLANES, sc_ref, p_scr, b * nk)
    y_ref[0] = (_dot(u, w1_ref[0, :, pl.ds(0, S5_CW)])
                + _dot(p_scr[...].astype(BF16), w2_ref[0]))


def _s5_core(u, w1, w2, sc, nseq):
    g, rows, cw = u.shape
    nk = rows // nseq
    nt0 = nk // SUBLANES
    e1_rows = ((nt0 + SUBLANES - 1) // SUBLANES) * SUBLANES + 2 * SUBLANES
    nt1 = (nt0 + SUBLANES - 1) // SUBLANES
    e2_rows = ((nt1 + SUBLANES - 1) // SUBLANES) * SUBLANES + 2 * SUBLANES
    assert nt1 <= SUBLANES * SUBLANES, "sequence too long for S5_LEVELS scan levels"
    return pl.pallas_call(
        functools.partial(_s5_core_kernel, nseq),
        grid=(g,),
        in_specs=[
            pl.BlockSpec((1, rows, cw), lambda i: (i, 0, 0)),
            pl.BlockSpec((1, cw, cw + S5_NSTATE), lambda i: (i, 0, 0)),
            pl.BlockSpec((1, S5_NSTATE, cw), lambda i: (i, 0, 0)),
            pl.BlockSpec((1, S5_LEVELS * 8, SUBLANES, LANES), lambda i: (i, 0, 0, 0)),
        ],
        out_specs=pl.BlockSpec((1, rows, cw), lambda i: (i, 0, 0)),
        out_shape=jax.ShapeDtypeStruct((g, rows, cw), F32),
        scratch_shapes=[
            pltpu.VMEM((rows, LANES), F32), pltpu.VMEM((rows, LANES), F32),
            pltpu.VMEM((e1_rows, LANES), F32), pltpu.VMEM((e1_rows, LANES), F32),
            pltpu.VMEM((e2_rows, LANES), F32), pltpu.VMEM((e2_rows, LANES), F32),
            pltpu.VMEM((rows, 2 * LANES), F32),
        ],
        compiler_params=_cparams(("parallel",)),
        name="s5_core",
    )(u, w1, w2, sc)


def _s5_post_kernel(x_ref, y_ref, g_ref, sc_ref, sh_ref, gt_ref, dsk_ref, wg_ref, o_ref, y_scr):
    nk = S5_TBK // S5_C
    seg = lax.broadcasted_iota(jnp.int32, (nk, LANES), 1) // S5_GROUP
    for j in range(D_MODEL // LANES):
        for q in range(S5_C // SEGS):
            arrs = [y_ref[SEGS * j + gl, :, pl.ds(LANES * q, LANES)] for gl in range(SEGS)]
            outs = _seg_transpose8(arrs, seg)
            for i in range(SEGS):
                y_scr[j, pl.ds(SEGS * q + i, nk, stride=S5_C), :] = outs[i]
    x = x_ref[0]
    h = _norm_mod(x, g_ref[...], sc_ref[0], sh_ref[0])
    y = jnp.concatenate([y_scr[j] for j in range(D_MODEL // LANES)], axis=-1) + dsk_ref[...] * h
    z = _dot(jax.nn.gelu(y).astype(BF16), wg_ref[...])
    m = z[:, :D_MODEL] * jax.nn.sigmoid(z[:, D_MODEL:])
    o_ref[0] = x + gt_ref[0] * m


def _s5_post(x, y, norm_g, scale, shift, gate, d_skip, w_glu):
    b, l, d = x.shape
    nb = l // S5_TBK
    nk = S5_TBK // S5_C
    vec = pl.BlockSpec((1, d), lambda bi, i: (0, 0))
    mod = pl.BlockSpec((1, 1, d), lambda bi, i: (bi, 0, 0))
    return pl.pallas_call(
        _s5_post_kernel,
        grid=(b, nb),
        in_specs=[
            pl.BlockSpec((1, S5_TBK, d), lambda bi, i: (bi, i, 0)),
            pl.BlockSpec((S5_GROUPS, nk, S5_CW), lambda bi, i: (0, bi * nb + i, 0)),
            vec, mod, mod, mod, vec,
            pl.BlockSpec((d, 2 * d), lambda bi, i: (0, 0)),
        ],
        out_specs=pl.BlockSpec((1, S5_TBK, d), lambda bi, i: (bi, i, 0)),
        out_shape=jax.ShapeDtypeStruct((b, l, d), F32),
        scratch_shapes=[pltpu.VMEM((d // LANES, S5_TBK, LANES), F32)],
        compiler_params=_cparams(("parallel", "parallel")),
        name="s5_post",
    )(x, y, norm_g.reshape(1, d), scale, shift, gate, d_skip.reshape(1, d), w_glu)


def _mlp_kernel(final, x_ref, g_ref, sc_ref, sh_ref, gt_ref, w1_ref, w2_ref, fg_ref, o_ref,
                h_scr, acc_scr):
    j = pl.program_id(2)

    @pl.when(j == 0)
    def _():
        h_scr[...] = _norm_mod(x_ref[0], g_ref[...], sc_ref[0], sh_ref[0]).astype(BF16)
        acc_scr[...] = jnp.zeros_like(acc_scr)

    a = jnp.maximum(_dot(h_scr[...], w1_ref[...]), 0.0)
    acc_scr[...] += _dot((a * a).astype(BF16), w2_ref[...])

    @pl.when(j == pl.num_programs(2) - 1)
    def _():
        out = x_ref[0] + gt_ref[0] * acc_scr[...]
        if final:
            out = out * lax.rsqrt(jnp.mean(out * out, axis=-1, keepdims=True) + EPS) * fg_ref[...]
        o_ref[0] = out


def _mlp(x, norm_g, scale, shift, gate, w1, w2, final_g, final):
    b, l, d = x.shape
    ff = w1.shape[1]
    tm = min(1024, l)
    tf = 1024
    vec = pl.BlockSpec((1, d), lambda bi, i, j: (0, 0))
    mod = pl.BlockSpec((1, 1, d), lambda bi, i, j: (bi, 0, 0))
    return pl.pallas_call(
        functools.partial(_mlp_kernel, final),
        grid=(b, l // tm, ff // tf),
        in_specs=[
            pl.BlockSpec((1, tm, d), lambda bi, i, j: (bi, i, 0)),
            vec, mod, mod, mod,
            pl.BlockSpec((d, tf), lambda bi, i, j: (0, j)),
            pl.BlockSpec((tf, d), lambda bi, i, j: (j, 0)),
            vec,
        ],
        out_specs=pl.BlockSpec((1, tm, d), lambda bi, i, j: (bi, i, 0)),
        out_shape=jax.ShapeDtypeStruct((b, l, d), F32),
        scratch_shapes=[pltpu.VMEM((tm, d), BF16), pltpu.VMEM((tm, d), F32)],
        compiler_params=_cparams(("parallel", "parallel", "arbitrary")),
        name="mlp",
    )(x, norm_g.reshape(1, d), scale, shift, gate, w1, w2, final_g.reshape(1, d))


def _gla_pre_kernel(x_ref, g_ref, sc_ref, sh_ref, win_ref, wa1_ref, wa2_ref, ba_ref,
                    q_ref, k_ref, v_ref, r_ref, gf_ref, gb_ref):
    h = _norm_mod(x_ref[0], g_ref[...], sc_ref[0], sh_ref[0]).astype(BF16)
    proj = _dot(h, win_ref[...])
    q_ref[0] = proj[:, :GLA_DK] * (GLA_HEAD_K ** -0.5)
    k_ref[0] = proj[:, GLA_DK:2 * GLA_DK]
    v_ref[0] = proj[:, 2 * GLA_DK:2 * GLA_DK + GLA_DV].astype(BF16)
    r_ref[0] = proj[:, 2 * GLA_DK + GLA_DV:]
    a = _dot(h, wa1_ref[...]).astype(BF16)
    z = _dot(a, wa2_ref[...]) + ba_ref[...]
    lg = (jnp.minimum(z, 0.0) - jnp.log1p(jnp.exp(-jnp.abs(z)))) / GLA_GATE_TAU
    gf_ref[0] = lg[:, :GLA_DK]
    gb_ref[0] = lg[:, GLA_DK:]


def _gla_pre(x, norm_g, scale, shift, w_in, w_a1c, w_a2bd, b_ac):
    b, l, d = x.shape
    tm = 512
    vec = pl.BlockSpec((1, d), lambda bi, i: (0, 0))
    mod = pl.BlockSpec((1, 1, d), lambda bi, i: (bi, 0, 0))

    def full(a):
        return pl.BlockSpec(a.shape, lambda bi, i: (0,) * a.ndim)

    def tok(w):
        return pl.BlockSpec((1, tm, w), lambda bi, i: (bi, i, 0))

    return pl.pallas_call(
        _gla_pre_kernel,
        grid=(b, l // tm),
        in_specs=[tok(d), vec, mod, mod, full(w_in), full(w_a1c), full(w_a2bd), full(b_ac)],
        out_specs=[tok(GLA_DK), tok(GLA_DK), tok(GLA_DV), tok(GLA_DV), tok(GLA_DK), tok(GLA_DK)],
        out_shape=[
            jax.ShapeDtypeStruct((b, l, GLA_DK), F32),
            jax.ShapeDtypeStruct((b, l, GLA_DK), F32),
            jax.ShapeDtypeStruct((b, l, GLA_DV), BF16),
            jax.ShapeDtypeStruct((b, l, GLA_DV), F32),
            jax.ShapeDtypeStruct((b, l, GLA_DK), F32),
            jax.ShapeDtypeStruct((b, l, GLA_DK), F32),
        ],
        compiler_params=_cparams(("parallel", "parallel")),
        name="gla_pre",
    )(x, norm_g.reshape(1, d), scale, shift, w_in, w_a1c, w_a2bd, b_ac)


def _gla_chunk(q, k, v, g, st_ref, o_ref, tri, last_row):
    bcum = _dot(tri.astype(F32), g, lax.Precision.HIGHEST)
    blast = bcum[last_row:last_row + 1, :]
    e_pos = jnp.exp(bcum)
    e_neg = jnp.exp(-bcum)
    e_tail = jnp.exp(blast - bcum)
    dec = jnp.exp(blast)
    for h in range(GLA_HEADS):
        kl = slice(GLA_HEAD_K * h, GLA_HEAD_K * (h + 1))
        vl = slice(GLA_HEAD_V * h, GLA_HEAD_V * (h + 1))
        qd = (q[:, kl] * e_pos[:, kl]).astype(BF16)
        kd = (k[:, kl] * e_neg[:, kl]).astype(BF16)
        kt = (k[:, kl] * e_tail[:, kl]).astype(BF16)
        vh = v[:, vl]
        s_t = st_ref[h]
        scores = jnp.where(tri, _dot_nt(qd, kd), 0.0)
        o = _dot(scores.astype(BF16), vh) + _dot_nt(qd, s_t.astype(BF16))
        o_ref[0, :, pl.ds(GLA_HEAD_V * h, GLA_HEAD_V)] = o
        st_ref[h] = s_t * dec[:, kl] + _dot_tn(vh, kt)


def _gla_core_kernel(qf_ref, kf_ref, vf_ref, gf_ref, qb_ref, kb_ref, vb_ref, gb_ref,
                     of_ref, ob_ref, stf_scr, stb_scr):
    @pl.when(pl.program_id(1) == 0)
    def _():
        stf_scr[...] = jnp.zeros_like(stf_scr)
        stb_scr[...] = jnp.zeros_like(stb_scr)

    c = GLA_CHUNK
    row = lax.broadcasted_iota(jnp.int32, (c, c), 0)
    col = lax.broadcasted_iota(jnp.int32, (c, c), 1)
    _gla_chunk(qf_ref[0], kf_ref[0], vf_ref[0], gf_ref[0], stf_scr, of_ref, row >= col, c - 1)
    _gla_chunk(qb_ref[0], kb_ref[0], vb_ref[0], gb_ref[0], stb_scr, ob_ref, row <= col, 0)


def _gla_core(q, k, v, gf, gb):
    b, l, _ = q.shape
    c = GLA_CHUNK
    n = l // c

    def fw(w):
        return pl.BlockSpec((1, c, w), lambda bi, i: (bi, i, 0))

    def bw(w):
        return pl.BlockSpec((1, c, w), lambda bi, i: (bi, n - 1 - i, 0))

    return pl.pallas_call(
        _gla_core_kernel,
        grid=(b, n),
        in_specs=[fw(GLA_DK), fw(GLA_DK), fw(GLA_DV), fw(GLA_DK),
                  bw(GLA_DK), bw(GLA_DK), bw(GLA_DV), bw(GLA_DK)],
        out_specs=[fw(GLA_DV), bw(GLA_DV)],
        out_shape=[jax.ShapeDtypeStruct((b, l, GLA_DV), F32),
                   jax.ShapeDtypeStruct((b, l, GLA_DV), F32)],
        scratch_shapes=[pltpu.VMEM((GLA_HEADS, GLA_HEAD_V, GLA_HEAD_K), F32),
                        pltpu.VMEM((GLA_HEADS, GLA_HEAD_V, GLA_HEAD_K), F32)],
        compiler_params=_cparams(("parallel", "arbitrary")),
        name="gla_core",
    )(q, k, v, gf, q, k, v, gb)


def _gla_post_kernel(x_ref, of_ref, ob_ref, r_ref, ng_ref, gt_ref, wo_ref, o_ref):
    o = of_ref[0] + ob_ref[0]
    parts = []
    for h in range(GLA_HEADS):
        oh = o[:, GLA_HEAD_V * h:GLA_HEAD_V * (h + 1)]
        parts.append(oh * lax.rsqrt(jnp.mean(oh * oh, axis=-1, keepdims=True) + EPS))
    on = jnp.concatenate(parts, axis=-1) * ng_ref[...]
    r = r_ref[0]
    gated = on * (r * jax.nn.sigmoid(r))
    o_ref[0] = x_ref[0] + gt_ref[0] * _dot(gated.astype(BF16), wo_ref[...])


def _gla_post(x, o_f, o_b, r, norm_g, gate, w_out):
    b, l, d = x.shape
    tm = 512
    tok = pl.BlockSpec((1, tm, d), lambda bi, i: (bi, i, 0))
    vec = pl.BlockSpec((1, d), lambda bi, i: (0, 0))
    mod = pl.BlockSpec((1, 1, d), lambda bi, i: (bi, 0, 0))
    return pl.pallas_call(
        _gla_post_kernel,
        grid=(b, l // tm),
        in_specs=[tok, tok, tok, tok, vec, mod, pl.BlockSpec((d, d), lambda bi, i: (0, 0))],
        out_specs=tok,
        out_shape=jax.ShapeDtypeStruct((b, l, d), F32),
        compiler_params=_cparams(("parallel", "parallel")),
        name="gla_post",
    )(x, o_f, o_b, r, norm_g.reshape(1, d), gate, w_out)


def _trunk(x, mod, wts):
    b = x.shape[0]

    def mods(layer):
        m = mod[layer].reshape(b, N_MOD, 1, D_MODEL)
        return [m[:, i] for i in range(N_MOD)]

    shift1, scale1, gate1, shift2, scale2, gate2 = mods(0)
    u = _s5_pre(x, wts["norm1_g"][0], scale1, shift1)
    y = _s5_core(u, wts["s5_w1"], wts["s5_w2"], wts["s5_sc"], b)
    x = _s5_post(x, y, wts["norm1_g"][0], scale1, shift1, gate1, wts["s5_d"][0], wts["s5_w_glu"])
    x = _mlp(x, wts["norm2_g"][0], scale2, shift2, gate2, wts["mlp_w1"][0], wts["mlp_w2"][0],
             wts["final_g"], False)
    shift1, scale1, gate1, shift2, scale2, gate2 = mods(1)
    q, k, v, r, gf, gb = _gla_pre(x, wts["norm1_g"][1], scale1, shift1, wts["gla_w_in"],
                                  wts["gla_w_a1"], wts["gla_w_a2"], wts["gla_b_a"])
    o_f, o_b = _gla_core(q, k, v, gf, gb)
    x = _gla_post(x, o_f, o_b, r, wts["gla_norm_g"], gate1, wts["gla_w_out"])
    return _mlp(x, wts["norm2_g"][1], scale2, shift2, gate2, wts["mlp_w1"][1], wts["mlp_w2"][1],
                wts["final_g"], True)


def kernel(x_prompt, x_sample, c_prompt, c_sample, ada_w, ada_b, norm1_g, norm2_g, s5_lam_re, s5_lam_im, s5_log_dt, s5_b_re, s5_b_im, s5_c_re, s5_c_im, s5_d, s5_w_glu, gla_w_in, gla_w_a1, gla_w_a2, gla_b_a, gla_norm_g, gla_w_out, mlp_w1, mlp_w2, final_g):
    bp, bs = c_prompt.shape[0], c_sample.shape[0]
    pad = (-(bp + bs)) % SUBLANES
    c_all = jnp.concatenate([c_prompt, c_sample, jnp.zeros((pad, D_MODEL), F32)], axis=0)
    mod_all = _modulation(c_all, ada_w, ada_b)

    s5_w1, s5_w2, s5_sc = _s5_build(s5_lam_re[0], s5_lam_im[0], s5_log_dt[0], s5_b_re[0],
                                    s5_b_im[0], s5_c_re[0], s5_c_im[0])
    r = GLA_GATE_RANK
    w_a2bd = jnp.zeros((2 * r, 2 * GLA_DK), F32)
    w_a2bd = w_a2bd.at[:r, :GLA_DK].set(gla_w_a2[0, 0]).at[r:, GLA_DK:].set(gla_w_a2[0, 1])
    wts = {
        "norm1_g": norm1_g, "norm2_g": norm2_g, "final_g": final_g,
        "s5_w1": s5_w1, "s5_w2": s5_w2, "s5_sc": s5_sc, "s5_d": s5_d,
        "s5_w_glu": s5_w_glu[0].astype(BF16),
        "mlp_w1": mlp_w1.astype(BF16), "mlp_w2": mlp_w2.astype(BF16),
        "gla_w_in": gla_w_in[0].astype(BF16),
        "gla_w_a1": jnp.concatenate([gla_w_a1[0, 0], gla_w_a1[0, 1]], axis=1).astype(BF16),
        "gla_w_a2": w_a2bd.astype(BF16),
        "gla_b_a": jnp.concatenate([gla_b_a[0, 0], gla_b_a[0, 1]], axis=0).reshape(1, 2 * GLA_DK),
        "gla_norm_g": gla_norm_g[0], "gla_w_out": gla_w_out[0].astype(BF16),
    }
    y_prompt = _trunk(x_prompt, mod_all[:, :bp], wts)
    y_sample = _trunk(x_sample, mod_all[:, bp:bp + bs], wts)
    return (y_prompt, y_sample)
```

```python
import functools
import math

import jax
import jax.numpy as jnp
from jax import lax
from jax.experimental import pallas as pl
from jax.experimental.pallas import tpu as pltpu

F32 = jnp.float32
BF16 = jnp.bfloat16

D_MODEL = 1024
S5_GROUP = 16
S5_GROUPS = D_MODEL // S5_GROUP
S5_STATE = 64
GLA_HEADS = 4
GLA_DK = D_MODEL // 2
GLA_DV = D_MODEL
GLA_HEAD_K = GLA_DK // GLA_HEADS
GLA_HEAD_V = GLA_DV // GLA_HEADS
GLA_GATE_RANK = 16
GLA_GATE_TAU = 16.0
GLA_CHUNK = 64
GLA_BLOCK = 256
D_FF = 4 * D_MODEL
N_MOD = 6
EPS = 1e-6

LANES = 128
SUBLANES = 8
SEGS = LANES // S5_GROUP

S5_C = 32
S5_CW = S5_C * S5_GROUP
S5_NSTATE = 4 * S5_STATE
S5_LEVELS = 3
S5_TBK = 512
S5_PITCH = S5_C + 4

VMEM_LIMIT = 56 * 1024 * 1024


def _cparams(sem):
    return pltpu.CompilerParams(dimension_semantics=sem, vmem_limit_bytes=VMEM_LIMIT)


def _dot(a, b, precision=None):
    return jnp.dot(a, b, preferred_element_type=F32, precision=precision)


def _dot_nt(a, b, precision=None):
    return lax.dot_general(a, b, (((1,), (1,)), ((), ())),
                           preferred_element_type=F32, precision=precision)


def _dot_tn(a, b, precision=None):
    return lax.dot_general(a, b, (((0,), (0,)), ((), ())),
                           preferred_element_type=F32, precision=precision)


def _norm_mod(x, g, scale, shift):
    y = x * lax.rsqrt(jnp.mean(x * x, axis=-1, keepdims=True) + EPS)
    return (y * g) * (1.0 + scale) + shift


def _mod_kernel(c_ref, w_ref, b_ref, o_ref):
    c = c_ref[...]
    s = c * jax.nn.sigmoid(c)
    o_ref[0] = _dot(s.astype(BF16), w_ref[0].astype(BF16)) + b_ref[0]


def _modulation(c_all, ada_w, ada_b):
    depth, d, n = ada_w.shape
    rows = c_all.shape[0]
    tn = 1536
    return pl.pallas_call(
        _mod_kernel,
        grid=(depth, n // tn),
        in_specs=[
            pl.BlockSpec((rows, d), lambda l, j: (0, 0)),
            pl.BlockSpec((1, d, tn), lambda l, j: (l, 0, j)),
            pl.BlockSpec((1, 1, tn), lambda l, j: (l, 0, j)),
        ],
        out_specs=pl.BlockSpec((1, rows, tn), lambda l, j: (l, 0, j)),
        out_shape=jax.ShapeDtypeStruct((depth, rows, n), F32),
        compiler_params=_cparams(("parallel", "parallel")),
        name="adaln_mod",
    )(c_all, ada_w, ada_b.reshape(depth, 1, n))


def _seg_exchange(lo, hi, d, seg):
    keep = (seg & d) == 0
    return (jnp.where(keep, lo, pltpu.roll(hi, S5_GROUP * d, 1)),
            jnp.where(keep, pltpu.roll(lo, LANES - S5_GROUP * d, 1), hi))


def _seg_transpose8(ngroups, load, store, buf_a, buf_b, seg):
    def put_a(n, i, v):
        buf_a[n, i] = v

    def put_b(n, i, v):
        buf_b[n, i] = v

    stages = ((4, load, put_a),
              (2, lambda n, i: buf_a[n, i], put_b),
              (1, lambda n, i: buf_b[n, i], store))
    for d, get, put in stages:
        for n in range(ngroups):
            for i in range(SEGS):
                if i & d == 0:
                    lo, hi = _seg_exchange(get(n, i), get(n, i + d), d, seg)
                    put(n, i, lo)
                    put(n, i + d, hi)


def _seg_buffers(rows):
    ngroups = D_MODEL // LANES * (S5_C // SEGS)
    return [pltpu.VMEM((ngroups, SEGS, rows, LANES), F32) for _ in range(2)]


def _s5_pre_kernel(x_ref, g_ref, sc_ref, sh_ref, u_ref, h_scr, buf_a, buf_b):
    h = _norm_mod(x_ref[0], g_ref[...], sc_ref[0], sh_ref[0])
    nk = S5_TBK // S5_C
    nq = S5_C // SEGS
    for k in range(nk):
        for j in range(D_MODEL // LANES):
            h_scr[j, pl.ds(S5_PITCH * k, S5_C), :] = h[S5_C * k:S5_C * (k + 1), LANES * j:LANES * (j + 1)]
    seg = lax.broadcasted_iota(jnp.int32, (nk, LANES), 1) // S5_GROUP

    def load(n, i):
        return h_scr[n // nq, pl.ds(SEGS * (n % nq) + i, nk, stride=S5_PITCH), :]

    def store(n, gl, v):
        u_ref[SEGS * (n // nq) + gl, :, pl.ds(LANES * (n % nq), LANES)] = v.astype(BF16)

    _seg_transpose8(D_MODEL // LANES * nq, load, store, buf_a, buf_b, seg)


def _s5_pre(x, norm_g, scale, shift):
    b, l, d = x.shape
    nb = l // S5_TBK
    nk = S5_TBK // S5_C
    return pl.pallas_call(
        _s5_pre_kernel,
        grid=(b, nb),
        in_specs=[
            pl.BlockSpec((1, S5_TBK, d), lambda bi, i: (bi, i, 0)),
            pl.BlockSpec((1, d), lambda bi, i: (0, 0)),
            pl.BlockSpec((1, 1, d), lambda bi, i: (bi, 0, 0)),
            pl.BlockSpec((1, 1, d), lambda bi, i: (bi, 0, 0)),
        ],
        out_specs=pl.BlockSpec((S5_GROUPS, nk, S5_CW), lambda bi, i: (0, bi * nb + i, 0)),
        out_shape=jax.ShapeDtypeStruct((S5_GROUPS, b * l // S5_C, S5_CW), BF16),
        scratch_shapes=[pltpu.VMEM((d // LANES, nk * S5_PITCH, LANES), F32)] + _seg_buffers(nk),
        compiler_params=_cparams(("parallel", "parallel")),
        name="s5_pre",
    )(x, norm_g.reshape(1, d), scale, shift)


def _s5_build_kernel(rows_ref, rowsfb_ref, bcat_ref, bswp_ref, ccat_ref, cswp_ref,
                     w1_ref, w2_ref, sc_ref, ex_scr, mat_scr, kt_scr):
    c = S5_C
    cw = S5_CW
    hi = lax.Precision.HIGHEST
    nt = ((c + 1 + SUBLANES - 1) // SUBLANES) * SUBLANES
    lane1 = lax.broadcasted_iota(jnp.int32, (1, LANES), 1)
    sgn = jnp.where(lane1 < S5_STATE, -1.0, 1.0).astype(F32)
    nrow = lax.broadcasted_iota(jnp.int32, (nt, LANES), 0).astype(F32)

    for d in range(2):
        lr = rows_ref[0, d, 0:1, :]
        li = rows_ref[0, d, 1:2, :]
        dt = jnp.exp(rows_ref[0, d, 2:3, :])
        mag = jnp.exp(nrow * (lr * dt))
        ang = nrow * (li * dt)
        p_re = mag * jnp.cos(ang)
        p_im = mag * jnp.sin(ang)
        ab_re = p_re[1:2, :]
        ab_im = p_im[1:2, :]
        den = lr * lr + li * li
        z_re = ((ab_re - 1.0) * lr + ab_im * li) / den
        z_im = (ab_im * lr - (ab_re - 1.0) * li) / den
        bbar = z_re * bcat_ref[0, d] + (z_im * sgn) * bswp_ref[0, d]
        p_is = p_im * sgn
        for n in range(c + 1):
            blk = pl.ds(S5_GROUP * n, S5_GROUP)
            ex_scr[0, blk, :] = jnp.broadcast_to(p_re[n:n + 1, :], (S5_GROUP, LANES))
            ex_scr[1, blk, :] = jnp.broadcast_to(p_is[n:n + 1, :], (S5_GROUP, LANES))
            ex_scr[2, blk, :] = jnp.broadcast_to(p_re[c - n:c - n + 1, :], (S5_GROUP, LANES))
            ex_scr[3, blk, :] = jnp.broadcast_to(p_is[c - n:c - n + 1, :], (S5_GROUP, LANES))
        c_t = jnp.concatenate([ccat_ref[0, d]] * c, axis=0)
        c_s = jnp.concatenate([cswp_ref[0, d]] * c, axis=0)
        bswap = z_re * bswp_ref[0, d] - (z_im * sgn) * bcat_ref[0, d]
        b_t = jnp.concatenate([bbar] * c, axis=0)
        b_s = jnp.concatenate([bswap] * c, axis=0)
        lo = pl.ds(0, cw)
        up = pl.ds(S5_GROUP, cw)
        bneg = bbar * (-sgn)
        if d == 0:
            ct = ex_scr[0, lo, :] * c_t + ex_scr[1, lo, :] * c_s
            kt_scr[0] = _dot_nt(bneg, ct, hi)
            mat_scr[0] = ex_scr[2, up, :] * b_t + ex_scr[3, up, :] * b_s
            mat_scr[2] = ex_scr[0, up, :] * c_t + ex_scr[1, up, :] * c_s
        else:
            ct = ex_scr[2, up, :] * c_t + ex_scr[3, up, :] * c_s
            kt_scr[1] = _dot_nt(bneg, ct, hi)
            mat_scr[1] = ex_scr[0, lo, :] * b_t + ex_scr[1, lo, :] * b_s
            mat_scr[3] = ex_scr[2, lo, :] * c_t + ex_scr[3, lo, :] * c_s

    lane_cw = lax.broadcasted_iota(jnp.int32, (S5_GROUP, cw), 1)
    kf = kt_scr[0]
    kb = kt_scr[1]
    for s in range(c):
        rf = kf if s == 0 else pltpu.roll(kf, S5_GROUP * s, 1)
        rb = kb if s == c - 1 else pltpu.roll(kb, S5_GROUP * (s + 1), 1)
        blk = (jnp.where(lane_cw >= S5_GROUP * s, rf, 0.0)
               + jnp.where(lane_cw < S5_GROUP * (s + 1), rb, 0.0))
        w1_ref[0, pl.ds(S5_GROUP * s, S5_GROUP), pl.ds(0, cw)] = blk.astype(BF16)

    lane_m = lax.broadcasted_iota(jnp.int32, (cw, LANES), 1) < S5_STATE
    bmf = mat_scr[0]
    bmb = mat_scr[1]
    w1_ref[0, :, pl.ds(cw, LANES)] = jnp.where(
        lane_m, bmf, pltpu.roll(bmb, S5_STATE, 1)).astype(BF16)
    w1_ref[0, :, pl.ds(cw + LANES, LANES)] = jnp.where(
        lane_m, pltpu.roll(bmf, S5_STATE, 1), bmb).astype(BF16)
    caf = mat_scr[2]
    cab = mat_scr[3]
    w2t_re = jnp.where(lane_m, caf, pltpu.roll(cab, S5_STATE, 1))
    w2t_im = -jnp.where(lane_m, pltpu.roll(caf, S5_STATE, 1), cab)
    w2_ref[0, pl.ds(0, LANES), :] = w2t_re.T.astype(BF16)
    w2_ref[0, pl.ds(LANES, LANES), :] = w2t_im.T.astype(BF16)

    lrfb = rowsfb_ref[0, 0:1, :]
    lifb = rowsfb_ref[0, 1:2, :]
    dtfb = jnp.exp(rowsfb_ref[0, 2:3, :])
    row8 = lax.broadcasted_iota(jnp.int32, (SUBLANES, LANES), 0)
    fwd8 = lax.broadcasted_iota(jnp.int32, (SUBLANES, LANES), 1) < S5_STATE
    for lvl in range(S5_LEVELS):
        m = float(c * SUBLANES ** lvl)
        for k, dd in enumerate((1, 2, 4, 0)):
            if dd:
                n8 = jnp.full((SUBLANES, LANES), m * dd, F32)
                ok = (fwd8 & (row8 >= dd)) | (jnp.logical_not(fwd8) & (row8 < SUBLANES - dd))
            else:
                n8 = m * jnp.where(fwd8, row8 + 1, SUBLANES - row8).astype(F32)
                ok = row8 >= 0
            mag8 = jnp.exp(n8 * (lrfb * dtfb))
            ang8 = n8 * (lifb * dtfb)
            sc_ref[0, lvl * 8 + 2 * k] = jnp.where(ok, mag8 * jnp.cos(ang8), 0.0)
            sc_ref[0, lvl * 8 + 2 * k + 1] = jnp.where(ok, mag8 * jnp.sin(ang8), 0.0)


def _s5_build(lam_re, lam_im, log_dt, b_re, b_im, c_re, c_im):
    g, p = S5_GROUPS, S5_STATE

    def dup(a):
        return jnp.concatenate([a, a], axis=-1).transpose(1, 0, 2)[:, :, None, :]

    ldt = jnp.broadcast_to(log_dt.T[:, :, None, None], (g, 2, 1, 2 * p))
    rows = jnp.concatenate(
        [dup(lam_re), dup(lam_im), ldt, jnp.zeros((g, 2, SUBLANES - 3, 2 * p), F32)], axis=2)

    def fb(a):
        return jnp.concatenate([a[0], a[1]], axis=-1)[:, None, :]

    ldt2 = jnp.broadcast_to(log_dt[:, :, None], (2, g, p))
    rowsfb = jnp.concatenate(
        [fb(lam_re), fb(lam_im), fb(ldt2), jnp.zeros((g, SUBLANES - 3, 2 * p), F32)], axis=1)
    brt = b_re.transpose(1, 0, 3, 2)
    bit = b_im.transpose(1, 0, 3, 2)
    crt = c_re.transpose(1, 0, 2, 3)
    cit = c_im.transpose(1, 0, 2, 3)
    bcat = jnp.concatenate([brt, bit], axis=-1)
    bswp = jnp.concatenate([bit, brt], axis=-1)
    ccat = jnp.concatenate([crt, cit], axis=-1)
    cswp = jnp.concatenate([cit, crt], axis=-1)
    nsc = S5_LEVELS * 8
    spec4 = pl.BlockSpec((1, 2, S5_GROUP, LANES), lambda i: (i, 0, 0, 0))
    return pl.pallas_call(
        _s5_build_kernel,
        grid=(g,),
        in_specs=[
            pl.BlockSpec((1, 2, SUBLANES, LANES), lambda i: (i, 0, 0, 0)),
            pl.BlockSpec((1, SUBLANES, LANES), lambda i: (i, 0, 0)),
            spec4, spec4, spec4, spec4,
        ],
        out_specs=[
            pl.BlockSpec((1, S5_CW, S5_CW + S5_NSTATE), lambda i: (i, 0, 0)),
            pl.BlockSpec((1, S5_NSTATE, S5_CW), lambda i: (i, 0, 0)),
            pl.BlockSpec((1, nsc, SUBLANES, LANES), lambda i: (i, 0, 0, 0)),
        ],
        out_shape=[
            jax.ShapeDtypeStruct((g, S5_CW, S5_CW + S5_NSTATE), BF16),
            jax.ShapeDtypeStruct((g, S5_NSTATE, S5_CW), BF16),
            jax.ShapeDtypeStruct((g, nsc, SUBLANES, LANES), F32),
        ],
        scratch_shapes=[
            pltpu.VMEM((4, (S5_C + 1) * S5_GROUP, LANES), F32),
            pltpu.VMEM((4, S5_CW, LANES), F32),
            pltpu.VMEM((2, S5_GROUP, S5_CW), F32),
        ],
        compiler_params=_cparams(("parallel",)),
        name="s5_build",
    )(rows, rowsfb, bcat, bswp, ccat, cswp)


def _tile_scan(re, im, sc_ref, lvl, fwd):
    for k, dd in enumerate((1, 2, 4)):
        mr = sc_ref[0, lvl * 8 + 2 * k]
        mi = sc_ref[0, lvl * 8 + 2 * k + 1]
        sre = jnp.where(fwd, pltpu.roll(re, dd, 0), pltpu.roll(re, SUBLANES - dd, 0))
        sim = jnp.where(fwd, pltpu.roll(im, dd, 0), pltpu.roll(im, SUBLANES - dd, 0))
        re, im = re + mr * sre - mi * sim, im + mr * sim + mi * sre
    return re, im


def _scan_rows(bufs, lvl, base, ntiles, sc_ref, p_ref, p_base):
    re_ref, im_ref = bufs[lvl]
    fwd = lax.broadcasted_iota(jnp.int32, (SUBLANES, LANES), 1) < S5_STATE
    row = lax.broadcasted_iota(jnp.int32, (SUBLANES, LANES), 0)
    fwd1 = fwd[0:1, :]

    def carried(hre, him, cre, cim):
        pre = jnp.where(fwd, jnp.where(row == 0, cre, pltpu.roll(hre, 1, 0)),
                        jnp.where(row == SUBLANES - 1, cre, pltpu.roll(hre, SUBLANES - 1, 0)))
        pim = jnp.where(fwd, jnp.where(row == 0, cim, pltpu.roll(him, 1, 0)),
                        jnp.where(row == SUBLANES - 1, cim, pltpu.roll(him, SUBLANES - 1, 0)))
        return pre, pim

    if ntiles == 1:
        sl = pl.ds(base, SUBLANES)
        hre, him = _tile_scan(re_ref[sl, :], im_ref[sl, :], sc_ref, lvl, fwd)
        re_ref[sl, :] = hre
        im_ref[sl, :] = him
        if lvl == 0:
            zero = jnp.zeros((1, LANES), F32)
            pre, pim = carried(hre, him, zero, zero)
            p_ref[pl.ds(p_base, SUBLANES), pl.ds(0, LANES)] = pre
            p_ref[pl.ds(p_base, SUBLANES), pl.ds(LANES, LANES)] = pim
        return

    ere_ref, eim_ref = bufs[lvl + 1]
    ntn = (ntiles + SUBLANES - 1) // SUBLANES
    ere_ref[...] = jnp.zeros_like(ere_ref)
    eim_ref[...] = jnp.zeros_like(eim_ref)

    def local(i, carry):
        sl = pl.ds(pl.multiple_of(base + i * SUBLANES, SUBLANES), SUBLANES)
        hre, him = _tile_scan(re_ref[sl, :], im_ref[sl, :], sc_ref, lvl, fwd)
        re_ref[sl, :] = hre
        im_ref[sl, :] = him
        ere_ref[pl.ds(SUBLANES + i, 1), :] = jnp.where(fwd1, hre[SUBLANES - 1:SUBLANES, :], hre[0:1, :])
        eim_ref[pl.ds(SUBLANES + i, 1), :] = jnp.where(fwd1, him[SUBLANES - 1:SUBLANES, :], him[0:1, :])
        return carry

    lax.fori_loop(0, ntiles, local, 0, unroll=min(ntiles, SUBLANES))
    _scan_rows(bufs, lvl + 1, SUBLANES, ntn, sc_ref, None, 0)
    apr = sc_ref[0, lvl * 8 + 6]
    api = sc_ref[0, lvl * 8 + 7]

    def apply(i, carry):
        sl = pl.ds(pl.multiple_of(base + i * SUBLANES, SUBLANES), SUBLANES)
        cre = jnp.where(fwd1, ere_ref[pl.ds(SUBLANES - 1 + i, 1), :], ere_ref[pl.ds(SUBLANES + 1 + i, 1), :])
        cim = jnp.where(fwd1, eim_ref[pl.ds(SUBLANES - 1 + i, 1), :], eim_ref[pl.ds(SUBLANES + 1 + i, 1), :])
        hre = re_ref[sl, :] + apr * cre - api * cim
        him = im_ref[sl, :] + apr * cim + api * cre
        if lvl == 0:
            psl = pl.ds(pl.multiple_of(p_base + i * SUBLANES, SUBLANES), SUBLANES)
            pre, pim = carried(hre, him, cre, cim)
            p_ref[psl, pl.ds(0, LANES)] = pre
            p_ref[psl, pl.ds(LANES, LANES)] = pim
        else:
            re_ref[sl, :] = hre
            im_ref[sl, :] = him
        return carry

    lax.fori_loop(0, ntiles, apply, 0, unroll=min(ntiles, SUBLANES))


def _s5_core_kernel(nseq, u_ref, w1_ref, w2_ref, sc_ref, y_ref,
                    s_re, s_im, e1_re, e1_im, e2_re, e2_im, p_scr):
    rows = u_ref.shape[1]
    nk = rows // nseq
    u = u_ref[0]
    st = _dot(u, w1_ref[0, :, pl.ds(S5_CW, S5_NSTATE)])
    s_re[...] = st[:, :LANES]
    s_im[...] = st[:, LANES:]
    bufs = [(s_re, s_im), (e1_re, e1_im), (e2_re, e2_im)]
    for b in range(nseq):
        _scan_rows(bufs, 0, b * nk, nk // SUBLANES, sc_ref, p_scr, b * nk)
    y_ref[0] = (_dot(u, w1_ref[0, :, pl.ds(0, S5_CW)])
                + _dot(p_scr[...].astype(BF16), w2_ref[0]))


def _s5_core(u, w1, w2, sc, nseq):
    g, rows, cw = u.shape
    nk = rows // nseq
    nt0 = nk // SUBLANES
    e1_rows = ((nt0 + SUBLANES - 1) // SUBLANES) * SUBLANES + 2 * SUBLANES
    nt1 = (nt0 + SUBLANES - 1) // SUBLANES
    e2_rows = ((nt1 + SUBLANES - 1) // SUBLANES) * SUBLANES + 2 * SUBLANES
    assert nt1 <= SUBLANES * SUBLANES, "sequence too long for S5_LEVELS scan levels"
    return pl.pallas_call(
        functools.partial(_s5_core_kernel, nseq),
        grid=(g,),
        in_specs=[
            pl.BlockSpec((1, rows, cw), lambda i: (i, 0, 0)),
            pl.BlockSpec((1, cw, cw + S5_NSTATE), lambda i: (i, 0, 0)),
            pl.BlockSpec((1, S5_NSTATE, cw), lambda i: (i, 0, 0)),
            pl.BlockSpec((1, S5_LEVELS * 8, SUBLANES, LANES), lambda i: (i, 0, 0, 0)),
        ],
        out_specs=pl.BlockSpec((1, rows, cw), lambda i: (i, 0, 0)),
        out_shape=jax.ShapeDtypeStruct((g, rows, cw), F32),
        scratch_shapes=[
            pltpu.VMEM((rows, LANES), F32), pltpu.VMEM((rows, LANES), F32),
            pltpu.VMEM((e1_rows, LANES), F32), pltpu.VMEM((e1_rows, LANES), F32),
            pltpu.VMEM((e2_rows, LANES), F32), pltpu.VMEM((e2_rows, LANES), F32),
            pltpu.VMEM((rows, 2 * LANES), F32),
        ],
        compiler_params=_cparams(("parallel",)),
        name="s5_core",
    )(u, w1, w2, sc)


def _s5_post_kernel(x_ref, y_ref, g_ref, sc_ref, sh_ref, gt_ref, dsk_ref, wg_ref, o_ref,
                    y_scr, buf_a, buf_b):
    nk = S5_TBK // S5_C
    nq = S5_C // SEGS
    seg = lax.broadcasted_iota(jnp.int32, (nk, LANES), 1) // S5_GROUP

    def load(n, gl):
        return y_ref[SEGS * (n // nq) + gl, :, pl.ds(LANES * (n % nq), LANES)]

    def store(n, i, v):
        y_scr[n // nq, pl.ds(SEGS * (n % nq) + i, nk, stride=S5_PITCH), :] = v

    _seg_transpose8(D_MODEL // LANES * nq, load, store, buf_a, buf_b, seg)
    x = x_ref[0]
    h = _norm_mod(x, g_ref[...], sc_ref[0], sh_ref[0])
    y = jnp.concatenate(
        [jnp.concatenate([y_scr[j, pl.ds(S5_PITCH * k, S5_C), :] for j in range(D_MODEL // LANES)], axis=-1)
         for k in range(nk)], axis=0) + dsk_ref[...] * h
    z = _dot(jax.nn.gelu(y).astype(BF16), wg_ref[...])
    m = z[:, :D_MODEL] * jax.nn.sigmoid(z[:, D_MODEL:])
    o_ref[0] = x + gt_ref[0] * m


def _s5_post(x, y, norm_g, scale, shift, gate, d_skip, w_glu):
    b, l, d = x.shape
    nb = l // S5_TBK
    nk = S5_TBK // S5_C
    vec = pl.BlockSpec((1, d), lambda bi, i: (0, 0))
    mod = pl.BlockSpec((1, 1, d), lambda bi, i: (bi, 0, 0))
    return pl.pallas_call(
        _s5_post_kernel,
        grid=(b, nb),
        in_specs=[
            pl.BlockSpec((1, S5_TBK, d), lambda bi, i: (bi, i, 0)),
            pl.BlockSpec((S5_GROUPS, nk, S5_CW), lambda bi, i: (0, bi * nb + i, 0)),
            vec, mod, mod, mod, vec,
            pl.BlockSpec((d, 2 * d), lambda bi, i: (0, 0)),
        ],
        out_specs=pl.BlockSpec((1, S5_TBK, d), lambda bi, i: (bi, i, 0)),
        out_shape=jax.ShapeDtypeStruct((b, l, d), F32),
        scratch_shapes=[pltpu.VMEM((d // LANES, nk * S5_PITCH, LANES), F32)] + _seg_buffers(nk),
        compiler_params=_cparams(("parallel", "parallel")),
        name="s5_post",
    )(x, y, norm_g.reshape(1, d), scale, shift, gate, d_skip.reshape(1, d), w_glu)


def _mlp_kernel(final, x_ref, g_ref, sc_ref, sh_ref, gt_ref, w1_ref, w2_ref, fg_ref, o_ref,
                h_scr, acc_scr):
    j = pl.program_id(2)

    @pl.when(j == 0)
    def _():
        h_scr[...] = _norm_mod(x_ref[0], g_ref[...], sc_ref[0], sh_ref[0]).astype(BF16)
        acc_scr[...] = jnp.zeros_like(acc_scr)

    a = jnp.maximum(_dot(h_scr[...], w1_ref[...]), 0.0)
    acc_scr[...] += _dot((a * a).astype(BF16), w2_ref[...])

    @pl.when(j == pl.num_programs(2) - 1)
    def _():
        out = x_ref[0] + gt_ref[0] * acc_scr[...]
        if final:
            out = out * lax.rsqrt(jnp.mean(out * out, axis=-1, keepdims=True) + EPS) * fg_ref[...]
        o_ref[0] = out


def _mlp(x, norm_g, scale, shift, gate, w1, w2, final_g, final):
    b, l, d = x.shape
    ff = w1.shape[1]
    tm = min(1024, l)
    tf = 1024
    vec = pl.BlockSpec((1, d), lambda bi, i, j: (0, 0))
    mod = pl.BlockSpec((1, 1, d), lambda bi, i, j: (bi, 0, 0))
    return pl.pallas_call(
        functools.partial(_mlp_kernel, final),
        grid=(b, l // tm, ff // tf),
        in_specs=[
            pl.BlockSpec((1, tm, d), lambda bi, i, j: (bi, i, 0)),
            vec, mod, mod, mod,
            pl.BlockSpec((d, tf), lambda bi, i, j: (0, j)),
            pl.BlockSpec((tf, d), lambda bi, i, j: (j, 0)),
            vec,
        ],
        out_specs=pl.BlockSpec((1, tm, d), lambda bi, i, j: (bi, i, 0)),
        out_shape=jax.ShapeDtypeStruct((b, l, d), F32),
        scratch_shapes=[pltpu.VMEM((tm, d), BF16), pltpu.VMEM((tm, d), F32)],
        compiler_params=_cparams(("parallel", "parallel", "arbitrary")),
        name="mlp",
    )(x, norm_g.reshape(1, d), scale, shift, gate, w1, w2, final_g.reshape(1, d))


def _gla_pre_kernel(x_ref, g_ref, sc_ref, sh_ref, win_ref, wa1_ref, wa2_ref, ba_ref,
                    q_ref, k_ref, v_ref, r_ref, gf_ref, gb_ref):
    h = _norm_mod(x_ref[0], g_ref[...], sc_ref[0], sh_ref[0]).astype(BF16)
    proj = _dot(h, win_ref[...])
    q_ref[0] = proj[:, :GLA_DK] * (GLA_HEAD_K ** -0.5)
    k_ref[0] = proj[:, GLA_DK:2 * GLA_DK]
    v_ref[0] = proj[:, 2 * GLA_DK:2 * GLA_DK + GLA_DV].astype(BF16)
    r_ref[0] = proj[:, 2 * GLA_DK + GLA_DV:]
    a = _dot(h, wa1_ref[...]).astype(BF16)
    z = _dot(a, wa2_ref[...]) + ba_ref[...]
    lg = (jnp.minimum(z, 0.0) - jnp.log1p(jnp.exp(-jnp.abs(z)))) / GLA_GATE_TAU
    gf_ref[0] = lg[:, :GLA_DK]
    gb_ref[0] = lg[:, GLA_DK:]


def _gla_pre(x, norm_g, scale, shift, w_in, w_a1c, w_a2bd, b_ac):
    b, l, d = x.shape
    tm = 512
    vec = pl.BlockSpec((1, d), lambda bi, i: (0, 0))
    mod = pl.BlockSpec((1, 1, d), lambda bi, i: (bi, 0, 0))

    def full(a):
        return pl.BlockSpec(a.shape, lambda bi, i: (0,) * a.ndim)

    def tok(w):
        return pl.BlockSpec((1, tm, w), lambda bi, i: (bi, i, 0))

    return pl.pallas_call(
        _gla_pre_kernel,
        grid=(b, l // tm),
        in_specs=[tok(d), vec, mod, mod, full(w_in), full(w_a1c), full(w_a2bd), full(b_ac)],
        out_specs=[tok(GLA_DK), tok(GLA_DK), tok(GLA_DV), tok(GLA_DV), tok(GLA_DK), tok(GLA_DK)],
        out_shape=[
            jax.ShapeDtypeStruct((b, l, GLA_DK), F32),
            jax.ShapeDtypeStruct((b, l, GLA_DK), F32),
            jax.ShapeDtypeStruct((b, l, GLA_DV), BF16),
            jax.ShapeDtypeStruct((b, l, GLA_DV), F32),
            jax.ShapeDtypeStruct((b, l, GLA_DK), F32),
            jax.ShapeDtypeStruct((b, l, GLA_DK), F32),
        ],
        compiler_params=_cparams(("parallel", "parallel")),
        name="gla_pre",
    )(x, norm_g.reshape(1, d), scale, shift, w_in, w_a1c, w_a2bd, b_ac)


def _gla_block(q, k, v, g, st_ref, o_ref, fwd):
    c = GLA_CHUNK
    nb = GLA_BLOCK
    n = nb // c
    row = lax.broadcasted_iota(jnp.int32, (nb, nb), 0)
    col = lax.broadcasted_iota(jnp.int32, (nb, nb), 1)
    causal = (row >= col) if fwd else (row <= col)
    cum_m = (causal & ((row // c) == (col // c))).astype(BF16)
    g1 = g.astype(BF16)
    r1 = g - g1.astype(F32)
    g2 = r1.astype(BF16)
    g3 = (r1 - g2.astype(F32)).astype(BF16)
    bcum = _dot(cum_m, g1) + _dot(cum_m, g2) + _dot(cum_m, g3)
    last = c - 1 if fwd else 0
    sub = [slice(c * j, c * (j + 1)) for j in range(n)]
    blast = [bcum[c * j + last:c * j + last + 1, :] for j in range(n)]
    order = list(range(n)) if fwd else list(range(n - 1, -1, -1))
    pos = {j: a for a, j in enumerate(order)}

    def between(j, i):
        terms = [blast[m] for m in range(n) if pos[j] < pos[m] < pos[i]]
        return sum(terms[1:], terms[0]) if terms else None

    def before(j):
        terms = [blast[m] for m in range(n) if pos[m] < pos[j]]
        return sum(terms[1:], terms[0]) if terms else None

    def after(j):
        terms = [blast[m] for m in range(n) if pos[m] > pos[j]]
        return sum(terms[1:], terms[0]) if terms else None

    total = sum(blast[1:], blast[0])
    e_pos = jnp.exp(bcum)
    e_neg = jnp.exp(-bcum)
    e_tail = jnp.exp(jnp.concatenate([blast[j] - bcum[sub[j], :] for j in range(n)], axis=0))
    dec = jnp.exp(total)
    for h in range(GLA_HEADS):
        kl = slice(GLA_HEAD_K * h, GLA_HEAD_K * (h + 1))
        vl = slice(GLA_HEAD_V * h, GLA_HEAD_V * (h + 1))
        qd = q[:, kl] * e_pos[:, kl]
        kt = k[:, kl] * e_tail[:, kl]
        qd_b = qd.astype(BF16)
        kd_b = (k[:, kl] * e_neg[:, kl]).astype(BF16)
        kt_b = kt.astype(BF16)
        vh = v[:, vl]

        def scaled(x, e):
            return x if e is None else x * jnp.exp(e[:, kl])

        q_in = jnp.concatenate([scaled(qd[sub[j], :], before(j)) for j in range(n)], axis=0)
        k_out = jnp.concatenate([scaled(kt[sub[j], :], after(j)) for j in range(n)], axis=0)
        rows = []
        for i in range(n):
            keys = []
            for j in range(n):
                if j == i or pos[j] > pos[i]:
                    keys.append(kd_b[sub[j], :])
                elif between(j, i) is None:
                    keys.append(kt_b[sub[j], :])
                else:
                    keys.append(scaled(kt[sub[j], :], between(j, i)).astype(BF16))
            rows.append(_dot_nt(qd_b[sub[i], :], jnp.concatenate(keys, axis=0)))
        scores = jnp.where(causal, jnp.concatenate(rows, axis=0), 0.0)
        s_t = st_ref[h]
        o = _dot(scores.astype(BF16), vh) + _dot_nt(q_in.astype(BF16), s_t.astype(BF16))
        o_ref[0, :, pl.ds(GLA_HEAD_V * h, GLA_HEAD_V)] = o
        st_ref[h] = s_t * dec[:, kl] + _dot_tn(vh, k_out.astype(BF16))


def _gla_core_kernel(qf_ref, kf_ref, vf_ref, gf_ref, qb_ref, kb_ref, vb_ref, gb_ref,
                     of_ref, ob_ref, stf_scr, stb_scr):
    @pl.when(pl.program_id(1) == 0)
    def _():
        stf_scr[...] = jnp.zeros_like(stf_scr)
        stb_scr[...] = jnp.zeros_like(stb_scr)

    _gla_block(qf_ref[0], kf_ref[0], vf_ref[0], gf_ref[0], stf_scr, of_ref, True)
    _gla_block(qb_ref[0], kb_ref[0], vb_ref[0], gb_ref[0], stb_scr, ob_ref, False)


def _gla_core(q, k, v, gf, gb):
    b, l, _ = q.shape
    c = GLA_BLOCK
    n = l // c

    def fw(w):
        return pl.BlockSpec((1, c, w), lambda bi, i: (bi, i, 0))

    def bw(w):
        return pl.BlockSpec((1, c, w), lambda bi, i: (bi, n - 1 - i, 0))

    return pl.pallas_call(
        _gla_core_kernel,
        grid=(b, n),
        in_specs=[fw(GLA_DK), fw(GLA_DK), fw(GLA_DV), fw(GLA_DK),
                  bw(GLA_DK), bw(GLA_DK), bw(GLA_DV), bw(GLA_DK)],
        out_specs=[fw(GLA_DV), bw(GLA_DV)],
        out_shape=[jax.ShapeDtypeStruct((b, l, GLA_DV), F32),
                   jax.ShapeDtypeStruct((b, l, GLA_DV), F32)],
        scratch_shapes=[pltpu.VMEM((GLA_HEADS, GLA_HEAD_V, GLA_HEAD_K), F32),
                        pltpu.VMEM((GLA_HEADS, GLA_HEAD_V, GLA_HEAD_K), F32)],
        compiler_params=_cparams(("parallel", "arbitrary")),
        name="gla_core",
    )(q, k, v, gf, q, k, v, gb)


def _gla_post_kernel(x_ref, of_ref, ob_ref, r_ref, ng_ref, gt_ref, wo_ref, o_ref):
    o = of_ref[0] + ob_ref[0]
    parts = []
    for h in range(GLA_HEADS):
        oh = o[:, GLA_HEAD_V * h:GLA_HEAD_V * (h + 1)]
        parts.append(oh * lax.rsqrt(jnp.mean(oh * oh, axis=-1, keepdims=True) + EPS))
    on = jnp.concatenate(parts, axis=-1) * ng_ref[...]
    r = r_ref[0]
    gated = on * (r * jax.nn.sigmoid(r))
    o_ref[0] = x_ref[0] + gt_ref[0] * _dot(gated.astype(BF16), wo_ref[...])


def _gla_post(x, o_f, o_b, r, norm_g, gate, w_out):
    b, l, d = x.shape
    tm = 512
    tok = pl.BlockSpec((1, tm, d), lambda bi, i: (bi, i, 0))
    vec = pl.BlockSpec((1, d), lambda bi, i: (0, 0))
    mod = pl.BlockSpec((1, 1, d), lambda bi, i: (bi, 0, 0))
    return pl.pallas_call(
        _gla_post_kernel,
        grid=(b, l // tm),
        in_specs=[tok, tok, tok, tok, vec, mod, pl.BlockSpec((d, d), lambda bi, i: (0, 0))],
        out_specs=tok,
        out_shape=jax.ShapeDtypeStruct((b, l, d), F32),
        compiler_params=_cparams(("parallel", "parallel")),
        name="gla_post",
    )(x, o_f, o_b, r, norm_g.reshape(1, d), gate, w_out)


def _trunk(x, mod, wts):
    b = x.shape[0]

    def mods(layer):
        m = mod[layer].reshape(b, N_MOD, 1, D_MODEL)
        return [m[:, i] for i in range(N_MOD)]

    shift1, scale1, gate1, shift2, scale2, gate2 = mods(0)
    u = _s5_pre(x, wts["norm1_g"][0], scale1, shift1)
    y = _s5_core(u, wts["s5_w1"], wts["s5_w2"], wts["s5_sc"], b)
    x = _s5_post(x, y, wts["norm1_g"][0], scale1, shift1, gate1, wts["s5_d"][0], wts["s5_w_glu"])
    x = _mlp(x, wts["norm2_g"][0], scale2, shift2, gate2, wts["mlp_w1"][0], wts["mlp_w2"][0],
             wts["final_g"], False)
    shift1, scale1, gate1, shift2, scale2, gate2 = mods(1)
    q, k, v, r, gf, gb = _gla_pre(x, wts["norm1_g"][1], scale1, shift1, wts["gla_w_in"],
                                  wts["gla_w_a1"], wts["gla_w_a2"], wts["gla_b_a"])
    o_f, o_b = _gla_core(q, k, v, gf, gb)
    x = _gla_post(x, o_f, o_b, r, wts["gla_norm_g"], gate1, wts["gla_w_out"])
    return _mlp(x, wts["norm2_g"][1], scale2, shift2, gate2, wts["mlp_w1"][1], wts["mlp_w2"][1],
                wts["final_g"], True)


def kernel(x_prompt, x_sample, c_prompt, c_sample, ada_w, ada_b, norm1_g, norm2_g, s5_lam_re, s5_lam_im, s5_log_dt, s5_b_re, s5_b_im, s5_c_re, s5_c_im, s5_d, s5_w_glu, gla_w_in, gla_w_a1, gla_w_a2, gla_b_a, gla_norm_g, gla_w_out, mlp_w1, mlp_w2, final_g):
    bp, bs = c_prompt.shape[0], c_sample.shape[0]
    pad = (-(bp + bs)) % SUBLANES
    c_all = jnp.concatenate([c_prompt, c_sample, jnp.zeros((pad, D_MODEL), F32)], axis=0)
    mod_all = _modulation(c_all, ada_w, ada_b)

    s5_w1, s5_w2, s5_sc = _s5_build(s5_lam_re[0], s5_lam_im[0], s5_log_dt[0], s5_b_re[0],
                                    s5_b_im[0], s5_c_re[0], s5_c_im[0])
    r = GLA_GATE_RANK
    w_a2bd = jnp.zeros((2 * r, 2 * GLA_DK), F32)
    w_a2bd = w_a2bd.at[:r, :GLA_DK].set(gla_w_a2[0, 0]).at[r:, GLA_DK:].set(gla_w_a2[0, 1])
    wts = {
        "norm1_g": norm1_g, "norm2_g": norm2_g, "final_g": final_g,
        "s5_w1": s5_w1, "s5_w2": s5_w2, "s5_sc": s5_sc, "s5_d": s5_d,
        "s5_w_glu": s5_w_glu[0].astype(BF16),
        "mlp_w1": mlp_w1.astype(BF16), "mlp_w2": mlp_w2.astype(BF16),
        "gla_w_in": gla_w_in[0].astype(BF16),
        "gla_w_a1": jnp.concatenate([gla_w_a1[0, 0], gla_w_a1[0, 1]], axis=1).astype(BF16),
        "gla_w_a2": w_a2bd.astype(BF16),
        "gla_b_a": jnp.concatenate([gla_b_a[0, 0], gla_b_a[0, 1]], axis=0).reshape(1, 2 * GLA_DK),
        "gla_norm_g": gla_norm_g[0], "gla_w_out": gla_w_out[0].astype(BF16),
    }
    y_prompt = _trunk(x_prompt, mod_all[:, :bp], wts)
    y_sample = _trunk(x_sample, mod_all[:, bp:bp + bs], wts)
    return (y_prompt, y_sample)
```

```python
import functools
import math

import jax
import jax.numpy as jnp
from jax import lax
from jax.experimental import pallas as pl
from jax.experimental.pallas import tpu as pltpu

F32 = jnp.float32
BF16 = jnp.bfloat16

D_MODEL = 1024
S5_GROUP = 16
S5_GROUPS = D_MODEL // S5_GROUP
S5_STATE = 64
GLA_HEADS = 4
GLA_DK = D_MODEL // 2
GLA_DV = D_MODEL
GLA_HEAD_K = GLA_DK // GLA_HEADS
GLA_HEAD_V = GLA_DV // GLA_HEADS
GLA_GATE_RANK = 16
GLA_GATE_TAU = 16.0
GLA_CHUNK = 64
GLA_BLOCK = 256
D_FF = 4 * D_MODEL
N_MOD = 6
EPS = 1e-6

LANES = 128
SUBLANES = 8
SEGS = LANES // S5_GROUP

S5_C = 32
S5_CW = S5_C * S5_GROUP
S5_NSTATE = 4 * S5_STATE
S5_LEVELS = 3
S5_TBK = 512
S5_PITCH = S5_C + 4

VMEM_LIMIT = 56 * 1024 * 1024


def _cparams(sem):
    return pltpu.CompilerParams(dimension_semantics=sem, vmem_limit_bytes=VMEM_LIMIT)


def _dot(a, b, precision=None):
    return jnp.dot(a, b, preferred_element_type=F32, precision=precision)


def _dot_nt(a, b, precision=None):
    return lax.dot_general(a, b, (((1,), (1,)), ((), ())),
                           preferred_element_type=F32, precision=precision)


def _dot_tn(a, b, precision=None):
    return lax.dot_general(a, b, (((0,), (0,)), ((), ())),
                           preferred_element_type=F32, precision=precision)


def _norm_mod(x, g, scale, shift):
    y = x * lax.rsqrt(jnp.mean(x * x, axis=-1, keepdims=True) + EPS)
    return (y * g) * (1.0 + scale) + shift


def _mod_kernel(c_ref, w_ref, b_ref, o_ref):
    c = c_ref[...]
    s = c * jax.nn.sigmoid(c)
    o_ref[0] = _dot(s.astype(BF16), w_ref[0].astype(BF16)) + b_ref[0]


def _modulation(c_all, ada_w, ada_b):
    depth, d, n = ada_w.shape
    rows = c_all.shape[0]
    tn = 1536
    return pl.pallas_call(
        _mod_kernel,
        grid=(depth, n // tn),
        in_specs=[
            pl.BlockSpec((rows, d), lambda l, j: (0, 0)),
            pl.BlockSpec((1, d, tn), lambda l, j: (l, 0, j)),
            pl.BlockSpec((1, 1, tn), lambda l, j: (l, 0, j)),
        ],
        out_specs=pl.BlockSpec((1, rows, tn), lambda l, j: (l, 0, j)),
        out_shape=jax.ShapeDtypeStruct((depth, rows, n), F32),
        compiler_params=_cparams(("parallel", "parallel")),
        name="adaln_mod",
    )(c_all, ada_w, ada_b.reshape(depth, 1, n))


def _seg_exchange(lo, hi, d, seg):
    keep = (seg & d) == 0
    return (jnp.where(keep, lo, pltpu.roll(hi, S5_GROUP * d, 1)),
            jnp.where(keep, pltpu.roll(lo, LANES - S5_GROUP * d, 1), hi))


def _seg_transpose8(ngroups, load, store, buf_a, buf_b, seg):
    def put_a(n, i, v):
        buf_a[n, i] = v

    def put_b(n, i, v):
        buf_b[n, i] = v

    stages = ((4, load, put_a),
              (2, lambda n, i: buf_a[n, i], put_b),
              (1, lambda n, i: buf_b[n, i], store))
    for d, get, put in stages:
        for n in range(ngroups):
            for i in range(SEGS):
                if i & d == 0:
                    lo, hi = _seg_exchange(get(n, i), get(n, i + d), d, seg)
                    put(n, i, lo)
                    put(n, i + d, hi)


def _seg_buffers(rows, dtype):
    ngroups = D_MODEL // LANES * (S5_C // SEGS)
    return [pltpu.VMEM((ngroups, SEGS, rows, LANES), dtype) for _ in range(2)]


def _s5_pre_kernel(x_ref, g_ref, sc_ref, sh_ref, u_ref, h_scr, buf_a, buf_b):
    h = _norm_mod(x_ref[0], g_ref[...], sc_ref[0], sh_ref[0])
    nk = S5_TBK // S5_C
    nq = S5_C // SEGS
    for k in range(nk):
        for j in range(D_MODEL // LANES):
            h_scr[j, pl.ds(S5_PITCH * k, S5_C), :] = h[S5_C * k:S5_C * (k + 1), LANES * j:LANES * (j + 1)]
    seg = lax.broadcasted_iota(jnp.int32, (nk, LANES), 1) // S5_GROUP

    def load(n, i):
        return h_scr[n // nq, pl.ds(SEGS * (n % nq) + i, nk, stride=S5_PITCH), :]

    def store(n, gl, v):
        u_ref[SEGS * (n // nq) + gl, :, pl.ds(LANES * (n % nq), LANES)] = v.astype(BF16)

    _seg_transpose8(D_MODEL // LANES * nq, load, store, buf_a, buf_b, seg)


def _s5_pre(x, norm_g, scale, shift):
    b, l, d = x.shape
    nb = l // S5_TBK
    nk = S5_TBK // S5_C
    return pl.pallas_call(
        _s5_pre_kernel,
        grid=(b, nb),
        in_specs=[
            pl.BlockSpec((1, S5_TBK, d), lambda bi, i: (bi, i, 0)),
            pl.BlockSpec((1, d), lambda bi, i: (0, 0)),
            pl.BlockSpec((1, 1, d), lambda bi, i: (bi, 0, 0)),
            pl.BlockSpec((1, 1, d), lambda bi, i: (bi, 0, 0)),
        ],
        out_specs=pl.BlockSpec((S5_GROUPS, nk, S5_CW), lambda bi, i: (0, bi * nb + i, 0)),
        out_shape=jax.ShapeDtypeStruct((S5_GROUPS, b * l // S5_C, S5_CW), BF16),
        scratch_shapes=[pltpu.VMEM((d // LANES, nk * S5_PITCH, LANES), F32)] + _seg_buffers(nk, F32),
        compiler_params=_cparams(("parallel", "parallel")),
        name="s5_pre",
    )(x, norm_g.reshape(1, d), scale, shift)


def _s5_build_kernel(rows_ref, rowsfb_ref, bcat_ref, bswp_ref, ccat_ref, cswp_ref,
                     w1_ref, w2_ref, sc_ref, ex_scr, mat_scr, kt_scr):
    c = S5_C
    cw = S5_CW
    hi = lax.Precision.HIGHEST
    nt = ((c + 1 + SUBLANES - 1) // SUBLANES) * SUBLANES
    lane1 = lax.broadcasted_iota(jnp.int32, (1, LANES), 1)
    sgn = jnp.where(lane1 < S5_STATE, -1.0, 1.0).astype(F32)
    nrow = lax.broadcasted_iota(jnp.int32, (nt, LANES), 0).astype(F32)

    for d in range(2):
        lr = rows_ref[0, d, 0:1, :]
        li = rows_ref[0, d, 1:2, :]
        dt = jnp.exp(rows_ref[0, d, 2:3, :])
        mag = jnp.exp(nrow * (lr * dt))
        ang = nrow * (li * dt)
        p_re = mag * jnp.cos(ang)
        p_im = mag * jnp.sin(ang)
        ab_re = p_re[1:2, :]
        ab_im = p_im[1:2, :]
        den = lr * lr + li * li
        z_re = ((ab_re - 1.0) * lr + ab_im * li) / den
        z_im = (ab_im * lr - (ab_re - 1.0) * li) / den
        bbar = z_re * bcat_ref[0, d] + (z_im * sgn) * bswp_ref[0, d]
        p_is = p_im * sgn
        for n in range(c + 1):
            blk = pl.ds(S5_GROUP * n, S5_GROUP)
            ex_scr[0, blk, :] = jnp.broadcast_to(p_re[n:n + 1, :], (S5_GROUP, LANES))
            ex_scr[1, blk, :] = jnp.broadcast_to(p_is[n:n + 1, :], (S5_GROUP, LANES))
            ex_scr[2, blk, :] = jnp.broadcast_to(p_re[c - n:c - n + 1, :], (S5_GROUP, LANES))
            ex_scr[3, blk, :] = jnp.broadcast_to(p_is[c - n:c - n + 1, :], (S5_GROUP, LANES))
        c_t = jnp.concatenate([ccat_ref[0, d]] * c, axis=0)
        c_s = jnp.concatenate([cswp_ref[0, d]] * c, axis=0)
        bswap = z_re * bswp_ref[0, d] - (z_im * sgn) * bcat_ref[0, d]
        b_t = jnp.concatenate([bbar] * c, axis=0)
        b_s = jnp.concatenate([bswap] * c, axis=0)
        lo = pl.ds(0, cw)
        up = pl.ds(S5_GROUP, cw)
        bneg = bbar * (-sgn)
        if d == 0:
            ct = ex_scr[0, lo, :] * c_t + ex_scr[1, lo, :] * c_s
            kt_scr[0] = _dot_nt(bneg, ct, hi)
            mat_scr[0] = ex_scr[2, up, :] * b_t + ex_scr[3, up, :] * b_s
            mat_scr[2] = ex_scr[0, up, :] * c_t + ex_scr[1, up, :] * c_s
        else:
            ct = ex_scr[2, up, :] * c_t + ex_scr[3, up, :] * c_s
            kt_scr[1] = _dot_nt(bneg, ct, hi)
            mat_scr[1] = ex_scr[0, lo, :] * b_t + ex_scr[1, lo, :] * b_s
            mat_scr[3] = ex_scr[2, lo, :] * c_t + ex_scr[3, lo, :] * c_s

    lane_cw = lax.broadcasted_iota(jnp.int32, (S5_GROUP, cw), 1)
    kf = kt_scr[0]
    kb = kt_scr[1]
    for s in range(c):
        rf = kf if s == 0 else pltpu.roll(kf, S5_GROUP * s, 1)
        rb = kb if s == c - 1 else pltpu.roll(kb, S5_GROUP * (s + 1), 1)
        blk = (jnp.where(lane_cw >= S5_GROUP * s, rf, 0.0)
               + jnp.where(lane_cw < S5_GROUP * (s + 1), rb, 0.0))
        w1_ref[0, pl.ds(S5_GROUP * s, S5_GROUP), pl.ds(0, cw)] = blk.astype(BF16)

    lane_m = lax.broadcasted_iota(jnp.int32, (cw, LANES), 1) < S5_STATE
    bmf = mat_scr[0]
    bmb = mat_scr[1]
    w1_ref[0, :, pl.ds(cw, LANES)] = jnp.where(
        lane_m, bmf, pltpu.roll(bmb, S5_STATE, 1)).astype(BF16)
    w1_ref[0, :, pl.ds(cw + LANES, LANES)] = jnp.where(
        lane_m, pltpu.roll(bmf, S5_STATE, 1), bmb).astype(BF16)
    caf = mat_scr[2]
    cab = mat_scr[3]
    w2t_re = jnp.where(lane_m, caf, pltpu.roll(cab, S5_STATE, 1))
    w2t_im = -jnp.where(lane_m, pltpu.roll(caf, S5_STATE, 1), cab)
    w2_ref[0, pl.ds(0, LANES), :] = w2t_re.T.astype(BF16)
    w2_ref[0, pl.ds(LANES, LANES), :] = w2t_im.T.astype(BF16)

    lrfb = rowsfb_ref[0, 0:1, :]
    lifb = rowsfb_ref[0, 1:2, :]
    dtfb = jnp.exp(rowsfb_ref[0, 2:3, :])
    row8 = lax.broadcasted_iota(jnp.int32, (SUBLANES, LANES), 0)
    fwd8 = lax.broadcasted_iota(jnp.int32, (SUBLANES, LANES), 1) < S5_STATE
    for lvl in range(S5_LEVELS):
        m = float(c * SUBLANES ** lvl)
        for k, dd in enumerate((1, 2, 4, 0)):
            if dd:
                n8 = jnp.full((SUBLANES, LANES), m * dd, F32)
                ok = (fwd8 & (row8 >= dd)) | (jnp.logical_not(fwd8) & (row8 < SUBLANES - dd))
            else:
                n8 = m * jnp.where(fwd8, row8 + 1, SUBLANES - row8).astype(F32)
                ok = row8 >= 0
            mag8 = jnp.exp(n8 * (lrfb * dtfb))
            ang8 = n8 * (lifb * dtfb)
            sc_ref[0, lvl * 8 + 2 * k] = jnp.where(ok, mag8 * jnp.cos(ang8), 0.0)
            sc_ref[0, lvl * 8 + 2 * k + 1] = jnp.where(ok, mag8 * jnp.sin(ang8), 0.0)


def _s5_build(lam_re, lam_im, log_dt, b_re, b_im, c_re, c_im):
    g, p = S5_GROUPS, S5_STATE

    def dup(a):
        return jnp.concatenate([a, a], axis=-1).transpose(1, 0, 2)[:, :, None, :]

    ldt = jnp.broadcast_to(log_dt.T[:, :, None, None], (g, 2, 1, 2 * p))
    rows = jnp.concatenate(
        [dup(lam_re), dup(lam_im), ldt, jnp.zeros((g, 2, SUBLANES - 3, 2 * p), F32)], axis=2)

    def fb(a):
        return jnp.concatenate([a[0], a[1]], axis=-1)[:, None, :]

    ldt2 = jnp.broadcast_to(log_dt[:, :, None], (2, g, p))
    rowsfb = jnp.concatenate(
        [fb(lam_re), fb(lam_im), fb(ldt2), jnp.zeros((g, SUBLANES - 3, 2 * p), F32)], axis=1)
    brt = b_re.transpose(1, 0, 3, 2)
    bit = b_im.transpose(1, 0, 3, 2)
    crt = c_re.transpose(1, 0, 2, 3)
    cit = c_im.transpose(1, 0, 2, 3)
    bcat = jnp.concatenate([brt, bit], axis=-1)
    bswp = jnp.concatenate([bit, brt], axis=-1)
    ccat = jnp.concatenate([crt, cit], axis=-1)
    cswp = jnp.concatenate([cit, crt], axis=-1)
    nsc = S5_LEVELS * 8
    spec4 = pl.BlockSpec((1, 2, S5_GROUP, LANES), lambda i: (i, 0, 0, 0))
    return pl.pallas_call(
        _s5_build_kernel,
        grid=(g,),
        in_specs=[
            pl.BlockSpec((1, 2, SUBLANES, LANES), lambda i: (i, 0, 0, 0)),
            pl.BlockSpec((1, SUBLANES, LANES), lambda i: (i, 0, 0)),
            spec4, spec4, spec4, spec4,
        ],
        out_specs=[
            pl.BlockSpec((1, S5_CW, S5_CW + S5_NSTATE), lambda i: (i, 0, 0)),
            pl.BlockSpec((1, S5_NSTATE, S5_CW), lambda i: (i, 0, 0)),
            pl.BlockSpec((1, nsc, SUBLANES, LANES), lambda i: (i, 0, 0, 0)),
        ],
        out_shape=[
            jax.ShapeDtypeStruct((g, S5_CW, S5_CW + S5_NSTATE), BF16),
            jax.ShapeDtypeStruct((g, S5_NSTATE, S5_CW), BF16),
            jax.ShapeDtypeStruct((g, nsc, SUBLANES, LANES), F32),
        ],
        scratch_shapes=[
            pltpu.VMEM((4, (S5_C + 1) * S5_GROUP, LANES), F32),
            pltpu.VMEM((4, S5_CW, LANES), F32),
            pltpu.VMEM((2, S5_GROUP, S5_CW), F32),
        ],
        compiler_params=_cparams(("parallel",)),
        name="s5_build",
    )(rows, rowsfb, bcat, bswp, ccat, cswp)


def _tile_scan(re, im, sc_ref, lvl, fwd):
    for k, dd in enumerate((1, 2, 4)):
        mr = sc_ref[0, lvl * 8 + 2 * k]
        mi = sc_ref[0, lvl * 8 + 2 * k + 1]
        sre = jnp.where(fwd, pltpu.roll(re, dd, 0), pltpu.roll(re, SUBLANES - dd, 0))
        sim = jnp.where(fwd, pltpu.roll(im, dd, 0), pltpu.roll(im, SUBLANES - dd, 0))
        re, im = re + mr * sre - mi * sim, im + mr * sim + mi * sre
    return re, im


def _scan_rows(bufs, lvl, base, ntiles, sc_ref, p_ref, p_base):
    re_ref, im_ref = bufs[lvl]
    fwd = lax.broadcasted_iota(jnp.int32, (SUBLANES, LANES), 1) < S5_STATE
    row = lax.broadcasted_iota(jnp.int32, (SUBLANES, LANES), 0)
    fwd1 = fwd[0:1, :]

    def carried(hre, him, cre, cim):
        pre = jnp.where(fwd, jnp.where(row == 0, cre, pltpu.roll(hre, 1, 0)),
                        jnp.where(row == SUBLANES - 1, cre, pltpu.roll(hre, SUBLANES - 1, 0)))
        pim = jnp.where(fwd, jnp.where(row == 0, cim, pltpu.roll(him, 1, 0)),
                        jnp.where(row == SUBLANES - 1, cim, pltpu.roll(him, SUBLANES - 1, 0)))
        return pre, pim

    if ntiles == 1:
        sl = pl.ds(base, SUBLANES)
        hre, him = _tile_scan(re_ref[sl, :], im_ref[sl, :], sc_ref, lvl, fwd)
        re_ref[sl, :] = hre
        im_ref[sl, :] = him
        if lvl == 0:
            zero = jnp.zeros((1, LANES), F32)
            pre, pim = carried(hre, him, zero, zero)
            p_ref[pl.ds(p_base, SUBLANES), pl.ds(0, LANES)] = pre
            p_ref[pl.ds(p_base, SUBLANES), pl.ds(LANES, LANES)] = pim
        return

    ere_ref, eim_ref = bufs[lvl + 1]
    ntn = (ntiles + SUBLANES - 1) // SUBLANES
    ere_ref[...] = jnp.zeros_like(ere_ref)
    eim_ref[...] = jnp.zeros_like(eim_ref)

    def local(i, carry):
        sl = pl.ds(pl.multiple_of(base + i * SUBLANES, SUBLANES), SUBLANES)
        hre, him = _tile_scan(re_ref[sl, :], im_ref[sl, :], sc_ref, lvl, fwd)
        re_ref[sl, :] = hre
        im_ref[sl, :] = him
        ere_ref[pl.ds(SUBLANES + i, 1), :] = jnp.where(fwd1, hre[SUBLANES - 1:SUBLANES, :], hre[0:1, :])
        eim_ref[pl.ds(SUBLANES + i, 1), :] = jnp.where(fwd1, him[SUBLANES - 1:SUBLANES, :], him[0:1, :])
        return carry

    lax.fori_loop(0, ntiles, local, 0, unroll=min(ntiles, SUBLANES))
    _scan_rows(bufs, lvl + 1, SUBLANES, ntn, sc_ref, None, 0)
    apr = sc_ref[0, lvl * 8 + 6]
    api = sc_ref[0, lvl * 8 + 7]

    def apply(i, carry):
        sl = pl.ds(pl.multiple_of(base + i * SUBLANES, SUBLANES), SUBLANES)
        cre = jnp.where(fwd1, ere_ref[pl.ds(SUBLANES - 1 + i, 1), :], ere_ref[pl.ds(SUBLANES + 1 + i, 1), :])
        cim = jnp.where(fwd1, eim_ref[pl.ds(SUBLANES - 1 + i, 1), :], eim_ref[pl.ds(SUBLANES + 1 + i, 1), :])
        hre = re_ref[sl, :] + apr * cre - api * cim
        him = im_ref[sl, :] + apr * cim + api * cre
        if lvl == 0:
            psl = pl.ds(pl.multiple_of(p_base + i * SUBLANES, SUBLANES), SUBLANES)
            pre, pim = carried(hre, him, cre, cim)
            p_ref[psl, pl.ds(0, LANES)] = pre
            p_ref[psl, pl.ds(LANES, LANES)] = pim
        else:
            re_ref[sl, :] = hre
            im_ref[sl, :] = him
        return carry

    lax.fori_loop(0, ntiles, apply, 0, unroll=min(ntiles, SUBLANES))


def _s5_core_kernel(nseq, u_ref, w1_ref, w2_ref, sc_ref, y_ref,
                    s_re, s_im, e1_re, e1_im, e2_re, e2_im, p_scr):
    rows = u_ref.shape[1]
    nk = rows // nseq
    u = u_ref[0]
    st = _dot(u, w1_ref[0, :, pl.ds(S5_CW, S5_NSTATE)])
    s_re[...] = st[:, :LANES]
    s_im[...] = st[:, LANES:]
    bufs = [(s_re, s_im), (e1_re, e1_im), (e2_re, e2_im)]
    for b in range(nseq):
        _scan_rows(bufs, 0, b * nk, nk // SUBLANES, sc_ref, p_scr, b * nk)
    y_ref[0] = (_dot(u, w1_ref[0, :, pl.ds(0, S5_CW)])
                + _dot(p_scr[...].astype(BF16), w2_ref[0])).astype(BF16)


def _s5_core(u, w1, w2, sc, nseq):
    g, rows, cw = u.shape
    nk = rows // nseq
    nt0 = nk // SUBLANES
    e1_rows = ((nt0 + SUBLANES - 1) // SUBLANES) * SUBLANES + 2 * SUBLANES
    nt1 = (nt0 + SUBLANES - 1) // SUBLANES
    e2_rows = ((nt1 + SUBLANES - 1) // SUBLANES) * SUBLANES + 2 * SUBLANES
    assert nt1 <= SUBLANES * SUBLANES, "sequence too long for S5_LEVELS scan levels"
    return pl.pallas_call(
        functools.partial(_s5_core_kernel, nseq),
        grid=(g,),
        in_specs=[
            pl.BlockSpec((1, rows, cw), lambda i: (i, 0, 0)),
            pl.BlockSpec((1, cw, cw + S5_NSTATE), lambda i: (i, 0, 0)),
            pl.BlockSpec((1, S5_NSTATE, cw), lambda i: (i, 0, 0)),
            pl.BlockSpec((1, S5_LEVELS * 8, SUBLANES, LANES), lambda i: (i, 0, 0, 0)),
        ],
        out_specs=pl.BlockSpec((1, rows, cw), lambda i: (i, 0, 0)),
        out_shape=jax.ShapeDtypeStruct((g, rows, cw), BF16),
        scratch_shapes=[
            pltpu.VMEM((rows, LANES), F32), pltpu.VMEM((rows, LANES), F32),
            pltpu.VMEM((e1_rows, LANES), F32), pltpu.VMEM((e1_rows, LANES), F32),
            pltpu.VMEM((e2_rows, LANES), F32), pltpu.VMEM((e2_rows, LANES), F32),
            pltpu.VMEM((rows, 2 * LANES), F32),
        ],
        compiler_params=_cparams(("parallel",)),
        name="s5_core",
    )(u, w1, w2, sc)


def _s5_post_kernel(x_ref, y_ref, g_ref, sc_ref, sh_ref, gt_ref, dsk_ref, wg_ref, o_ref,
                    y_scr, buf_a, buf_b):
    nk = S5_TBK // S5_C
    nq = S5_C // SEGS
    seg = lax.broadcasted_iota(jnp.int32, (nk, LANES), 1) // S5_GROUP

    def load(n, gl):
        return y_ref[SEGS * (n // nq) + gl, :, pl.ds(LANES * (n % nq), LANES)]

    def store(n, i, v):
        y_scr[n // nq, pl.ds(SEGS * (n % nq) + i, nk, stride=S5_PITCH), :] = v.astype(F32)

    _seg_transpose8(D_MODEL // LANES * nq, load, store, buf_a, buf_b, seg)
    x = x_ref[0]
    h = _norm_mod(x, g_ref[...], sc_ref[0], sh_ref[0])
    y = jnp.concatenate(
        [jnp.concatenate([y_scr[j, pl.ds(S5_PITCH * k, S5_C), :] for j in range(D_MODEL // LANES)], axis=-1)
         for k in range(nk)], axis=0) + dsk_ref[...] * h
    z = _dot(jax.nn.gelu(y).astype(BF16), wg_ref[...])
    m = z[:, :D_MODEL] * jax.nn.sigmoid(z[:, D_MODEL:])
    o_ref[0] = x + gt_ref[0] * m


def _s5_post(x, y, norm_g, scale, shift, gate, d_skip, w_glu):
    b, l, d = x.shape
    nb = l // S5_TBK
    nk = S5_TBK // S5_C
    vec = pl.BlockSpec((1, d), lambda bi, i: (0, 0))
    mod = pl.BlockSpec((1, 1, d), lambda bi, i: (bi, 0, 0))
    return pl.pallas_call(
        _s5_post_kernel,
        grid=(b, nb),
        in_specs=[
            pl.BlockSpec((1, S5_TBK, d), lambda bi, i: (bi, i, 0)),
            pl.BlockSpec((S5_GROUPS, nk, S5_CW), lambda bi, i: (0, bi * nb + i, 0)),
            vec, mod, mod, mod, vec,
            pl.BlockSpec((d, 2 * d), lambda bi, i: (0, 0)),
        ],
        out_specs=pl.BlockSpec((1, S5_TBK, d), lambda bi, i: (bi, i, 0)),
        out_shape=jax.ShapeDtypeStruct((b, l, d), F32),
        scratch_shapes=[pltpu.VMEM((d // LANES, nk * S5_PITCH, LANES), F32)] + _seg_buffers(nk, BF16),
        compiler_params=_cparams(("parallel", "parallel")),
        name="s5_post",
    )(x, y, norm_g.reshape(1, d), scale, shift, gate, d_skip.reshape(1, d), w_glu)


def _mlp_kernel(final, x_ref, g_ref, sc_ref, sh_ref, gt_ref, w1_ref, w2_ref, fg_ref, o_ref,
                h_scr, acc_scr):
    j = pl.program_id(2)

    @pl.when(j == 0)
    def _():
        h_scr[...] = _norm_mod(x_ref[0], g_ref[...], sc_ref[0], sh_ref[0]).astype(BF16)
        acc_scr[...] = jnp.zeros_like(acc_scr)

    a = jnp.maximum(_dot(h_scr[...], w1_ref[...]), 0.0)
    acc_scr[...] += _dot((a * a).astype(BF16), w2_ref[...])

    @pl.when(j == pl.num_programs(2) - 1)
    def _():
        out = x_ref[0] + gt_ref[0] * acc_scr[...]
        if final:
            out = out * lax.rsqrt(jnp.mean(out * out, axis=-1, keepdims=True) + EPS) * fg_ref[...]
        o_ref[0] = out


def _mlp(x, norm_g, scale, shift, gate, w1, w2, final_g, final):
    b, l, d = x.shape
    ff = w1.shape[1]
    tm = min(1024, l)
    tf = 1024
    vec = pl.BlockSpec((1, d), lambda bi, i, j: (0, 0))
    mod = pl.BlockSpec((1, 1, d), lambda bi, i, j: (bi, 0, 0))
    return pl.pallas_call(
        functools.partial(_mlp_kernel, final),
        grid=(b, l // tm, ff // tf),
        in_specs=[
            pl.BlockSpec((1, tm, d), lambda bi, i, j: (bi, i, 0)),
            vec, mod, mod, mod,
            pl.BlockSpec((d, tf), lambda bi, i, j: (0, j)),
            pl.BlockSpec((tf, d), lambda bi, i, j: (j, 0)),
            vec,
        ],
        out_specs=pl.BlockSpec((1, tm, d), lambda bi, i, j: (bi, i, 0)),
        out_shape=jax.ShapeDtypeStruct((b, l, d), F32),
        scratch_shapes=[pltpu.VMEM((tm, d), BF16), pltpu.VMEM((tm, d), F32)],
        compiler_params=_cparams(("parallel", "parallel", "arbitrary")),
        name="mlp",
    )(x, norm_g.reshape(1, d), scale, shift, gate, w1, w2, final_g.reshape(1, d))


def _gla_pre_kernel(x_ref, g_ref, sc_ref, sh_ref, win_ref, wa1_ref, wa2_ref, ba_ref,
                    q_ref, k_ref, v_ref, r_ref, gf_ref, gb_ref):
    h = _norm_mod(x_ref[0], g_ref[...], sc_ref[0], sh_ref[0]).astype(BF16)
    proj = _dot(h, win_ref[...])
    q_ref[0] = proj[:, :GLA_DK] * (GLA_HEAD_K ** -0.5)
    k_ref[0] = proj[:, GLA_DK:2 * GLA_DK]
    v_ref[0] = proj[:, 2 * GLA_DK:2 * GLA_DK + GLA_DV].astype(BF16)
    r_ref[0] = proj[:, 2 * GLA_DK + GLA_DV:].astype(BF16)
    a = _dot(h, wa1_ref[...]).astype(BF16)
    z = _dot(a, wa2_ref[...]) + ba_ref[...]
    lg = (jnp.minimum(z, 0.0) - jnp.log1p(jnp.exp(-jnp.abs(z)))) / GLA_GATE_TAU
    gf_ref[0] = lg[:, :GLA_DK]
    gb_ref[0] = lg[:, GLA_DK:]


def _gla_pre(x, norm_g, scale, shift, w_in, w_a1c, w_a2bd, b_ac):
    b, l, d = x.shape
    tm = 512
    vec = pl.BlockSpec((1, d), lambda bi, i: (0, 0))
    mod = pl.BlockSpec((1, 1, d), lambda bi, i: (bi, 0, 0))

    def full(a):
        return pl.BlockSpec(a.shape, lambda bi, i: (0,) * a.ndim)

    def tok(w):
        return pl.BlockSpec((1, tm, w), lambda bi, i: (bi, i, 0))

    return pl.pallas_call(
        _gla_pre_kernel,
        grid=(b, l // tm),
        in_specs=[tok(d), vec, mod, mod, full(w_in), full(w_a1c), full(w_a2bd), full(b_ac)],
        out_specs=[tok(GLA_DK), tok(GLA_DK), tok(GLA_DV), tok(GLA_DV), tok(GLA_DK), tok(GLA_DK)],
        out_shape=[
            jax.ShapeDtypeStruct((b, l, GLA_DK), F32),
            jax.ShapeDtypeStruct((b, l, GLA_DK), F32),
            jax.ShapeDtypeStruct((b, l, GLA_DV), BF16),
            jax.ShapeDtypeStruct((b, l, GLA_DV), BF16),
            jax.ShapeDtypeStruct((b, l, GLA_DK), F32),
            jax.ShapeDtypeStruct((b, l, GLA_DK), F32),
        ],
        compiler_params=_cparams(("parallel", "parallel")),
        name="gla_pre",
    )(x, norm_g.reshape(1, d), scale, shift, w_in, w_a1c, w_a2bd, b_ac)


def _gla_block(q_ref, k_ref, v_ref, g_ref, st_ref, bc_ref, o_ref, fwd):
    c = GLA_CHUNK
    nb = GLA_BLOCK
    n = nb // c
    row = lax.broadcasted_iota(jnp.int32, (nb, nb), 0)
    col = lax.broadcasted_iota(jnp.int32, (nb, nb), 1)
    causal = (row >= col) if fwd else (row <= col)
    cum_m = (causal & ((row // c) == (col // c))).astype(BF16)
    g = g_ref[0]
    g1 = g.astype(BF16)
    r1 = g - g1.astype(F32)
    g2 = r1.astype(BF16)
    g3 = (r1 - g2.astype(F32)).astype(BF16)
    bc_ref[...] = _dot(cum_m, g1) + _dot(cum_m, g2) + _dot(cum_m, g3)
    last = c - 1 if fwd else 0
    sub = [slice(c * j, c * (j + 1)) for j in range(n)]
    order = list(range(n)) if fwd else list(range(n - 1, -1, -1))
    pos = {j: a for a, j in enumerate(order)}
    for h in range(GLA_HEADS):
        kl = pl.ds(GLA_HEAD_K * h, GLA_HEAD_K)
        bcum = bc_ref[:, kl]
        blast = [bcum[c * j + last:c * j + last + 1, :] for j in range(n)]

        def total(select):
            terms = [blast[m] for m in range(n) if select(m)]
            return sum(terms[1:], terms[0]) if terms else None

        def between(j, i):
            return total(lambda m: pos[j] < pos[m] < pos[i])

        def scaled(x, e):
            return x if e is None else x * jnp.exp(e)

        qh = q_ref[0, :, kl]
        kh = k_ref[0, :, kl]
        qd = qh * jnp.exp(bcum)
        kt = kh * jnp.exp(jnp.concatenate([blast[j] - bcum[sub[j], :] for j in range(n)], axis=0))
        qd_b = qd.astype(BF16)
        kd_b = (kh * jnp.exp(-bcum)).astype(BF16)
        kt_b = kt.astype(BF16)
        vh = v_ref[0, :, pl.ds(GLA_HEAD_V * h, GLA_HEAD_V)]
        dec = jnp.exp(total(lambda m: True))

        def before(j):
            return total(lambda m: pos[m] < pos[j])

        def after(j):
            return total(lambda m: pos[m] > pos[j])

        q_in = jnp.concatenate([scaled(qd[sub[j], :], before(j)) for j in range(n)], axis=0)
        k_out = jnp.concatenate([scaled(kt[sub[j], :], after(j)) for j in range(n)], axis=0)
        rows = []
        for i in range(n):
            keys = []
            for j in range(n):
                if j == i or pos[j] > pos[i]:
                    keys.append(kd_b[sub[j], :])
                elif between(j, i) is None:
                    keys.append(kt_b[sub[j], :])
                else:
                    keys.append(scaled(kt[sub[j], :], between(j, i)).astype(BF16))
            rows.append(_dot_nt(qd_b[sub[i], :], jnp.concatenate(keys, axis=0)))
        scores = jnp.where(causal, jnp.concatenate(rows, axis=0), 0.0)
        s_t = st_ref[h]
        o = _dot(scores.astype(BF16), vh) + _dot_nt(q_in.astype(BF16), s_t.astype(BF16))
        o_ref[0, :, pl.ds(GLA_HEAD_V * h, GLA_HEAD_V)] = o.astype(BF16)
        st_ref[h] = s_t * dec + _dot_tn(vh, k_out.astype(BF16))


def _gla_core_kernel(qf_ref, kf_ref, vf_ref, gf_ref, qb_ref, kb_ref, vb_ref, gb_ref,
                     of_ref, ob_ref, stf_scr, stb_scr, bcf_scr, bcb_scr):
    @pl.when(pl.program_id(1) == 0)
    def _():
        stf_scr[...] = jnp.zeros_like(stf_scr)
        stb_scr[...] = jnp.zeros_like(stb_scr)

    _gla_block(qf_ref, kf_ref, vf_ref, gf_ref, stf_scr, bcf_scr, of_ref, True)
    _gla_block(qb_ref, kb_ref, vb_ref, gb_ref, stb_scr, bcb_scr, ob_ref, False)


def _gla_core(q, k, v, gf, gb):
    b, l, _ = q.shape
    c = GLA_BLOCK
    n = l // c

    def fw(w):
        return pl.BlockSpec((1, c, w), lambda bi, i: (bi, i, 0))

    def bw(w):
        return pl.BlockSpec((1, c, w), lambda bi, i: (bi, n - 1 - i, 0))

    return pl.pallas_call(
        _gla_core_kernel,
        grid=(b, n),
        in_specs=[fw(GLA_DK), fw(GLA_DK), fw(GLA_DV), fw(GLA_DK),
                  bw(GLA_DK), bw(GLA_DK), bw(GLA_DV), bw(GLA_DK)],
        out_specs=[fw(GLA_DV), bw(GLA_DV)],
        out_shape=[jax.ShapeDtypeStruct((b, l, GLA_DV), BF16),
                   jax.ShapeDtypeStruct((b, l, GLA_DV), BF16)],
        scratch_shapes=[pltpu.VMEM((GLA_HEADS, GLA_HEAD_V, GLA_HEAD_K), F32),
                        pltpu.VMEM((GLA_HEADS, GLA_HEAD_V, GLA_HEAD_K), F32),
                        pltpu.VMEM((c, GLA_DK), F32), pltpu.VMEM((c, GLA_DK), F32)],
        compiler_params=_cparams(("parallel", "arbitrary")),
        name="gla_core",
    )(q, k, v, gf, q, k, v, gb)


def _gla_post_kernel(x_ref, of_ref, ob_ref, r_ref, ng_ref, gt_ref, wo_ref, o_ref):
    o = of_ref[0].astype(F32) + ob_ref[0].astype(F32)
    parts = []
    for h in range(GLA_HEADS):
        oh = o[:, GLA_HEAD_V * h:GLA_HEAD_V * (h + 1)]
        parts.append(oh * lax.rsqrt(jnp.mean(oh * oh, axis=-1, keepdims=True) + EPS))
    on = jnp.concatenate(parts, axis=-1) * ng_ref[...]
    r = r_ref[0].astype(F32)
    gated = on * (r * jax.nn.sigmoid(r))
    o_ref[0] = x_ref[0] + gt_ref[0] * _dot(gated.astype(BF16), wo_ref[...])


def _gla_post(x, o_f, o_b, r, norm_g, gate, w_out):
    b, l, d = x.shape
    tm = 512
    tok = pl.BlockSpec((1, tm, d), lambda bi, i: (bi, i, 0))
    vec = pl.BlockSpec((1, d), lambda bi, i: (0, 0))
    mod = pl.BlockSpec((1, 1, d), lambda bi, i: (bi, 0, 0))
    return pl.pallas_call(
        _gla_post_kernel,
        grid=(b, l // tm),
        in_specs=[tok, tok, tok, tok, vec, mod, pl.BlockSpec((d, d), lambda bi, i: (0, 0))],
        out_specs=tok,
        out_shape=jax.ShapeDtypeStruct((b, l, d), F32),
        compiler_params=_cparams(("parallel", "parallel")),
        name="gla_post",
    )(x, o_f, o_b, r, norm_g.reshape(1, d), gate, w_out)


def _trunk(x, mod, wts):
    b = x.shape[0]

    def mods(layer):
        m = mod[layer].reshape(b, N_MOD, 1, D_MODEL)
        return [m[:, i] for i in range(N_MOD)]

    shift1, scale1, gate1, shift2, scale2, gate2 = mods(0)
    u = _s5_pre(x, wts["norm1_g"][0], scale1, shift1)
    y = _s5_core(u, wts["s5_w1"], wts["s5_w2"], wts["s5_sc"], b)
    x = _s5_post(x, y, wts["norm1_g"][0], scale1, shift1, gate1, wts["s5_d"][0], wts["s5_w_glu"])
    x = _mlp(x, wts["norm2_g"][0], scale2, shift2, gate2, wts["mlp_w1"][0], wts["mlp_w2"][0],
             wts["final_g"], False)
    shift1, scale1, gate1, shift2, scale2, gate2 = mods(1)
    q, k, v, r, gf, gb = _gla_pre(x, wts["norm1_g"][1], scale1, shift1, wts["gla_w_in"],
                                  wts["gla_w_a1"], wts["gla_w_a2"], wts["gla_b_a"])
    o_f, o_b = _gla_core(q, k, v, gf, gb)
    x = _gla_post(x, o_f, o_b, r, wts["gla_norm_g"], gate1, wts["gla_w_out"])
    return _mlp(x, wts["norm2_g"][1], scale2, shift2, gate2, wts["mlp_w1"][1], wts["mlp_w2"][1],
                wts["final_g"], True)


def kernel(x_prompt, x_sample, c_prompt, c_sample, ada_w, ada_b, norm1_g, norm2_g, s5_lam_re, s5_lam_im, s5_log_dt, s5_b_re, s5_b_im, s5_c_re, s5_c_im, s5_d, s5_w_glu, gla_w_in, gla_w_a1, gla_w_a2, gla_b_a, gla_norm_g, gla_w_out, mlp_w1, mlp_w2, final_g):
    bp, bs = c_prompt.shape[0], c_sample.shape[0]
    pad = (-(bp + bs)) % SUBLANES
    c_all = jnp.concatenate([c_prompt, c_sample, jnp.zeros((pad, D_MODEL), F32)], axis=0)
    mod_all = _modulation(c_all, ada_w, ada_b)

    s5_w1, s5_w2, s5_sc = _s5_build(s5_lam_re[0], s5_lam_im[0], s5_log_dt[0], s5_b_re[0],
                                    s5_b_im[0], s5_c_re[0], s5_c_im[0])
    r = GLA_GATE_RANK
    w_a2bd = jnp.zeros((2 * r, 2 * GLA_DK), F32)
    w_a2bd = w_a2bd.at[:r, :GLA_DK].set(gla_w_a2[0, 0]).at[r:, GLA_DK:].set(gla_w_a2[0, 1])
    wts = {
        "norm1_g": norm1_g, "norm2_g": norm2_g, "final_g": final_g,
        "s5_w1": s5_w1, "s5_w2": s5_w2, "s5_sc": s5_sc, "s5_d": s5_d,
        "s5_w_glu": s5_w_glu[0].astype(BF16),
        "mlp_w1": mlp_w1.astype(BF16), "mlp_w2": mlp_w2.astype(BF16),
        "gla_w_in": gla_w_in[0].astype(BF16),
        "gla_w_a1": jnp.concatenate([gla_w_a1[0, 0], gla_w_a1[0, 1]], axis=1).astype(BF16),
        "gla_w_a2": w_a2bd.astype(BF16),
        "gla_b_a": jnp.concatenate([gla_b_a[0, 0], gla_b_a[0, 1]], axis=0).reshape(1, 2 * GLA_DK),
        "gla_norm_g": gla_norm_g[0], "gla_w_out": gla_w_out[0].astype(BF16),
    }
    y_prompt = _trunk(x_prompt, mod_all[:, :bp], wts)
    y_sample = _trunk(x_sample, mod_all[:, bp:bp + bs], wts)
    return (y_prompt, y_sample)
```

```python
import functools
import math

import jax
import jax.numpy as jnp
from jax import lax
from jax.experimental import pallas as pl
from jax.experimental.pallas import tpu as pltpu

F32 = jnp.float32
BF16 = jnp.bfloat16

D_MODEL = 1024
S5_GROUP = 16
S5_GROUPS = D_MODEL // S5_GROUP
S5_STATE = 64
GLA_HEADS = 4
GLA_DK = D_MODEL // 2
GLA_DV = D_MODEL
GLA_HEAD_K = GLA_DK // GLA_HEADS
GLA_HEAD_V = GLA_DV // GLA_HEADS
GLA_GATE_RANK = 16
GLA_GATE_TAU = 16.0
GLA_CHUNK = 64
GLA_BLOCK = 256
D_FF = 4 * D_MODEL
N_MOD = 6
EPS = 1e-6

LANES = 128
SUBLANES = 8
SEGS = LANES // S5_GROUP

S5_C = 32
S5_CW = S5_C * S5_GROUP
S5_NSTATE = 4 * S5_STATE
S5_LEVELS = 3
S5_TBK = 1024
S5_PITCH = S5_C + 4

VMEM_LIMIT = 56 * 1024 * 1024


def _cparams(sem):
    return pltpu.CompilerParams(dimension_semantics=sem, vmem_limit_bytes=VMEM_LIMIT)


def _dot(a, b, precision=None):
    return jnp.dot(a, b, preferred_element_type=F32, precision=precision)


def _dot_nt(a, b, precision=None):
    return lax.dot_general(a, b, (((1,), (1,)), ((), ())),
                           preferred_element_type=F32, precision=precision)


def _dot_tn(a, b, precision=None):
    return lax.dot_general(a, b, (((0,), (0,)), ((), ())),
                           preferred_element_type=F32, precision=precision)


def _norm_mod(x, g, scale, shift):
    y = x * lax.rsqrt(jnp.mean(x * x, axis=-1, keepdims=True) + EPS)
    return (y * g) * (1.0 + scale) + shift


def _mod_kernel(c_ref, w_ref, b_ref, o_ref):
    c = c_ref[...]
    s = c * jax.nn.sigmoid(c)
    o_ref[0] = _dot(s.astype(BF16), w_ref[0].astype(BF16)) + b_ref[0]


def _modulation(c_all, ada_w, ada_b):
    depth, d, n = ada_w.shape
    rows = c_all.shape[0]
    tn = 1536
    return pl.pallas_call(
        _mod_kernel,
        grid=(depth, n // tn),
        in_specs=[
            pl.BlockSpec((rows, d), lambda l, j: (0, 0)),
            pl.BlockSpec((1, d, tn), lambda l, j: (l, 0, j)),
            pl.BlockSpec((1, 1, tn), lambda l, j: (l, 0, j)),
        ],
        out_specs=pl.BlockSpec((1, rows, tn), lambda l, j: (l, 0, j)),
        out_shape=jax.ShapeDtypeStruct((depth, rows, n), F32),
        compiler_params=_cparams(("parallel", "parallel")),
        name="adaln_mod",
    )(c_all, ada_w, ada_b.reshape(depth, 1, n))


def _seg_exchange(lo, hi, d, seg):
    keep = (seg & d) == 0
    return (jnp.where(keep, lo, pltpu.roll(hi, S5_GROUP * d, 1)),
            jnp.where(keep, pltpu.roll(lo, LANES - S5_GROUP * d, 1), hi))


def _seg_transpose8(ngroups, load, store, buf_a, buf_b, seg):
    def put_a(n, i, v):
        buf_a[n, i] = v

    def put_b(n, i, v):
        buf_b[n, i] = v

    stages = ((4, load, put_a),
              (2, lambda n, i: buf_a[n, i], put_b),
              (1, lambda n, i: buf_b[n, i], store))
    for d, get, put in stages:
        for n in range(ngroups):
            for i in range(SEGS):
                if i & d == 0:
                    lo, hi = _seg_exchange(get(n, i), get(n, i + d), d, seg)
                    put(n, i, lo)
                    put(n, i + d, hi)


def _seg_buffers(rows, dtype):
    ngroups = D_MODEL // LANES * (S5_C // SEGS)
    return [pltpu.VMEM((ngroups, SEGS, rows, LANES), dtype) for _ in range(2)]


def _s5_pre_kernel(x_ref, g_ref, sc_ref, sh_ref, u_ref, h_scr, buf_a, buf_b):
    h = _norm_mod(x_ref[0], g_ref[...], sc_ref[0], sh_ref[0])
    nk = S5_TBK // S5_C
    nq = S5_C // SEGS
    for k in range(nk):
        for j in range(D_MODEL // LANES):
            h_scr[j, pl.ds(S5_PITCH * k, S5_C), :] = h[S5_C * k:S5_C * (k + 1), LANES * j:LANES * (j + 1)]
    seg = lax.broadcasted_iota(jnp.int32, (nk, LANES), 1) // S5_GROUP

    def load(n, i):
        return h_scr[n // nq, pl.ds(SEGS * (n % nq) + i, nk, stride=S5_PITCH), :]

    def store(n, gl, v):
        u_ref[SEGS * (n // nq) + gl, :, pl.ds(LANES * (n % nq), LANES)] = v.astype(BF16)

    _seg_transpose8(D_MODEL // LANES * nq, load, store, buf_a, buf_b, seg)


def _s5_pre(x, norm_g, scale, shift):
    b, l, d = x.shape
    nb = l // S5_TBK
    nk = S5_TBK // S5_C
    return pl.pallas_call(
        _s5_pre_kernel,
        grid=(b, nb),
        in_specs=[
            pl.BlockSpec((1, S5_TBK, d), lambda bi, i: (bi, i, 0)),
            pl.BlockSpec((1, d), lambda bi, i: (0, 0)),
            pl.BlockSpec((1, 1, d), lambda bi, i: (bi, 0, 0)),
            pl.BlockSpec((1, 1, d), lambda bi, i: (bi, 0, 0)),
        ],
        out_specs=pl.BlockSpec((S5_GROUPS, nk, S5_CW), lambda bi, i: (0, bi * nb + i, 0)),
        out_shape=jax.ShapeDtypeStruct((S5_GROUPS, b * l // S5_C, S5_CW), BF16),
        scratch_shapes=[pltpu.VMEM((d // LANES, nk * S5_PITCH, LANES), F32)] + _seg_buffers(nk, F32),
        compiler_params=_cparams(("parallel", "parallel")),
        name="s5_pre",
    )(x, norm_g.reshape(1, d), scale, shift)


def _s5_build_kernel(rows_ref, rowsfb_ref, bcat_ref, bswp_ref, ccat_ref, cswp_ref,
                     w1_ref, w2_ref, sc_ref, ex_scr, mat_scr, kt_scr):
    c = S5_C
    cw = S5_CW
    hi = lax.Precision.HIGHEST
    nt = ((c + 1 + SUBLANES - 1) // SUBLANES) * SUBLANES
    lane1 = lax.broadcasted_iota(jnp.int32, (1, LANES), 1)
    sgn = jnp.where(lane1 < S5_STATE, -1.0, 1.0).astype(F32)
    nrow = lax.broadcasted_iota(jnp.int32, (nt, LANES), 0).astype(F32)

    for d in range(2):
        lr = rows_ref[0, d, 0:1, :]
        li = rows_ref[0, d, 1:2, :]
        dt = jnp.exp(rows_ref[0, d, 2:3, :])
        mag = jnp.exp(nrow * (lr * dt))
        ang = nrow * (li * dt)
        p_re = mag * jnp.cos(ang)
        p_im = mag * jnp.sin(ang)
        ab_re = p_re[1:2, :]
        ab_im = p_im[1:2, :]
        den = lr * lr + li * li
        z_re = ((ab_re - 1.0) * lr + ab_im * li) / den
        z_im = (ab_im * lr - (ab_re - 1.0) * li) / den
        bbar = z_re * bcat_ref[0, d] + (z_im * sgn) * bswp_ref[0, d]
        p_is = p_im * sgn
        for n in range(c + 1):
            blk = pl.ds(S5_GROUP * n, S5_GROUP)
            ex_scr[0, blk, :] = jnp.broadcast_to(p_re[n:n + 1, :], (S5_GROUP, LANES))
            ex_scr[1, blk, :] = jnp.broadcast_to(p_is[n:n + 1, :], (S5_GROUP, LANES))
            ex_scr[2, blk, :] = jnp.broadcast_to(p_re[c - n:c - n + 1, :], (S5_GROUP, LANES))
            ex_scr[3, blk, :] = jnp.broadcast_to(p_is[c - n:c - n + 1, :], (S5_GROUP, LANES))
        c_t = jnp.concatenate([ccat_ref[0, d]] * c, axis=0)
        c_s = jnp.concatenate([cswp_ref[0, d]] * c, axis=0)
        bswap = z_re * bswp_ref[0, d] - (z_im * sgn) * bcat_ref[0, d]
        b_t = jnp.concatenate([bbar] * c, axis=0)
        b_s = jnp.concatenate([bswap] * c, axis=0)
        lo = pl.ds(0, cw)
        up = pl.ds(S5_GROUP, cw)
        bneg = bbar * (-sgn)
        if d == 0:
            ct = ex_scr[0, lo, :] * c_t + ex_scr[1, lo, :] * c_s
            kt_scr[0] = _dot_nt(bneg, ct, hi)
            mat_scr[0] = ex_scr[2, up, :] * b_t + ex_scr[3, up, :] * b_s
            mat_scr[2] = ex_scr[0, up, :] * c_t + ex_scr[1, up, :] * c_s
        else:
            ct = ex_scr[2, up, :] * c_t + ex_scr[3, up, :] * c_s
            kt_scr[1] = _dot_nt(bneg, ct, hi)
            mat_scr[1] = ex_scr[0, lo, :] * b_t + ex_scr[1, lo, :] * b_s
            mat_scr[3] = ex_scr[2, lo, :] * c_t + ex_scr[3, lo, :] * c_s

    lane_cw = lax.broadcasted_iota(jnp.int32, (S5_GROUP, cw), 1)
    kf = kt_scr[0]
    kb = kt_scr[1]
    for s in range(c):
        rf = kf if s == 0 else pltpu.roll(kf, S5_GROUP * s, 1)
        rb = kb if s == c - 1 else pltpu.roll(kb, S5_GROUP * (s + 1), 1)
        blk = (jnp.where(lane_cw >= S5_GROUP * s, rf, 0.0)
               + jnp.where(lane_cw < S5_GROUP * (s + 1), rb, 0.0))
        w1_ref[0, pl.ds(S5_GROUP * s, S5_GROUP), pl.ds(0, cw)] = blk.astype(BF16)

    lane_m = lax.broadcasted_iota(jnp.int32, (cw, LANES), 1) < S5_STATE
    bmf = mat_scr[0]
    bmb = mat_scr[1]
    w1_ref[0, :, pl.ds(cw, LANES)] = jnp.where(
        lane_m, bmf, pltpu.roll(bmb, S5_STATE, 1)).astype(BF16)
    w1_ref[0, :, pl.ds(cw + LANES, LANES)] = jnp.where(
        lane_m, pltpu.roll(bmf, S5_STATE, 1), bmb).astype(BF16)
    caf = mat_scr[2]
    cab = mat_scr[3]
    w2t_re = jnp.where(lane_m, caf, pltpu.roll(cab, S5_STATE, 1))
    w2t_im = -jnp.where(lane_m, pltpu.roll(caf, S5_STATE, 1), cab)
    w2_ref[0, pl.ds(0, LANES), :] = w2t_re.T.astype(BF16)
    w2_ref[0, pl.ds(LANES, LANES), :] = w2t_im.T.astype(BF16)

    lrfb = rowsfb_ref[0, 0:1, :]
    lifb = rowsfb_ref[0, 1:2, :]
    dtfb = jnp.exp(rowsfb_ref[0, 2:3, :])
    row8 = lax.broadcasted_iota(jnp.int32, (SUBLANES, LANES), 0)
    fwd8 = lax.broadcasted_iota(jnp.int32, (SUBLANES, LANES), 1) < S5_STATE
    for lvl in range(S5_LEVELS):
        m = float(c * SUBLANES ** lvl)
        for k, dd in enumerate((1, 2, 4, 0)):
            if dd:
                n8 = jnp.full((SUBLANES, LANES), m * dd, F32)
                ok = (fwd8 & (row8 >= dd)) | (jnp.logical_not(fwd8) & (row8 < SUBLANES - dd))
            else:
                n8 = m * jnp.where(fwd8, row8 + 1, SUBLANES - row8).astype(F32)
                ok = row8 >= 0
            mag8 = jnp.exp(n8 * (lrfb * dtfb))
            ang8 = n8 * (lifb * dtfb)
            sc_ref[0, lvl * 8 + 2 * k] = jnp.where(ok, mag8 * jnp.cos(ang8), 0.0)
            sc_ref[0, lvl * 8 + 2 * k + 1] = jnp.where(ok, mag8 * jnp.sin(ang8), 0.0)


def _s5_build(lam_re, lam_im, log_dt, b_re, b_im, c_re, c_im):
    g, p = S5_GROUPS, S5_STATE

    def dup(a):
        return jnp.concatenate([a, a], axis=-1).transpose(1, 0, 2)[:, :, None, :]

    ldt = jnp.broadcast_to(log_dt.T[:, :, None, None], (g, 2, 1, 2 * p))
    rows = jnp.concatenate(
        [dup(lam_re), dup(lam_im), ldt, jnp.zeros((g, 2, SUBLANES - 3, 2 * p), F32)], axis=2)

    def fb(a):
        return jnp.concatenate([a[0], a[1]], axis=-1)[:, None, :]

    ldt2 = jnp.broadcast_to(log_dt[:, :, None], (2, g, p))
    rowsfb = jnp.concatenate(
        [fb(lam_re), fb(lam_im), fb(ldt2), jnp.zeros((g, SUBLANES - 3, 2 * p), F32)], axis=1)
    brt = b_re.transpose(1, 0, 3, 2)
    bit = b_im.transpose(1, 0, 3, 2)
    crt = c_re.transpose(1, 0, 2, 3)
    cit = c_im.transpose(1, 0, 2, 3)
    bcat = jnp.concatenate([brt, bit], axis=-1)
    bswp = jnp.concatenate([bit, brt], axis=-1)
    ccat = jnp.concatenate([crt, cit], axis=-1)
    cswp = jnp.concatenate([cit, crt], axis=-1)
    nsc = S5_LEVELS * 8
    spec4 = pl.BlockSpec((1, 2, S5_GROUP, LANES), lambda i: (i, 0, 0, 0))
    return pl.pallas_call(
        _s5_build_kernel,
        grid=(g,),
        in_specs=[
            pl.BlockSpec((1, 2, SUBLANES, LANES), lambda i: (i, 0, 0, 0)),
            pl.BlockSpec((1, SUBLANES, LANES), lambda i: (i, 0, 0)),
            spec4, spec4, spec4, spec4,
        ],
        out_specs=[
            pl.BlockSpec((1, S5_CW, S5_CW + S5_NSTATE), lambda i: (i, 0, 0)),
            pl.BlockSpec((1, S5_NSTATE, S5_CW), lambda i: (i, 0, 0)),
            pl.BlockSpec((1, nsc, SUBLANES, LANES), lambda i: (i, 0, 0, 0)),
        ],
        out_shape=[
            jax.ShapeDtypeStruct((g, S5_CW, S5_CW + S5_NSTATE), BF16),
            jax.ShapeDtypeStruct((g, S5_NSTATE, S5_CW), BF16),
            jax.ShapeDtypeStruct((g, nsc, SUBLANES, LANES), F32),
        ],
        scratch_shapes=[
            pltpu.VMEM((4, (S5_C + 1) * S5_GROUP, LANES), F32),
            pltpu.VMEM((4, S5_CW, LANES), F32),
            pltpu.VMEM((2, S5_GROUP, S5_CW), F32),
        ],
        compiler_params=_cparams(("parallel",)),
        name="s5_build",
    )(rows, rowsfb, bcat, bswp, ccat, cswp)


def _tile_scan(re, im, sc_ref, lvl, fwd):
    for k, dd in enumerate((1, 2, 4)):
        mr = sc_ref[0, lvl * 8 + 2 * k]
        mi = sc_ref[0, lvl * 8 + 2 * k + 1]
        sre = jnp.where(fwd, pltpu.roll(re, dd, 0), pltpu.roll(re, SUBLANES - dd, 0))
        sim = jnp.where(fwd, pltpu.roll(im, dd, 0), pltpu.roll(im, SUBLANES - dd, 0))
        re, im = re + mr * sre - mi * sim, im + mr * sim + mi * sre
    return re, im


def _scan_rows(bufs, lvl, base, ntiles, sc_ref, p_ref, p_base):
    re_ref, im_ref = bufs[lvl]
    fwd = lax.broadcasted_iota(jnp.int32, (SUBLANES, LANES), 1) < S5_STATE
    row = lax.broadcasted_iota(jnp.int32, (SUBLANES, LANES), 0)
    fwd1 = fwd[0:1, :]

    def carried(hre, him, cre, cim):
        pre = jnp.where(fwd, jnp.where(row == 0, cre, pltpu.roll(hre, 1, 0)),
                        jnp.where(row == SUBLANES - 1, cre, pltpu.roll(hre, SUBLANES - 1, 0)))
        pim = jnp.where(fwd, jnp.where(row == 0, cim, pltpu.roll(him, 1, 0)),
                        jnp.where(row == SUBLANES - 1, cim, pltpu.roll(him, SUBLANES - 1, 0)))
        return pre, pim

    if ntiles == 1:
        sl = pl.ds(base, SUBLANES)
        hre, him = _tile_scan(re_ref[sl, :], im_ref[sl, :], sc_ref, lvl, fwd)
        re_ref[sl, :] = hre
        im_ref[sl, :] = him
        if lvl == 0:
            zero = jnp.zeros((1, LANES), F32)
            pre, pim = carried(hre, him, zero, zero)
            p_ref[pl.ds(p_base, SUBLANES), pl.ds(0, LANES)] = pre
            p_ref[pl.ds(p_base, SUBLANES), pl.ds(LANES, LANES)] = pim
        return

    ere_ref, eim_ref = bufs[lvl + 1]
    ntn = (ntiles + SUBLANES - 1) // SUBLANES
    ere_ref[...] = jnp.zeros_like(ere_ref)
    eim_ref[...] = jnp.zeros_like(eim_ref)

    for i in range(ntiles):
        sl = pl.ds(base + i * SUBLANES, SUBLANES)
        hre, him = _tile_scan(re_ref[sl, :], im_ref[sl, :], sc_ref, lvl, fwd)
        re_ref[sl, :] = hre
        im_ref[sl, :] = him
        ere_ref[pl.ds(SUBLANES + i, 1), :] = jnp.where(fwd1, hre[SUBLANES - 1:SUBLANES, :], hre[0:1, :])
        eim_ref[pl.ds(SUBLANES + i, 1), :] = jnp.where(fwd1, him[SUBLANES - 1:SUBLANES, :], him[0:1, :])

    _scan_rows(bufs, lvl + 1, SUBLANES, ntn, sc_ref, None, 0)
    apr = sc_ref[0, lvl * 8 + 6]
    api = sc_ref[0, lvl * 8 + 7]

    for i in range(ntiles):
        sl = pl.ds(base + i * SUBLANES, SUBLANES)
        cre = jnp.where(fwd1, ere_ref[pl.ds(SUBLANES - 1 + i, 1), :], ere_ref[pl.ds(SUBLANES + 1 + i, 1), :])
        cim = jnp.where(fwd1, eim_ref[pl.ds(SUBLANES - 1 + i, 1), :], eim_ref[pl.ds(SUBLANES + 1 + i, 1), :])
        hre = re_ref[sl, :] + apr * cre - api * cim
        him = im_ref[sl, :] + apr * cim + api * cre
        if lvl == 0:
            psl = pl.ds(p_base + i * SUBLANES, SUBLANES)
            pre, pim = carried(hre, him, cre, cim)
            p_ref[psl, pl.ds(0, LANES)] = pre
            p_ref[psl, pl.ds(LANES, LANES)] = pim
        else:
            re_ref[sl, :] = hre
            im_ref[sl, :] = him


def _s5_core_kernel(nseq, u_ref, w1_ref, w2_ref, sc_ref, y_ref,
                    s_re, s_im, e1_re, e1_im, e2_re, e2_im, p_scr, yi_scr):
    rows = u_ref.shape[1]
    nk = rows // nseq
    u = u_ref[0]
    st = _dot(u, w1_ref[0, :, pl.ds(S5_CW, S5_NSTATE)])
    s_re[...] = st[:, :LANES]
    s_im[...] = st[:, LANES:]
    yi_scr[...] = _dot(u, w1_ref[0, :, pl.ds(0, S5_CW)])
    bufs = [(s_re, s_im), (e1_re, e1_im), (e2_re, e2_im)]
    for b in range(nseq):
        _scan_rows(bufs, 0, b * nk, nk // SUBLANES, sc_ref, p_scr, b * nk)
    y_ref[0] = (yi_scr[...] + _dot(p_scr[...].astype(BF16), w2_ref[0])).astype(BF16)


def _s5_core(u, w1, w2, sc, nseq):
    g, rows, cw = u.shape
    nk = rows // nseq
    nt0 = nk // SUBLANES
    e1_rows = ((nt0 + SUBLANES - 1) // SUBLANES) * SUBLANES + 2 * SUBLANES
    nt1 = (nt0 + SUBLANES - 1) // SUBLANES
    e2_rows = ((nt1 + SUBLANES - 1) // SUBLANES) * SUBLANES + 2 * SUBLANES
    assert nt1 <= SUBLANES * SUBLANES, "sequence too long for S5_LEVELS scan levels"
    return pl.pallas_call(
        functools.partial(_s5_core_kernel, nseq),
        grid=(g,),
        in_specs=[
            pl.BlockSpec((1, rows, cw), lambda i: (i, 0, 0)),
            pl.BlockSpec((1, cw, cw + S5_NSTATE), lambda i: (i, 0, 0)),
            pl.BlockSpec((1, S5_NSTATE, cw), lambda i: (i, 0, 0)),
            pl.BlockSpec((1, S5_LEVELS * 8, SUBLANES, LANES), lambda i: (i, 0, 0, 0)),
        ],
        out_specs=pl.BlockSpec((1, rows, cw), lambda i: (i, 0, 0)),
        out_shape=jax.ShapeDtypeStruct((g, rows, cw), BF16),
        scratch_shapes=[
            pltpu.VMEM((rows, LANES), F32), pltpu.VMEM((rows, LANES), F32),
            pltpu.VMEM((e1_rows, LANES), F32), pltpu.VMEM((e1_rows, LANES), F32),
            pltpu.VMEM((e2_rows, LANES), F32), pltpu.VMEM((e2_rows, LANES), F32),
            pltpu.VMEM((rows, 2 * LANES), F32),
            pltpu.VMEM((rows, cw), F32),
        ],
        compiler_params=_cparams(("parallel",)),
        name="s5_core",
    )(u, w1, w2, sc)


def _s5_post_kernel(x_ref, y_ref, g_ref, sc_ref, sh_ref, gt_ref, dsk_ref, wg_ref, o_ref,
                    y_scr, buf_a, buf_b):
    nk = S5_TBK // S5_C
    nq = S5_C // SEGS
    seg = lax.broadcasted_iota(jnp.int32, (nk, LANES), 1) // S5_GROUP

    def load(n, gl):
        return y_ref[SEGS * (n // nq) + gl, :, pl.ds(LANES * (n % nq), LANES)]

    def store(n, i, v):
        y_scr[n // nq, pl.ds(SEGS * (n % nq) + i, nk, stride=S5_PITCH), :] = v.astype(F32)

    _seg_transpose8(D_MODEL // LANES * nq, load, store, buf_a, buf_b, seg)
    x = x_ref[0]
    h = _norm_mod(x, g_ref[...], sc_ref[0], sh_ref[0])
    y = jnp.concatenate(
        [jnp.concatenate([y_scr[j, pl.ds(S5_PITCH * k, S5_C), :] for j in range(D_MODEL // LANES)], axis=-1)
         for k in range(nk)], axis=0) + dsk_ref[...] * h
    z = _dot(jax.nn.gelu(y).astype(BF16), wg_ref[...])
    m = z[:, :D_MODEL] * jax.nn.sigmoid(z[:, D_MODEL:])
    o_ref[0] = x + gt_ref[0] * m


def _s5_post(x, y, norm_g, scale, shift, gate, d_skip, w_glu):
    b, l, d = x.shape
    nb = l // S5_TBK
    nk = S5_TBK // S5_C
    vec = pl.BlockSpec((1, d), lambda bi, i: (0, 0))
    mod = pl.BlockSpec((1, 1, d), lambda bi, i: (bi, 0, 0))
    return pl.pallas_call(
        _s5_post_kernel,
        grid=(b, nb),
        in_specs=[
            pl.BlockSpec((1, S5_TBK, d), lambda bi, i: (bi, i, 0)),
            pl.BlockSpec((S5_GROUPS, nk, S5_CW), lambda bi, i: (0, bi * nb + i, 0)),
            vec, mod, mod, mod, vec,
            pl.BlockSpec((d, 2 * d), lambda bi, i: (0, 0)),
        ],
        out_specs=pl.BlockSpec((1, S5_TBK, d), lambda bi, i: (bi, i, 0)),
        out_shape=jax.ShapeDtypeStruct((b, l, d), F32),
        scratch_shapes=[pltpu.VMEM((d // LANES, nk * S5_PITCH, LANES), F32)] + _seg_buffers(nk, BF16),
        compiler_params=_cparams(("parallel", "parallel")),
        name="s5_post",
    )(x, y, norm_g.reshape(1, d), scale, shift, gate, d_skip.reshape(1, d), w_glu)


def _mlp_kernel(final, x_ref, g_ref, sc_ref, sh_ref, gt_ref, w1_ref, w2_ref, fg_ref, o_ref,
                h_scr, acc_scr):
    j = pl.program_id(2)

    @pl.when(j == 0)
    def _():
        h_scr[...] = _norm_mod(x_ref[0], g_ref[...], sc_ref[0], sh_ref[0]).astype(BF16)
        acc_scr[...] = jnp.zeros_like(acc_scr)

    a = jnp.maximum(_dot(h_scr[...], w1_ref[...]), 0.0)
    acc_scr[...] += _dot((a * a).astype(BF16), w2_ref[...])

    @pl.when(j == pl.num_programs(2) - 1)
    def _():
        out = x_ref[0] + gt_ref[0] * acc_scr[...]
        if final:
            out = out * lax.rsqrt(jnp.mean(out * out, axis=-1, keepdims=True) + EPS) * fg_ref[...]
        o_ref[0] = out


def _mlp(x, norm_g, scale, shift, gate, w1, w2, final_g, final):
    b, l, d = x.shape
    ff = w1.shape[1]
    tm = min(1024, l)
    tf = 2048
    vec = pl.BlockSpec((1, d), lambda bi, i, j: (0, 0))
    mod = pl.BlockSpec((1, 1, d), lambda bi, i, j: (bi, 0, 0))
    return pl.pallas_call(
        functools.partial(_mlp_kernel, final),
        grid=(b, l // tm, ff // tf),
        in_specs=[
            pl.BlockSpec((1, tm, d), lambda bi, i, j: (bi, i, 0)),
            vec, mod, mod, mod,
            pl.BlockSpec((d, tf), lambda bi, i, j: (0, j)),
            pl.BlockSpec((tf, d), lambda bi, i, j: (j, 0)),
            vec,
        ],
        out_specs=pl.BlockSpec((1, tm, d), lambda bi, i, j: (bi, i, 0)),
        out_shape=jax.ShapeDtypeStruct((b, l, d), F32),
        scratch_shapes=[pltpu.VMEM((tm, d), BF16), pltpu.VMEM((tm, d), F32)],
        compiler_params=_cparams(("parallel", "parallel", "arbitrary")),
        name="mlp",
    )(x, norm_g.reshape(1, d), scale, shift, gate, w1, w2, final_g.reshape(1, d))


def _gla_pre_kernel(x_ref, g_ref, sc_ref, sh_ref, win_ref, wa1_ref, wa2_ref, ba_ref,
                    q_ref, k_ref, v_ref, r_ref, gf_ref, gb_ref):
    h = _norm_mod(x_ref[0], g_ref[...], sc_ref[0], sh_ref[0]).astype(BF16)
    proj = _dot(h, win_ref[...])
    q_ref[0] = proj[:, :GLA_DK] * (GLA_HEAD_K ** -0.5)
    k_ref[0] = proj[:, GLA_DK:2 * GLA_DK]
    v_ref[0] = proj[:, 2 * GLA_DK:2 * GLA_DK + GLA_DV].astype(BF16)
    r_ref[0] = proj[:, 2 * GLA_DK + GLA_DV:].astype(BF16)
    a = _dot(h, wa1_ref[...]).astype(BF16)
    z = _dot(a, wa2_ref[...]) + ba_ref[...]
    lg = (jnp.minimum(z, 0.0) - jnp.log1p(jnp.exp(-jnp.abs(z)))) / GLA_GATE_TAU
    gf_ref[0] = lg[:, :GLA_DK]
    gb_ref[0] = lg[:, GLA_DK:]


def _gla_pre(x, norm_g, scale, shift, w_in, w_a1c, w_a2bd, b_ac):
    b, l, d = x.shape
    tm = 512
    vec = pl.BlockSpec((1, d), lambda bi, i: (0, 0))
    mod = pl.BlockSpec((1, 1, d), lambda bi, i: (bi, 0, 0))

    def full(a):
        return pl.BlockSpec(a.shape, lambda bi, i: (0,) * a.ndim)

    def tok(w):
        return pl.BlockSpec((1, tm, w), lambda bi, i: (bi, i, 0))

    return pl.pallas_call(
        _gla_pre_kernel,
        grid=(b, l // tm),
        in_specs=[tok(d), vec, mod, mod, full(w_in), full(w_a1c), full(w_a2bd), full(b_ac)],
        out_specs=[tok(GLA_DK), tok(GLA_DK), tok(GLA_DV), tok(GLA_DV), tok(GLA_DK), tok(GLA_DK)],
        out_shape=[
            jax.ShapeDtypeStruct((b, l, GLA_DK), F32),
            jax.ShapeDtypeStruct((b, l, GLA_DK), F32),
            jax.ShapeDtypeStruct((b, l, GLA_DV), BF16),
            jax.ShapeDtypeStruct((b, l, GLA_DV), BF16),
            jax.ShapeDtypeStruct((b, l, GLA_DK), F32),
            jax.ShapeDtypeStruct((b, l, GLA_DK), F32),
        ],
        compiler_params=_cparams(("parallel", "parallel")),
        name="gla_pre",
    )(x, norm_g.reshape(1, d), scale, shift, w_in, w_a1c, w_a2bd, b_ac)


def _gla_block(q_ref, k_ref, v_ref, g_ref, st_ref, bc_ref, o_ref, fwd):
    c = GLA_CHUNK
    nb = GLA_BLOCK
    n = nb // c
    row = lax.broadcasted_iota(jnp.int32, (nb, nb), 0)
    col = lax.broadcasted_iota(jnp.int32, (nb, nb), 1)
    causal = (row >= col) if fwd else (row <= col)
    cum_m = (causal & ((row // c) == (col // c))).astype(BF16)
    g = g_ref[0]
    g1 = g.astype(BF16)
    r1 = g - g1.astype(F32)
    g2 = r1.astype(BF16)
    g3 = (r1 - g2.astype(F32)).astype(BF16)
    bc_ref[...] = _dot(cum_m, g1) + _dot(cum_m, g2) + _dot(cum_m, g3)
    last = c - 1 if fwd else 0
    sub = [slice(c * j, c * (j + 1)) for j in range(n)]
    order = list(range(n)) if fwd else list(range(n - 1, -1, -1))
    pos = {j: a for a, j in enumerate(order)}
    for h in range(GLA_HEADS):
        kl = pl.ds(GLA_HEAD_K * h, GLA_HEAD_K)
        bcum = bc_ref[:, kl]
        blast = [bcum[c * j + last:c * j + last + 1, :] for j in range(n)]

        def total(select):
            terms = [blast[m] for m in range(n) if select(m)]
            return sum(terms[1:], terms[0]) if terms else None

        def between(j, i):
            return total(lambda m: pos[j] < pos[m] < pos[i])

        def scaled(x, e):
            return x if e is None else x * jnp.exp(e)

        qh = q_ref[0, :, kl]
        kh = k_ref[0, :, kl]
        qd = qh * jnp.exp(bcum)
        kt = kh * jnp.exp(jnp.concatenate([blast[j] - bcum[sub[j], :] for j in range(n)], axis=0))
        qd_b = qd.astype(BF16)
        kd_b = (kh * jnp.exp(-bcum)).astype(BF16)
        kt_b = kt.astype(BF16)
        vh = v_ref[0, :, pl.ds(GLA_HEAD_V * h, GLA_HEAD_V)]
        dec = jnp.exp(total(lambda m: True))

        def before(j):
            return total(lambda m: pos[m] < pos[j])

        def after(j):
            return total(lambda m: pos[m] > pos[j])

        q_in = jnp.concatenate([scaled(qd[sub[j], :], before(j)) for j in range(n)], axis=0)
        k_out = jnp.concatenate([scaled(kt[sub[j], :], after(j)) for j in range(n)], axis=0)
        rows = []
        for i in range(n):
            keys = []
            for j in range(n):
                if j == i or pos[j] > pos[i]:
                    keys.append(kd_b[sub[j], :])
                elif between(j, i) is None:
                    keys.append(kt_b[sub[j], :])
                else:
                    keys.append(scaled(kt[sub[j], :], between(j, i)).astype(BF16))
            rows.append(_dot_nt(qd_b[sub[i], :], jnp.concatenate(keys, axis=0)))
        scores = jnp.where(causal, jnp.concatenate(rows, axis=0), 0.0)
        s_t = st_ref[h]
        o = _dot(scores.astype(BF16), vh) + _dot_nt(q_in.astype(BF16), s_t.astype(BF16))
        o_ref[0, :, pl.ds(GLA_HEAD_V * h, GLA_HEAD_V)] = o.astype(BF16)
        st_ref[h] = s_t * dec + _dot_tn(vh, k_out.astype(BF16))


def _gla_core_kernel(qf_ref, kf_ref, vf_ref, gf_ref, qb_ref, kb_ref, vb_ref, gb_ref,
                     of_ref, ob_ref, stf_scr, stb_scr, bcf_scr, bcb_scr):
    @pl.when(pl.program_id(1) == 0)
    def _():
        stf_scr[...] = jnp.zeros_like(stf_scr)
        stb_scr[...] = jnp.zeros_like(stb_scr)

    _gla_block(qf_ref, kf_ref, vf_ref, gf_ref, stf_scr, bcf_scr, of_ref, True)
    _gla_block(qb_ref, kb_ref, vb_ref, gb_ref, stb_scr, bcb_scr, ob_ref, False)


def _gla_core(q, k, v, gf, gb):
    b, l, _ = q.shape
    c = GLA_BLOCK
    n = l // c

    def fw(w):
        return pl.BlockSpec((1, c, w), lambda bi, i: (bi, i, 0))

    def bw(w):
        return pl.BlockSpec((1, c, w), lambda bi, i: (bi, n - 1 - i, 0))

    return pl.pallas_call(
        _gla_core_kernel,
        grid=(b, n),
        in_specs=[fw(GLA_DK), fw(GLA_DK), fw(GLA_DV), fw(GLA_DK),
                  bw(GLA_DK), bw(GLA_DK), bw(GLA_DV), bw(GLA_DK)],
        out_specs=[fw(GLA_DV), bw(GLA_DV)],
        out_shape=[jax.ShapeDtypeStruct((b, l, GLA_DV), BF16),
                   jax.ShapeDtypeStruct((b, l, GLA_DV), BF16)],
        scratch_shapes=[pltpu.VMEM((GLA_HEADS, GLA_HEAD_V, GLA_HEAD_K), F32),
                        pltpu.VMEM((GLA_HEADS, GLA_HEAD_V, GLA_HEAD_K), F32),
                        pltpu.VMEM((c, GLA_DK), F32), pltpu.VMEM((c, GLA_DK), F32)],
        compiler_params=_cparams(("parallel", "arbitrary")),
        name="gla_core",
    )(q, k, v, gf, q, k, v, gb)


def _gla_post_kernel(x_ref, of_ref, ob_ref, r_ref, ng_ref, gt_ref, wo_ref, o_ref):
    o = of_ref[0].astype(F32) + ob_ref[0].astype(F32)
    parts = []
    for h in range(GLA_HEADS):
        oh = o[:, GLA_HEAD_V * h:GLA_HEAD_V * (h + 1)]
        parts.append(oh * lax.rsqrt(jnp.mean(oh * oh, axis=-1, keepdims=True) + EPS))
    on = jnp.concatenate(parts, axis=-1) * ng_ref[...]
    r = r_ref[0].astype(F32)
    gated = on * (r * jax.nn.sigmoid(r))
    o_ref[0] = x_ref[0] + gt_ref[0] * _dot(gated.astype(BF16), wo_ref[...])


def _gla_post(x, o_f, o_b, r, norm_g, gate, w_out):
    b, l, d = x.shape
    tm = 512
    tok = pl.BlockSpec((1, tm, d), lambda bi, i: (bi, i, 0))
    vec = pl.BlockSpec((1, d), lambda bi, i: (0, 0))
    mod = pl.BlockSpec((1, 1, d), lambda bi, i: (bi, 0, 0))
    return pl.pallas_call(
        _gla_post_kernel,
        grid=(b, l // tm),
        in_specs=[tok, tok, tok, tok, vec, mod, pl.BlockSpec((d, d), lambda bi, i: (0, 0))],
        out_specs=tok,
        out_shape=jax.ShapeDtypeStruct((b, l, d), F32),
        compiler_params=_cparams(("parallel", "parallel")),
        name="gla_post",
    )(x, o_f, o_b, r, norm_g.reshape(1, d), gate, w_out)


def _trunk(x, mod, wts):
    b = x.shape[0]

    def mods(layer):
        m = mod[layer].reshape(b, N_MOD, 1, D_MODEL)
        return [m[:, i] for i in range(N_MOD)]

    shift1, scale1, gate1, shift2, scale2, gate2 = mods(0)
    u = _s5_pre(x, wts["norm1_g"][0], scale1, shift1)
    y = _s5_core(u, wts["s5_w1"], wts["s5_w2"], wts["s5_sc"], b)
    x = _s5_post(x, y, wts["norm1_g"][0], scale1, shift1, gate1, wts["s5_d"][0], wts["s5_w_glu"])
    x = _mlp(x, wts["norm2_g"][0], scale2, shift2, gate2, wts["mlp_w1"][0], wts["mlp_w2"][0],
             wts["final_g"], False)
    shift1, scale1, gate1, shift2, scale2, gate2 = mods(1)
    q, k, v, r, gf, gb = _gla_pre(x, wts["norm1_g"][1], scale1, shift1, wts["gla_w_in"],
                                  wts["gla_w_a1"], wts["gla_w_a2"], wts["gla_b_a"])
    o_f, o_b = _gla_core(q, k, v, gf, gb)
    x = _gla_post(x, o_f, o_b, r, wts["gla_norm_g"], gate1, wts["gla_w_out"])
    return _mlp(x, wts["norm2_g"][1], scale2, shift2, gate2, wts["mlp_w1"][1], wts["mlp_w2"][1],
                wts["final_g"], True)


def kernel(x_prompt, x_sample, c_prompt, c_sample, ada_w, ada_b, norm1_g, norm2_g, s5_lam_re, s5_lam_im, s5_log_dt, s5_b_re, s5_b_im, s5_c_re, s5_c_im, s5_d, s5_w_glu, gla_w_in, gla_w_a1, gla_w_a2, gla_b_a, gla_norm_g, gla_w_out, mlp_w1, mlp_w2, final_g):
    bp, bs = c_prompt.shape[0], c_sample.shape[0]
    pad = (-(bp + bs)) % SUBLANES
    c_all = jnp.concatenate([c_prompt, c_sample, jnp.zeros((pad, D_MODEL), F32)], axis=0)
    mod_all = _modulation(c_all, ada_w, ada_b)

    s5_w1, s5_w2, s5_sc = _s5_build(s5_lam_re[0], s5_lam_im[0], s5_log_dt[0], s5_b_re[0],
                                    s5_b_im[0], s5_c_re[0], s5_c_im[0])
    r = GLA_GATE_RANK
    w_a2bd = jnp.zeros((2 * r, 2 * GLA_DK), F32)
    w_a2bd = w_a2bd.at[:r, :GLA_DK].set(gla_w_a2[0, 0]).at[r:, GLA_DK:].set(gla_w_a2[0, 1])
    wts = {
        "norm1_g": norm1_g, "norm2_g": norm2_g, "final_g": final_g,
        "s5_w1": s5_w1, "s5_w2": s5_w2, "s5_sc": s5_sc, "s5_d": s5_d,
        "s5_w_glu": s5_w_glu[0].astype(BF16),
        "mlp_w1": mlp_w1.astype(BF16), "mlp_w2": mlp_w2.astype(BF16),
        "gla_w_in": gla_w_in[0].astype(BF16),
        "gla_w_a1": jnp.concatenate([gla_w_a1[0, 0], gla_w_a1[0, 1]], axis=1).astype(BF16),
        "gla_w_a2": w_a2bd.astype(BF16),
        "gla_b_a": jnp.concatenate([gla_b_a[0, 0], gla_b_a[0, 1]], axis=0).reshape(1, 2 * GLA_DK),
        "gla_norm_g": gla_norm_g[0], "gla_w_out": gla_w_out[0].astype(BF16),
    }
    y_prompt = _trunk(x_prompt, mod_all[:, :bp], wts)
    y_sample = _trunk(x_sample, mod_all[:, bp:bp + bs], wts)
    return (y_prompt, y_sample)
```

```python
import functools
import math

import jax
import jax.numpy as jnp
from jax import lax
from jax.experimental import pallas as pl
from jax.experimental.pallas import tpu as pltpu

F32 = jnp.float32
BF16 = jnp.bfloat16

D_MODEL = 1024
S5_GROUP = 16
S5_GROUPS = D_MODEL // S5_GROUP
S5_STATE = 64
GLA_HEADS = 4
GLA_DK = D_MODEL // 2
GLA_DV = D_MODEL
GLA_HEAD_K = GLA_DK // GLA_HEADS
GLA_HEAD_V = GLA_DV // GLA_HEADS
GLA_GATE_RANK = 16
GLA_GATE_TAU = 16.0
GLA_CHUNK = 64
GLA_BLOCK = 256
D_FF = 4 * D_MODEL
N_MOD = 6
EPS = 1e-6

LANES = 128
SUBLANES = 8
SEGS = LANES // S5_GROUP

S5_C = 32
S5_CW = S5_C * S5_GROUP
S5_NSTATE = 4 * S5_STATE
S5_LEVELS = 3
S5_TBK = 1024
S5_PITCH = S5_C + 4

VMEM_LIMIT = 56 * 1024 * 1024


def _cparams(sem):
    return pltpu.CompilerParams(dimension_semantics=sem, vmem_limit_bytes=VMEM_LIMIT)


def _dot(a, b, precision=None):
    return jnp.dot(a, b, preferred_element_type=F32, precision=precision)


def _dot_nt(a, b, precision=None):
    return lax.dot_general(a, b, (((1,), (1,)), ((), ())),
                           preferred_element_type=F32, precision=precision)


def _dot_tn(a, b, precision=None):
    return lax.dot_general(a, b, (((0,), (0,)), ((), ())),
                           preferred_element_type=F32, precision=precision)


def _norm_mod(x, g, scale, shift):
    y = x * lax.rsqrt(jnp.mean(x * x, axis=-1, keepdims=True) + EPS)
    return (y * g) * (1.0 + scale) + shift


def _mod_kernel(c_ref, w_ref, b_ref, o_ref):
    c = c_ref[...]
    s = c * jax.nn.sigmoid(c)
    o_ref[0] = _dot(s.astype(BF16), w_ref[0].astype(BF16)) + b_ref[0]


def _modulation(c_all, ada_w, ada_b):
    depth, d, n = ada_w.shape
    rows = c_all.shape[0]
    tn = 1536
    return pl.pallas_call(
        _mod_kernel,
        grid=(depth, n // tn),
        in_specs=[
            pl.BlockSpec((rows, d), lambda l, j: (0, 0)),
            pl.BlockSpec((1, d, tn), lambda l, j: (l, 0, j)),
            pl.BlockSpec((1, 1, tn), lambda l, j: (l, 0, j)),
        ],
        out_specs=pl.BlockSpec((1, rows, tn), lambda l, j: (l, 0, j)),
        out_shape=jax.ShapeDtypeStruct((depth, rows, n), F32),
        compiler_params=_cparams(("parallel", "parallel")),
        name="adaln_mod",
    )(c_all, ada_w, ada_b.reshape(depth, 1, n))


def _seg_exchange(lo, hi, d, seg):
    keep = (seg & d) == 0
    return (jnp.where(keep, lo, pltpu.roll(hi, S5_GROUP * d, 1)),
            jnp.where(keep, pltpu.roll(lo, LANES - S5_GROUP * d, 1), hi))


def _seg_transpose8(ngroups, load, store, buf_a, buf_b, seg):
    def put_a(n, i, v):
        buf_a[n, i] = v

    def put_b(n, i, v):
        buf_b[n, i] = v

    stages = ((4, load, put_a),
              (2, lambda n, i: buf_a[n, i], put_b),
              (1, lambda n, i: buf_b[n, i], store))
    for d, get, put in stages:
        for n in range(ngroups):
            for i in range(SEGS):
                if i & d == 0:
                    lo, hi = _seg_exchange(get(n, i), get(n, i + d), d, seg)
                    put(n, i, lo)
                    put(n, i + d, hi)


def _seg_buffers(rows, dtype):
    ngroups = D_MODEL // LANES * (S5_C // SEGS)
    return [pltpu.VMEM((ngroups, SEGS, rows, LANES), dtype) for _ in range(2)]


def _s5_pre_kernel(x_ref, g_ref, sc_ref, sh_ref, u_ref, h_scr, buf_a, buf_b):
    h = _norm_mod(x_ref[0], g_ref[...], sc_ref[0], sh_ref[0])
    nk = S5_TBK // S5_C
    nq = S5_C // SEGS
    for k in range(nk):
        for j in range(D_MODEL // LANES):
            h_scr[j, pl.ds(S5_PITCH * k, S5_C), :] = h[S5_C * k:S5_C * (k + 1), LANES * j:LANES * (j + 1)]
    seg = lax.broadcasted_iota(jnp.int32, (nk, LANES), 1) // S5_GROUP

    def load(n, i):
        return h_scr[n // nq, pl.ds(SEGS * (n % nq) + i, nk, stride=S5_PITCH), :]

    def store(n, gl, v):
        u_ref[SEGS * (n // nq) + gl, :, pl.ds(LANES * (n % nq), LANES)] = v.astype(BF16)

    _seg_transpose8(D_MODEL // LANES * nq, load, store, buf_a, buf_b, seg)


def _s5_pre(x, norm_g, scale, shift):
    b, l, d = x.shape
    nb = l // S5_TBK
    nk = S5_TBK // S5_C
    return pl.pallas_call(
        _s5_pre_kernel,
        grid=(b, nb),
        in_specs=[
            pl.BlockSpec((1, S5_TBK, d), lambda bi, i: (bi, i, 0)),
            pl.BlockSpec((1, d), lambda bi, i: (0, 0)),
            pl.BlockSpec((1, 1, d), lambda bi, i: (bi, 0, 0)),
            pl.BlockSpec((1, 1, d), lambda bi, i: (bi, 0, 0)),
        ],
        out_specs=pl.BlockSpec((S5_GROUPS, nk, S5_CW), lambda bi, i: (0, bi * nb + i, 0)),
        out_shape=jax.ShapeDtypeStruct((S5_GROUPS, b * l // S5_C, S5_CW), BF16),
        scratch_shapes=[pltpu.VMEM((d // LANES, nk * S5_PITCH, LANES), F32)] + _seg_buffers(nk, F32),
        compiler_params=_cparams(("parallel", "parallel")),
        name="s5_pre",
    )(x, norm_g.reshape(1, d), scale, shift)


def _s5_build_kernel(rows_ref, rowsfb_ref, bcat_ref, bswp_ref, ccat_ref, cswp_ref, dsk_ref,
                     w1_ref, w2_ref, sc_ref, ex_scr, mat_scr, kt_scr):
    c = S5_C
    cw = S5_CW
    hi = lax.Precision.HIGHEST
    nt = ((c + 1 + SUBLANES - 1) // SUBLANES) * SUBLANES
    lane1 = lax.broadcasted_iota(jnp.int32, (1, LANES), 1)
    sgn = jnp.where(lane1 < S5_STATE, -1.0, 1.0).astype(F32)
    nrow = lax.broadcasted_iota(jnp.int32, (nt, LANES), 0).astype(F32)

    for d in range(2):
        lr = rows_ref[0, d, 0:1, :]
        li = rows_ref[0, d, 1:2, :]
        dt = jnp.exp(rows_ref[0, d, 2:3, :])
        mag = jnp.exp(nrow * (lr * dt))
        ang = nrow * (li * dt)
        p_re = mag * jnp.cos(ang)
        p_im = mag * jnp.sin(ang)
        ab_re = p_re[1:2, :]
        ab_im = p_im[1:2, :]
        den = lr * lr + li * li
        z_re = ((ab_re - 1.0) * lr + ab_im * li) / den
        z_im = (ab_im * lr - (ab_re - 1.0) * li) / den
        bbar = z_re * bcat_ref[0, d] + (z_im * sgn) * bswp_ref[0, d]
        p_is = p_im * sgn
        for n in range(c + 1):
            blk = pl.ds(S5_GROUP * n, S5_GROUP)
            ex_scr[0, blk, :] = jnp.broadcast_to(p_re[n:n + 1, :], (S5_GROUP, LANES))
            ex_scr[1, blk, :] = jnp.broadcast_to(p_is[n:n + 1, :], (S5_GROUP, LANES))
            ex_scr[2, blk, :] = jnp.broadcast_to(p_re[c - n:c - n + 1, :], (S5_GROUP, LANES))
            ex_scr[3, blk, :] = jnp.broadcast_to(p_is[c - n:c - n + 1, :], (S5_GROUP, LANES))
        c_t = jnp.concatenate([ccat_ref[0, d]] * c, axis=0)
        c_s = jnp.concatenate([cswp_ref[0, d]] * c, axis=0)
        bswap = z_re * bswp_ref[0, d] - (z_im * sgn) * bcat_ref[0, d]
        b_t = jnp.concatenate([bbar] * c, axis=0)
        b_s = jnp.concatenate([bswap] * c, axis=0)
        lo = pl.ds(0, cw)
        up = pl.ds(S5_GROUP, cw)
        bneg = bbar * (-sgn)
        if d == 0:
            ct = ex_scr[0, lo, :] * c_t + ex_scr[1, lo, :] * c_s
            kt_scr[0] = _dot_nt(bneg, ct, hi)
            mat_scr[0] = ex_scr[2, up, :] * b_t + ex_scr[3, up, :] * b_s
            mat_scr[2] = ex_scr[0, up, :] * c_t + ex_scr[1, up, :] * c_s
        else:
            ct = ex_scr[2, up, :] * c_t + ex_scr[3, up, :] * c_s
            kt_scr[1] = _dot_nt(bneg, ct, hi)
            mat_scr[1] = ex_scr[0, lo, :] * b_t + ex_scr[1, lo, :] * b_s
            mat_scr[3] = ex_scr[2, lo, :] * c_t + ex_scr[3, lo, :] * c_s

    lane_cw = lax.broadcasted_iota(jnp.int32, (S5_GROUP, cw), 1)
    chan = lax.broadcasted_iota(jnp.int32, (S5_GROUP, cw), 0)
    dsk = dsk_ref[0]
    kf = kt_scr[0]
    kb = kt_scr[1]
    for s in range(c):
        rf = kf if s == 0 else pltpu.roll(kf, S5_GROUP * s, 1)
        rb = kb if s == c - 1 else pltpu.roll(kb, S5_GROUP * (s + 1), 1)
        blk = (jnp.where(lane_cw >= S5_GROUP * s, rf, 0.0)
               + jnp.where(lane_cw < S5_GROUP * (s + 1), rb, 0.0)
               + jnp.where(lane_cw == S5_GROUP * s + chan, dsk, 0.0))
        w1_ref[0, pl.ds(S5_GROUP * s, S5_GROUP), pl.ds(0, cw)] = blk.astype(BF16)

    lane_m = lax.broadcasted_iota(jnp.int32, (cw, LANES), 1) < S5_STATE
    bmf = mat_scr[0]
    bmb = mat_scr[1]
    w1_ref[0, :, pl.ds(cw, LANES)] = jnp.where(
        lane_m, bmf, pltpu.roll(bmb, S5_STATE, 1)).astype(BF16)
    w1_ref[0, :, pl.ds(cw + LANES, LANES)] = jnp.where(
        lane_m, pltpu.roll(bmf, S5_STATE, 1), bmb).astype(BF16)
    caf = mat_scr[2]
    cab = mat_scr[3]
    w2t_re = jnp.where(lane_m, caf, pltpu.roll(cab, S5_STATE, 1))
    w2t_im = -jnp.where(lane_m, pltpu.roll(caf, S5_STATE, 1), cab)
    w2_ref[0, pl.ds(0, LANES), :] = w2t_re.T.astype(BF16)
    w2_ref[0, pl.ds(LANES, LANES), :] = w2t_im.T.astype(BF16)

    lrfb = rowsfb_ref[0, 0:1, :]
    lifb = rowsfb_ref[0, 1:2, :]
    dtfb = jnp.exp(rowsfb_ref[0, 2:3, :])
    row8 = lax.broadcasted_iota(jnp.int32, (SUBLANES, LANES), 0)
    fwd8 = lax.broadcasted_iota(jnp.int32, (SUBLANES, LANES), 1) < S5_STATE
    for lvl in range(S5_LEVELS):
        m = float(c * SUBLANES ** lvl)
        for k, dd in enumerate((1, 2, 4, 0)):
            if dd:
                n8 = jnp.full((SUBLANES, LANES), m * dd, F32)
                ok = (fwd8 & (row8 >= dd)) | (jnp.logical_not(fwd8) & (row8 < SUBLANES - dd))
            else:
                n8 = m * jnp.where(fwd8, row8 + 1, SUBLANES - row8).astype(F32)
                ok = row8 >= 0
            mag8 = jnp.exp(n8 * (lrfb * dtfb))
            ang8 = n8 * (lifb * dtfb)
            sc_ref[0, lvl * 8 + 2 * k] = jnp.where(ok, mag8 * jnp.cos(ang8), 0.0)
            sc_ref[0, lvl * 8 + 2 * k + 1] = jnp.where(ok, mag8 * jnp.sin(ang8), 0.0)


def _s5_build(lam_re, lam_im, log_dt, b_re, b_im, c_re, c_im, d_skip):
    g, p = S5_GROUPS, S5_STATE
    dsk = jnp.tile(d_skip.reshape(g, 1, S5_GROUP), (1, 1, S5_C))

    def dup(a):
        return jnp.concatenate([a, a], axis=-1).transpose(1, 0, 2)[:, :, None, :]

    ldt = jnp.broadcast_to(log_dt.T[:, :, None, None], (g, 2, 1, 2 * p))
    rows = jnp.concatenate(
        [dup(lam_re), dup(lam_im), ldt, jnp.zeros((g, 2, SUBLANES - 3, 2 * p), F32)], axis=2)

    def fb(a):
        return jnp.concatenate([a[0], a[1]], axis=-1)[:, None, :]

    ldt2 = jnp.broadcast_to(log_dt[:, :, None], (2, g, p))
    rowsfb = jnp.concatenate(
        [fb(lam_re), fb(lam_im), fb(ldt2), jnp.zeros((g, SUBLANES - 3, 2 * p), F32)], axis=1)
    brt = b_re.transpose(1, 0, 3, 2)
    bit = b_im.transpose(1, 0, 3, 2)
    crt = c_re.transpose(1, 0, 2, 3)
    cit = c_im.transpose(1, 0, 2, 3)
    bcat = jnp.concatenate([brt, bit], axis=-1)
    bswp = jnp.concatenate([bit, brt], axis=-1)
    ccat = jnp.concatenate([crt, cit], axis=-1)
    cswp = jnp.concatenate([cit, crt], axis=-1)
    nsc = S5_LEVELS * 8
    spec4 = pl.BlockSpec((1, 2, S5_GROUP, LANES), lambda i: (i, 0, 0, 0))
    return pl.pallas_call(
        _s5_build_kernel,
        grid=(g,),
        in_specs=[
            pl.BlockSpec((1, 2, SUBLANES, LANES), lambda i: (i, 0, 0, 0)),
            pl.BlockSpec((1, SUBLANES, LANES), lambda i: (i, 0, 0)),
            spec4, spec4, spec4, spec4,
            pl.BlockSpec((1, 1, S5_CW), lambda i: (i, 0, 0)),
        ],
        out_specs=[
            pl.BlockSpec((1, S5_CW, S5_CW + S5_NSTATE), lambda i: (i, 0, 0)),
            pl.BlockSpec((1, S5_NSTATE, S5_CW), lambda i: (i, 0, 0)),
            pl.BlockSpec((1, nsc, SUBLANES, LANES), lambda i: (i, 0, 0, 0)),
        ],
        out_shape=[
            jax.ShapeDtypeStruct((g, S5_CW, S5_CW + S5_NSTATE), BF16),
            jax.ShapeDtypeStruct((g, S5_NSTATE, S5_CW), BF16),
            jax.ShapeDtypeStruct((g, nsc, SUBLANES, LANES), F32),
        ],
        scratch_shapes=[
            pltpu.VMEM((4, (S5_C + 1) * S5_GROUP, LANES), F32),
            pltpu.VMEM((4, S5_CW, LANES), F32),
            pltpu.VMEM((2, S5_GROUP, S5_CW), F32),
        ],
        compiler_params=_cparams(("parallel",)),
        name="s5_build",
    )(rows, rowsfb, bcat, bswp, ccat, cswp, dsk)


def _tile_scan(re, im, sc_ref, lvl, fwd):
    for k, dd in enumerate((1, 2, 4)):
        mr = sc_ref[0, lvl * 8 + 2 * k]
        mi = sc_ref[0, lvl * 8 + 2 * k + 1]
        sre = jnp.where(fwd, pltpu.roll(re, dd, 0), pltpu.roll(re, SUBLANES - dd, 0))
        sim = jnp.where(fwd, pltpu.roll(im, dd, 0), pltpu.roll(im, SUBLANES - dd, 0))
        re, im = re + mr * sre - mi * sim, im + mr * sim + mi * sre
    return re, im


def _scan_rows(bufs, lvl, base, ntiles, sc_ref, p_ref, p_base):
    re_ref, im_ref = bufs[lvl]
    fwd = lax.broadcasted_iota(jnp.int32, (SUBLANES, LANES), 1) < S5_STATE
    row = lax.broadcasted_iota(jnp.int32, (SUBLANES, LANES), 0)
    fwd1 = fwd[0:1, :]

    def carried(hre, him, cre, cim):
        pre = jnp.where(fwd, jnp.where(row == 0, cre, pltpu.roll(hre, 1, 0)),
                        jnp.where(row == SUBLANES - 1, cre, pltpu.roll(hre, SUBLANES - 1, 0)))
        pim = jnp.where(fwd, jnp.where(row == 0, cim, pltpu.roll(him, 1, 0)),
                        jnp.where(row == SUBLANES - 1, cim, pltpu.roll(him, SUBLANES - 1, 0)))
        return pre, pim

    if ntiles == 1:
        sl = pl.ds(base, SUBLANES)
        hre, him = _tile_scan(re_ref[sl, :], im_ref[sl, :], sc_ref, lvl, fwd)
        re_ref[sl, :] = hre
        im_ref[sl, :] = him
        if lvl == 0:
            zero = jnp.zeros((1, LANES), F32)
            pre, pim = carried(hre, him, zero, zero)
            p_ref[pl.ds(p_base, SUBLANES), pl.ds(0, LANES)] = pre
            p_ref[pl.ds(p_base, SUBLANES), pl.ds(LANES, LANES)] = pim
        return

    ere_ref, eim_ref = bufs[lvl + 1]
    ntn = (ntiles + SUBLANES - 1) // SUBLANES
    ere_ref[...] = jnp.zeros_like(ere_ref)
    eim_ref[...] = jnp.zeros_like(eim_ref)

    for i in range(ntiles):
        sl = pl.ds(base + i * SUBLANES, SUBLANES)
        hre, him = _tile_scan(re_ref[sl, :], im_ref[sl, :], sc_ref, lvl, fwd)
        re_ref[sl, :] = hre
        im_ref[sl, :] = him
        ere_ref[pl.ds(SUBLANES + i, 1), :] = jnp.where(fwd1, hre[SUBLANES - 1:SUBLANES, :], hre[0:1, :])
        eim_ref[pl.ds(SUBLANES + i, 1), :] = jnp.where(fwd1, him[SUBLANES - 1:SUBLANES, :], him[0:1, :])

    _scan_rows(bufs, lvl + 1, SUBLANES, ntn, sc_ref, None, 0)
    apr = sc_ref[0, lvl * 8 + 6]
    api = sc_ref[0, lvl * 8 + 7]

    for i in range(ntiles):
        sl = pl.ds(base + i * SUBLANES, SUBLANES)
        cre = jnp.where(fwd1, ere_ref[pl.ds(SUBLANES - 1 + i, 1), :], ere_ref[pl.ds(SUBLANES + 1 + i, 1), :])
        cim = jnp.where(fwd1, eim_ref[pl.ds(SUBLANES - 1 + i, 1), :], eim_ref[pl.ds(SUBLANES + 1 + i, 1), :])
        hre = re_ref[sl, :] + apr * cre - api * cim
        him = im_ref[sl, :] + apr * cim + api * cre
        if lvl == 0:
            psl = pl.ds(p_base + i * SUBLANES, SUBLANES)
            pre, pim = carried(hre, him, cre, cim)
            p_ref[psl, pl.ds(0, LANES)] = pre
            p_ref[psl, pl.ds(LANES, LANES)] = pim
        else:
            re_ref[sl, :] = hre
            im_ref[sl, :] = him


def _s5_core_kernel(nseq, u_ref, w1_ref, w2_ref, sc_ref, y_ref,
                    s_re, s_im, e1_re, e1_im, e2_re, e2_im, p_scr, yi_scr):
    rows = u_ref.shape[1]
    nk = rows // nseq
    u = u_ref[0]
    st = _dot(u, w1_ref[0, :, pl.ds(S5_CW, S5_NSTATE)])
    s_re[...] = st[:, :LANES]
    s_im[...] = st[:, LANES:]
    yi_scr[...] = _dot(u, w1_ref[0, :, pl.ds(0, S5_CW)])
    bufs = [(s_re, s_im), (e1_re, e1_im), (e2_re, e2_im)]
    for b in range(nseq):
        _scan_rows(bufs, 0, b * nk, nk // SUBLANES, sc_ref, p_scr, b * nk)
    y_ref[0] = (yi_scr[...] + _dot(p_scr[...].astype(BF16), w2_ref[0])).astype(BF16)


def _s5_core(u, w1, w2, sc, nseq):
    g, rows, cw = u.shape
    nk = rows // nseq
    nt0 = nk // SUBLANES
    e1_rows = ((nt0 + SUBLANES - 1) // SUBLANES) * SUBLANES + 2 * SUBLANES
    nt1 = (nt0 + SUBLANES - 1) // SUBLANES
    e2_rows = ((nt1 + SUBLANES - 1) // SUBLANES) * SUBLANES + 2 * SUBLANES
    assert nt1 <= SUBLANES * SUBLANES, "sequence too long for S5_LEVELS scan levels"
    return pl.pallas_call(
        functools.partial(_s5_core_kernel, nseq),
        grid=(g,),
        in_specs=[
            pl.BlockSpec((1, rows, cw), lambda i: (i, 0, 0)),
            pl.BlockSpec((1, cw, cw + S5_NSTATE), lambda i: (i, 0, 0)),
            pl.BlockSpec((1, S5_NSTATE, cw), lambda i: (i, 0, 0)),
            pl.BlockSpec((1, S5_LEVELS * 8, SUBLANES, LANES), lambda i: (i, 0, 0, 0)),
        ],
        out_specs=pl.BlockSpec((1, rows, cw), lambda i: (i, 0, 0)),
        out_shape=jax.ShapeDtypeStruct((g, rows, cw), BF16),
        scratch_shapes=[
            pltpu.VMEM((rows, LANES), F32), pltpu.VMEM((rows, LANES), F32),
            pltpu.VMEM((e1_rows, LANES), F32), pltpu.VMEM((e1_rows, LANES), F32),
            pltpu.VMEM((e2_rows, LANES), F32), pltpu.VMEM((e2_rows, LANES), F32),
            pltpu.VMEM((rows, 2 * LANES), F32),
            pltpu.VMEM((rows, cw), F32),
        ],
        compiler_params=_cparams(("parallel",)),
        name="s5_core",
    )(u, w1, w2, sc)


def _s5_post_kernel(x_ref, y_ref, gt_ref, wg_ref, o_ref, y_scr, buf_a, buf_b):
    nk = S5_TBK // S5_C
    nq = S5_C // SEGS
    seg = lax.broadcasted_iota(jnp.int32, (nk, LANES), 1) // S5_GROUP

    def load(n, gl):
        return y_ref[SEGS * (n // nq) + gl, :, pl.ds(LANES * (n % nq), LANES)]

    def store(n, i, v):
        y_scr[n // nq, pl.ds(SEGS * (n % nq) + i, nk, stride=S5_PITCH), :] = v.astype(F32)

    _seg_transpose8(D_MODEL // LANES * nq, load, store, buf_a, buf_b, seg)
    y = jnp.concatenate(
        [jnp.concatenate([y_scr[j, pl.ds(S5_PITCH * k, S5_C), :] for j in range(D_MODEL // LANES)], axis=-1)
         for k in range(nk)], axis=0)
    z = _dot(jax.nn.gelu(y).astype(BF16), wg_ref[...])
    m = z[:, :D_MODEL] * jax.nn.sigmoid(z[:, D_MODEL:])
    o_ref[0] = x_ref[0] + gt_ref[0] * m


def _s5_post(x, y, gate, w_glu):
    b, l, d = x.shape
    nb = l // S5_TBK
    nk = S5_TBK // S5_C
    mod = pl.BlockSpec((1, 1, d), lambda bi, i: (bi, 0, 0))
    return pl.pallas_call(
        _s5_post_kernel,
        grid=(b, nb),
        in_specs=[
            pl.BlockSpec((1, S5_TBK, d), lambda bi, i: (bi, i, 0)),
            pl.BlockSpec((S5_GROUPS, nk, S5_CW), lambda bi, i: (0, bi * nb + i, 0)),
            mod,
            pl.BlockSpec((d, 2 * d), lambda bi, i: (0, 0)),
        ],
        out_specs=pl.BlockSpec((1, S5_TBK, d), lambda bi, i: (bi, i, 0)),
        out_shape=jax.ShapeDtypeStruct((b, l, d), F32),
        scratch_shapes=[pltpu.VMEM((d // LANES, nk * S5_PITCH, LANES), F32)] + _seg_buffers(nk, BF16),
        compiler_params=_cparams(("parallel", "parallel")),
        name="s5_post",
    )(x, y, gate, w_glu)


def _mlp_kernel(final, x_ref, g_ref, sc_ref, sh_ref, gt_ref, w1_ref, w2_ref, fg_ref, o_ref,
                h_scr, acc_scr):
    j = pl.program_id(2)

    @pl.when(j == 0)
    def _():
        h_scr[...] = _norm_mod(x_ref[0], g_ref[...], sc_ref[0], sh_ref[0]).astype(BF16)
        acc_scr[...] = jnp.zeros_like(acc_scr)

    a = jnp.maximum(_dot(h_scr[...], w1_ref[...]), 0.0)
    acc_scr[...] += _dot((a * a).astype(BF16), w2_ref[...])

    @pl.when(j == pl.num_programs(2) - 1)
    def _():
        out = x_ref[0] + gt_ref[0] * acc_scr[...]
        if final:
            out = out * lax.rsqrt(jnp.mean(out * out, axis=-1, keepdims=True) + EPS) * fg_ref[...]
        o_ref[0] = out


def _mlp(x, norm_g, scale, shift, gate, w1, w2, final_g, final):
    b, l, d = x.shape
    ff = w1.shape[1]
    tm = min(1024, l)
    tf = 2048
    vec = pl.BlockSpec((1, d), lambda bi, i, j: (0, 0))
    mod = pl.BlockSpec((1, 1, d), lambda bi, i, j: (bi, 0, 0))
    return pl.pallas_call(
        functools.partial(_mlp_kernel, final),
        grid=(b, l // tm, ff // tf),
        in_specs=[
            pl.BlockSpec((1, tm, d), lambda bi, i, j: (bi, i, 0)),
            vec, mod, mod, mod,
            pl.BlockSpec((d, tf), lambda bi, i, j: (0, j)),
            pl.BlockSpec((tf, d), lambda bi, i, j: (j, 0)),
            vec,
        ],
        out_specs=pl.BlockSpec((1, tm, d), lambda bi, i, j: (bi, i, 0)),
        out_shape=jax.ShapeDtypeStruct((b, l, d), F32),
        scratch_shapes=[pltpu.VMEM((tm, d), BF16), pltpu.VMEM((tm, d), F32)],
        compiler_params=_cparams(("parallel", "parallel", "arbitrary")),
        name="mlp",
    )(x, norm_g.reshape(1, d), scale, shift, gate, w1, w2, final_g.reshape(1, d))


def _gla_pre_kernel(x_ref, g_ref, sc_ref, sh_ref, win_ref, wa1_ref, wa2_ref, ba_ref,
                    q_ref, k_ref, v_ref, r_ref, gf_ref, gb_ref):
    h = _norm_mod(x_ref[0], g_ref[...], sc_ref[0], sh_ref[0]).astype(BF16)
    proj = _dot(h, win_ref[...])
    q_ref[0] = proj[:, :GLA_DK] * (GLA_HEAD_K ** -0.5)
    k_ref[0] = proj[:, GLA_DK:2 * GLA_DK]
    v_ref[0] = proj[:, 2 * GLA_DK:2 * GLA_DK + GLA_DV].astype(BF16)
    r_ref[0] = proj[:, 2 * GLA_DK + GLA_DV:].astype(BF16)
    a = _dot(h, wa1_ref[...]).astype(BF16)
    z = _dot(a, wa2_ref[...]) + ba_ref[...]
    lg = (jnp.minimum(z, 0.0) - jnp.log(1.0 + jnp.exp(-jnp.abs(z)))) * (1.0 / GLA_GATE_TAU)
    gf_ref[0] = lg[:, :GLA_DK]
    gb_ref[0] = lg[:, GLA_DK:]


def _gla_pre(x, norm_g, scale, shift, w_in, w_a1c, w_a2bd, b_ac):
    b, l, d = x.shape
    tm = 512
    vec = pl.BlockSpec((1, d), lambda bi, i: (0, 0))
    mod = pl.BlockSpec((1, 1, d), lambda bi, i: (bi, 0, 0))

    def full(a):
        return pl.BlockSpec(a.shape, lambda bi, i: (0,) * a.ndim)

    def tok(w):
        return pl.BlockSpec((1, tm, w), lambda bi, i: (bi, i, 0))

    return pl.pallas_call(
        _gla_pre_kernel,
        grid=(b, l // tm),
        in_specs=[tok(d), vec, mod, mod, full(w_in), full(w_a1c), full(w_a2bd), full(b_ac)],
        out_specs=[tok(GLA_DK), tok(GLA_DK), tok(GLA_DV), tok(GLA_DV), tok(GLA_DK), tok(GLA_DK)],
        out_shape=[
            jax.ShapeDtypeStruct((b, l, GLA_DK), F32),
            jax.ShapeDtypeStruct((b, l, GLA_DK), F32),
            jax.ShapeDtypeStruct((b, l, GLA_DV), BF16),
            jax.ShapeDtypeStruct((b, l, GLA_DV), BF16),
            jax.ShapeDtypeStruct((b, l, GLA_DK), F32),
            jax.ShapeDtypeStruct((b, l, GLA_DK), F32),
        ],
        compiler_params=_cparams(("parallel", "parallel")),
        name="gla_pre",
    )(x, norm_g.reshape(1, d), scale, shift, w_in, w_a1c, w_a2bd, b_ac)


def _gla_block(q_ref, k_ref, v_ref, g_ref, st_ref, bc_ref, o_ref, fwd):
    c = GLA_CHUNK
    nb = GLA_BLOCK
    n = nb // c
    row = lax.broadcasted_iota(jnp.int32, (nb, nb), 0)
    col = lax.broadcasted_iota(jnp.int32, (nb, nb), 1)
    causal = (row >= col) if fwd else (row <= col)
    cum_m = (causal & ((row // c) == (col // c))).astype(BF16)
    g = g_ref[0]
    g1 = g.astype(BF16)
    r1 = g - g1.astype(F32)
    g2 = r1.astype(BF16)
    g3 = (r1 - g2.astype(F32)).astype(BF16)
    bc_ref[...] = _dot(cum_m, g1) + _dot(cum_m, g2) + _dot(cum_m, g3)
    last = c - 1 if fwd else 0
    sub = [slice(c * j, c * (j + 1)) for j in range(n)]
    order = list(range(n)) if fwd else list(range(n - 1, -1, -1))
    pos = {j: a for a, j in enumerate(order)}
    for h in range(GLA_HEADS):
        kl = pl.ds(GLA_HEAD_K * h, GLA_HEAD_K)
        bcum = bc_ref[:, kl]
        blast = [bcum[c * j + last:c * j + last + 1, :] for j in range(n)]

        def total(select):
            terms = [blast[m] for m in range(n) if select(m)]
            return sum(terms[1:], terms[0]) if terms else None

        def between(j, i):
            return total(lambda m: pos[j] < pos[m] < pos[i])

        def scaled(x, e):
            return x if e is None else x * jnp.exp(e)

        qh = q_ref[0, :, kl]
        kh = k_ref[0, :, kl]
        qd = qh * jnp.exp(bcum)
        kd = kh * jnp.exp(-bcum)
        kt = jnp.concatenate([kd[sub[j], :] * jnp.exp(blast[j]) for j in range(n)], axis=0)
        qd_b = qd.astype(BF16)
        kd_b = kd.astype(BF16)
        kt_b = kt.astype(BF16)
        vh = v_ref[0, :, pl.ds(GLA_HEAD_V * h, GLA_HEAD_V)]
        dec = jnp.exp(total(lambda m: True))

        def before(j):
            return total(lambda m: pos[m] < pos[j])

        def after(j):
            return total(lambda m: pos[m] > pos[j])

        q_in = jnp.concatenate([scaled(qd[sub[j], :], before(j)) for j in range(n)], axis=0)
        k_out = jnp.concatenate([scaled(kt[sub[j], :], after(j)) for j in range(n)], axis=0)
        rows = []
        for i in range(n):
            keys = []
            for j in range(n):
                if j == i or pos[j] > pos[i]:
                    keys.append(kd_b[sub[j], :])
                elif between(j, i) is None:
                    keys.append(kt_b[sub[j], :])
                else:
                    keys.append(scaled(kt[sub[j], :], between(j, i)).astype(BF16))
            rows.append(_dot_nt(qd_b[sub[i], :], jnp.concatenate(keys, axis=0)))
        scores = jnp.where(causal, jnp.concatenate(rows, axis=0), 0.0)
        s_t = st_ref[h]
        o = _dot(scores.astype(BF16), vh) + _dot_nt(q_in.astype(BF16), s_t.astype(BF16))
        o_ref[0, :, pl.ds(GLA_HEAD_V * h, GLA_HEAD_V)] = o.astype(BF16)
        st_ref[h] = s_t * dec + _dot_tn(vh, k_out.astype(BF16))


def _gla_core_kernel(qf_ref, kf_ref, vf_ref, gf_ref, qb_ref, kb_ref, vb_ref, gb_ref,
                     of_ref, ob_ref, stf_scr, stb_scr, bcf_scr, bcb_scr):
    @pl.when(pl.program_id(1) == 0)
    def _():
        stf_scr[...] = jnp.zeros_like(stf_scr)
        stb_scr[...] = jnp.zeros_like(stb_scr)

    _gla_block(qf_ref, kf_ref, vf_ref, gf_ref, stf_scr, bcf_scr, of_ref, True)
    _gla_block(qb_ref, kb_ref, vb_ref, gb_ref, stb_scr, bcb_scr, ob_ref, False)


def _gla_core(q, k, v, gf, gb):
    b, l, _ = q.shape
    c = GLA_BLOCK
    n = l // c

    def fw(w):
        return pl.BlockSpec((1, c, w), lambda bi, i: (bi, i, 0))

    def bw(w):
        return pl.BlockSpec((1, c, w), lambda bi, i: (bi, n - 1 - i, 0))

    return pl.pallas_call(
        _gla_core_kernel,
        grid=(b, n),
        in_specs=[fw(GLA_DK), fw(GLA_DK), fw(GLA_DV), fw(GLA_DK),
                  bw(GLA_DK), bw(GLA_DK), bw(GLA_DV), bw(GLA_DK)],
        out_specs=[fw(GLA_DV), bw(GLA_DV)],
        out_shape=[jax.ShapeDtypeStruct((b, l, GLA_DV), BF16),
                   jax.ShapeDtypeStruct((b, l, GLA_DV), BF16)],
        scratch_shapes=[pltpu.VMEM((GLA_HEADS, GLA_HEAD_V, GLA_HEAD_K), F32),
                        pltpu.VMEM((GLA_HEADS, GLA_HEAD_V, GLA_HEAD_K), F32),
                        pltpu.VMEM((c, GLA_DK), F32), pltpu.VMEM((c, GLA_DK), F32)],
        compiler_params=_cparams(("parallel", "arbitrary")),
        name="gla_core",
    )(q, k, v, gf, q, k, v, gb)


def _gla_post_kernel(x_ref, of_ref, ob_ref, r_ref, ng_ref, gt_ref, wo_ref, o_ref):
    o = of_ref[0].astype(F32) + ob_ref[0].astype(F32)
    parts = []
    for h in range(GLA_HEADS):
        oh = o[:, GLA_HEAD_V * h:GLA_HEAD_V * (h + 1)]
        parts.append(oh * lax.rsqrt(jnp.mean(oh * oh, axis=-1, keepdims=True) + EPS))
    on = jnp.concatenate(parts, axis=-1) * ng_ref[...]
    r = r_ref[0].astype(F32)
    gated = on * (r * jax.nn.sigmoid(r))
    o_ref[0] = x_ref[0] + gt_ref[0] * _dot(gated.astype(BF16), wo_ref[...])


def _gla_post(x, o_f, o_b, r, norm_g, gate, w_out):
    b, l, d = x.shape
    tm = 512
    tok = pl.BlockSpec((1, tm, d), lambda bi, i: (bi, i, 0))
    vec = pl.BlockSpec((1, d), lambda bi, i: (0, 0))
    mod = pl.BlockSpec((1, 1, d), lambda bi, i: (bi, 0, 0))
    return pl.pallas_call(
        _gla_post_kernel,
        grid=(b, l // tm),
        in_specs=[tok, tok, tok, tok, vec, mod, pl.BlockSpec((d, d), lambda bi, i: (0, 0))],
        out_specs=tok,
        out_shape=jax.ShapeDtypeStruct((b, l, d), F32),
        compiler_params=_cparams(("parallel", "parallel")),
        name="gla_post",
    )(x, o_f, o_b, r, norm_g.reshape(1, d), gate, w_out)


def _trunk(x, mod, wts):
    b = x.shape[0]

    def mods(layer):
        m = mod[layer].reshape(b, N_MOD, 1, D_MODEL)
        return [m[:, i] for i in range(N_MOD)]

    shift1, scale1, gate1, shift2, scale2, gate2 = mods(0)
    u = _s5_pre(x, wts["norm1_g"][0], scale1, shift1)
    y = _s5_core(u, wts["s5_w1"], wts["s5_w2"], wts["s5_sc"], b)
    x = _s5_post(x, y, gate1, wts["s5_w_glu"])
    x = _mlp(x, wts["norm2_g"][0], scale2, shift2, gate2, wts["mlp_w1"][0], wts["mlp_w2"][0],
             wts["final_g"], False)
    shift1, scale1, gate1, shift2, scale2, gate2 = mods(1)
    q, k, v, r, gf, gb = _gla_pre(x, wts["norm1_g"][1], scale1, shift1, wts["gla_w_in"],
                                  wts["gla_w_a1"], wts["gla_w_a2"], wts["gla_b_a"])
    o_f, o_b = _gla_core(q, k, v, gf, gb)
    x = _gla_post(x, o_f, o_b, r, wts["gla_norm_g"], gate1, wts["gla_w_out"])
    return _mlp(x, wts["norm2_g"][1], scale2, shift2, gate2, wts["mlp_w1"][1], wts["mlp_w2"][1],
                wts["final_g"], True)


def kernel(x_prompt, x_sample, c_prompt, c_sample, ada_w, ada_b, norm1_g, norm2_g, s5_lam_re, s5_lam_im, s5_log_dt, s5_b_re, s5_b_im, s5_c_re, s5_c_im, s5_d, s5_w_glu, gla_w_in, gla_w_a1, gla_w_a2, gla_b_a, gla_norm_g, gla_w_out, mlp_w1, mlp_w2, final_g):
    bp, bs = c_prompt.shape[0], c_sample.shape[0]
    pad = (-(bp + bs)) % SUBLANES
    c_all = jnp.concatenate([c_prompt, c_sample, jnp.zeros((pad, D_MODEL), F32)], axis=0)
    mod_all = _modulation(c_all, ada_w, ada_b)

    s5_w1, s5_w2, s5_sc = _s5_build(s5_lam_re[0], s5_lam_im[0], s5_log_dt[0], s5_b_re[0],
                                    s5_b_im[0], s5_c_re[0], s5_c_im[0], s5_d[0])
    r = GLA_GATE_RANK
    w_a2bd = jnp.zeros((2 * r, 2 * GLA_DK), F32)
    w_a2bd = w_a2bd.at[:r, :GLA_DK].set(gla_w_a2[0, 0]).at[r:, GLA_DK:].set(gla_w_a2[0, 1])
    wts = {
        "norm1_g": norm1_g, "norm2_g": norm2_g, "final_g": final_g,
        "s5_w1": s5_w1, "s5_w2": s5_w2, "s5_sc": s5_sc,
        "s5_w_glu": s5_w_glu[0].astype(BF16),
        "mlp_w1": mlp_w1.astype(BF16), "mlp_w2": mlp_w2.astype(BF16),
        "gla_w_in": gla_w_in[0].astype(BF16),
        "gla_w_a1": jnp.concatenate([gla_w_a1[0, 0], gla_w_a1[0, 1]], axis=1).astype(BF16),
        "gla_w_a2": w_a2bd.astype(BF16),
        "gla_b_a": jnp.concatenate([gla_b_a[0, 0], gla_b_a[0, 1]], axis=0).reshape(1, 2 * GLA_DK),
        "gla_norm_g": gla_norm_g[0], "gla_w_out": gla_w_out[0].astype(BF16),
    }
    y_prompt = _trunk(x_prompt, mod_all[:, :bp], wts)
    y_sample = _trunk(x_sample, mod_all[:, bp:bp + bs], wts)
    return (y_prompt, y_sample)
```

```python
import functools

import jax
import jax.numpy as jnp
from jax import lax
from jax.experimental import pallas as pl
from jax.experimental.pallas import tpu as pltpu

F32 = jnp.float32
BF16 = jnp.bfloat16

D_MODEL = 1024
S5_GROUP = 16
S5_GROUPS = D_MODEL // S5_GROUP
S5_STATE = 64
GLA_HEADS = 4
GLA_DK = D_MODEL // 2
GLA_DV = D_MODEL
GLA_HEAD_K = GLA_DK // GLA_HEADS
GLA_HEAD_V = GLA_DV // GLA_HEADS
GLA_GATE_RANK = 16
GLA_GATE_TAU = 16.0
GLA_CHUNK = 64
GLA_BLOCK = 256
D_FF = 4 * D_MODEL
N_MOD = 6
EPS = 1e-6

LANES = 128
SUBLANES = 8
SEGS = LANES // S5_GROUP

S5_C = 32
S5_CW = S5_C * S5_GROUP
S5_NSTATE = 4 * S5_STATE
S5_LEVELS = 3
S5_TBK = 1024
S5_PITCH = S5_C + 4

VMEM_LIMIT = 56 * 1024 * 1024


def _cparams(sem):
    return pltpu.CompilerParams(dimension_semantics=sem, vmem_limit_bytes=VMEM_LIMIT)


def _dot(a, b, precision=None):
    return jnp.dot(a, b, preferred_element_type=F32, precision=precision)


def _dot_nt(a, b, precision=None):
    return lax.dot_general(a, b, (((1,), (1,)), ((), ())),
                           preferred_element_type=F32, precision=precision)


def _dot_tn(a, b, precision=None):
    return lax.dot_general(a, b, (((0,), (0,)), ((), ())),
                           preferred_element_type=F32, precision=precision)


def _norm_mod(x, g, scale, shift):
    y = x * lax.rsqrt(jnp.mean(x * x, axis=-1, keepdims=True) + EPS)
    return (y * g) * (1.0 + scale) + shift


def _mod_kernel(c_ref, w_ref, b_ref, o_ref):
    c = c_ref[...]
    s = c * jax.nn.sigmoid(c)
    o_ref[0] = _dot(s.astype(BF16), w_ref[0].astype(BF16)) + b_ref[0]


def _modulation(c_all, ada_w, ada_b):
    depth, d, n = ada_w.shape
    rows = c_all.shape[0]
    tn = 1536
    return pl.pallas_call(
        _mod_kernel,
        grid=(depth, n // tn),
        in_specs=[
            pl.BlockSpec((rows, d), lambda l, j: (0, 0)),
            pl.BlockSpec((1, d, tn), lambda l, j: (l, 0, j)),
            pl.BlockSpec((1, 1, tn), lambda l, j: (l, 0, j)),
        ],
        out_specs=pl.BlockSpec((1, rows, tn), lambda l, j: (l, 0, j)),
        out_shape=jax.ShapeDtypeStruct((depth, rows, n), F32),
        compiler_params=_cparams(("parallel", "parallel")),
        name="adaln_mod",
    )(c_all, ada_w, ada_b.reshape(depth, 1, n))


def _seg_exchange(lo, hi, d, seg):
    keep = (seg & d) == 0
    return (jnp.where(keep, lo, pltpu.roll(hi, S5_GROUP * d, 1)),
            jnp.where(keep, pltpu.roll(lo, LANES - S5_GROUP * d, 1), hi))


def _seg_transpose8(ngroups, load, store, buf_a, buf_b, seg):
    def put_a(n, i, v):
        buf_a[n, i] = v

    def put_b(n, i, v):
        buf_b[n, i] = v

    stages = ((4, load, put_a),
              (2, lambda n, i: buf_a[n, i], put_b),
              (1, lambda n, i: buf_b[n, i], store))
    for d, get, put in stages:
        for n in range(ngroups):
            for i in range(SEGS):
                if i & d == 0:
                    lo, hi = _seg_exchange(get(n, i), get(n, i + d), d, seg)
                    put(n, i, lo)
                    put(n, i + d, hi)


def _seg_buffers(rows, dtype):
    ngroups = D_MODEL // LANES * (S5_C // SEGS)
    return [pltpu.VMEM((ngroups, SEGS, rows, LANES), dtype) for _ in range(2)]


def _s5_pre_kernel(x_ref, g_ref, sc_ref, sh_ref, u_ref, h_scr, buf_a, buf_b):
    h = _norm_mod(x_ref[0], g_ref[...], sc_ref[0], sh_ref[0])
    nk = S5_TBK // S5_C
    nq = S5_C // SEGS
    for k in range(nk):
        for j in range(D_MODEL // LANES):
            h_scr[j, pl.ds(S5_PITCH * k, S5_C), :] = h[S5_C * k:S5_C * (k + 1), LANES * j:LANES * (j + 1)]
    seg = lax.broadcasted_iota(jnp.int32, (nk, LANES), 1) // S5_GROUP

    def load(n, i):
        return h_scr[n // nq, pl.ds(SEGS * (n % nq) + i, nk, stride=S5_PITCH), :]

    def store(n, gl, v):
        u_ref[SEGS * (n // nq) + gl, :, pl.ds(LANES * (n % nq), LANES)] = v.astype(BF16)

    _seg_transpose8(D_MODEL // LANES * nq, load, store, buf_a, buf_b, seg)


def _s5_pre(x, norm_g, scale, shift):
    b, l, d = x.shape
    nb = l // S5_TBK
    nk = S5_TBK // S5_C
    return pl.pallas_call(
        _s5_pre_kernel,
        grid=(b, nb),
        in_specs=[
            pl.BlockSpec((1, S5_TBK, d), lambda bi, i: (bi, i, 0)),
            pl.BlockSpec((1, d), lambda bi, i: (0, 0)),
            pl.BlockSpec((1, 1, d), lambda bi, i: (bi, 0, 0)),
            pl.BlockSpec((1, 1, d), lambda bi, i: (bi, 0, 0)),
        ],
        out_specs=pl.BlockSpec((S5_GROUPS, nk, S5_CW), lambda bi, i: (0, bi * nb + i, 0)),
        out_shape=jax.ShapeDtypeStruct((S5_GROUPS, b * l // S5_C, S5_CW), BF16),
        scratch_shapes=[pltpu.VMEM((d // LANES, nk * S5_PITCH, LANES), F32)] + _seg_buffers(nk, F32),
        compiler_params=_cparams(("parallel", "parallel")),
        name="s5_pre",
    )(x, norm_g.reshape(1, d), scale, shift)


def _s5_build_kernel(rows_ref, rowsfb_ref, bcat_ref, bswp_ref, ccat_ref, cswp_ref, dsk_ref,
                     w1_ref, w2_ref, sc_ref, ex_scr, mat_scr, kt_scr):
    c = S5_C
    cw = S5_CW
    hi = lax.Precision.HIGHEST
    nt = ((c + 1 + SUBLANES - 1) // SUBLANES) * SUBLANES
    lane1 = lax.broadcasted_iota(jnp.int32, (1, LANES), 1)
    sgn = jnp.where(lane1 < S5_STATE, -1.0, 1.0).astype(F32)
    nrow = lax.broadcasted_iota(jnp.int32, (nt, LANES), 0).astype(F32)

    for d in range(2):
        lr = rows_ref[0, d, 0:1, :]
        li = rows_ref[0, d, 1:2, :]
        dt = jnp.exp(rows_ref[0, d, 2:3, :])
        mag = jnp.exp(nrow * (lr * dt))
        ang = nrow * (li * dt)
        p_re = mag * jnp.cos(ang)
        p_im = mag * jnp.sin(ang)
        ab_re = p_re[1:2, :]
        ab_im = p_im[1:2, :]
        den = lr * lr + li * li
        z_re = ((ab_re - 1.0) * lr + ab_im * li) / den
        z_im = (ab_im * lr - (ab_re - 1.0) * li) / den
        bbar = z_re * bcat_ref[0, d] + (z_im * sgn) * bswp_ref[0, d]
        p_is = p_im * sgn
        for n in range(c + 1):
            blk = pl.ds(S5_GROUP * n, S5_GROUP)
            ex_scr[0, blk, :] = jnp.broadcast_to(p_re[n:n + 1, :], (S5_GROUP, LANES))
            ex_scr[1, blk, :] = jnp.broadcast_to(p_is[n:n + 1, :], (S5_GROUP, LANES))
            ex_scr[2, blk, :] = jnp.broadcast_to(p_re[c - n:c - n + 1, :], (S5_GROUP, LANES))
            ex_scr[3, blk, :] = jnp.broadcast_to(p_is[c - n:c - n + 1, :], (S5_GROUP, LANES))
        c_t = jnp.concatenate([ccat_ref[0, d]] * c, axis=0)
        c_s = jnp.concatenate([cswp_ref[0, d]] * c, axis=0)
        bswap = z_re * bswp_ref[0, d] - (z_im * sgn) * bcat_ref[0, d]
        b_t = jnp.concatenate([bbar] * c, axis=0)
        b_s = jnp.concatenate([bswap] * c, axis=0)
        lo = pl.ds(0, cw)
        up = pl.ds(S5_GROUP, cw)
        bneg = bbar * (-sgn)
        if d == 0:
            ct = ex_scr[0, lo, :] * c_t + ex_scr[1, lo, :] * c_s
            kt_scr[0] = _dot_nt(bneg, ct, hi)
            mat_scr[0] = ex_scr[2, up, :] * b_t + ex_scr[3, up, :] * b_s
            mat_scr[2] = ex_scr[0, up, :] * c_t + ex_scr[1, up, :] * c_s
        else:
            ct = ex_scr[2, up, :] * c_t + ex_scr[3, up, :] * c_s
            kt_scr[1] = _dot_nt(bneg, ct, hi)
            mat_scr[1] = ex_scr[0, lo, :] * b_t + ex_scr[1, lo, :] * b_s
            mat_scr[3] = ex_scr[2, lo, :] * c_t + ex_scr[3, lo, :] * c_s

    lane_cw = lax.broadcasted_iota(jnp.int32, (S5_GROUP, cw), 1)
    chan = lax.broadcasted_iota(jnp.int32, (S5_GROUP, cw), 0)
    dsk = dsk_ref[0]
    kf = kt_scr[0]
    kb = kt_scr[1]
    for s in range(c):
        rf = kf if s == 0 else pltpu.roll(kf, S5_GROUP * s, 1)
        rb = kb if s == c - 1 else pltpu.roll(kb, S5_GROUP * (s + 1), 1)
        blk = (jnp.where(lane_cw >= S5_GROUP * s, rf, 0.0)
               + jnp.where(lane_cw < S5_GROUP * (s + 1), rb, 0.0)
               + jnp.where(lane_cw == S5_GROUP * s + chan, dsk, 0.0))
        w1_ref[0, pl.ds(S5_GROUP * s, S5_GROUP), pl.ds(0, cw)] = blk.astype(BF16)

    lane_m = lax.broadcasted_iota(jnp.int32, (cw, LANES), 1) < S5_STATE
    bmf = mat_scr[0]
    bmb = mat_scr[1]
    w1_ref[0, :, pl.ds(cw, LANES)] = jnp.where(
        lane_m, bmf, pltpu.roll(bmb, S5_STATE, 1)).astype(BF16)
    w1_ref[0, :, pl.ds(cw + LANES, LANES)] = jnp.where(
        lane_m, pltpu.roll(bmf, S5_STATE, 1), bmb).astype(BF16)
    caf = mat_scr[2]
    cab = mat_scr[3]
    w2t_re = jnp.where(lane_m, caf, pltpu.roll(cab, S5_STATE, 1))
    w2t_im = -jnp.where(lane_m, pltpu.roll(caf, S5_STATE, 1), cab)
    w2_ref[0, pl.ds(0, LANES), :] = w2t_re.T.astype(BF16)
    w2_ref[0, pl.ds(LANES, LANES), :] = w2t_im.T.astype(BF16)

    lrfb = rowsfb_ref[0, 0:1, :]
    lifb = rowsfb_ref[0, 1:2, :]
    dtfb = jnp.exp(rowsfb_ref[0, 2:3, :])
    row8 = lax.broadcasted_iota(jnp.int32, (SUBLANES, LANES), 0)
    fwd8 = lax.broadcasted_iota(jnp.int32, (SUBLANES, LANES), 1) < S5_STATE
    for lvl in range(S5_LEVELS):
        m = float(c * SUBLANES ** lvl)
        for k, dd in enumerate((1, 2, 4, 0)):
            if dd:
                n8 = jnp.full((SUBLANES, LANES), m * dd, F32)
                ok = (fwd8 & (row8 >= dd)) | (jnp.logical_not(fwd8) & (row8 < SUBLANES - dd))
            else:
                n8 = m * jnp.where(fwd8, row8 + 1, SUBLANES - row8).astype(F32)
                ok = row8 >= 0
            mag8 = jnp.exp(n8 * (lrfb * dtfb))
            ang8 = n8 * (lifb * dtfb)
            sc_ref[0, lvl * 8 + 2 * k] = jnp.where(ok, mag8 * jnp.cos(ang8), 0.0)
            sc_ref[0, lvl * 8 + 2 * k + 1] = jnp.where(ok, mag8 * jnp.sin(ang8), 0.0)


def _s5_build(lam_re, lam_im, log_dt, b_re, b_im, c_re, c_im, d_skip):
    g, p = S5_GROUPS, S5_STATE
    dsk = jnp.tile(d_skip.reshape(g, 1, S5_GROUP), (1, 1, S5_C))

    def dup(a):
        return jnp.concatenate([a, a], axis=-1).transpose(1, 0, 2)[:, :, None, :]

    ldt = jnp.broadcast_to(log_dt.T[:, :, None, None], (g, 2, 1, 2 * p))
    rows = jnp.concatenate(
        [dup(lam_re), dup(lam_im), ldt, jnp.zeros((g, 2, SUBLANES - 3, 2 * p), F32)], axis=2)

    def fb(a):
        return jnp.concatenate([a[0], a[1]], axis=-1)[:, None, :]

    ldt2 = jnp.broadcast_to(log_dt[:, :, None], (2, g, p))
    rowsfb = jnp.concatenate(
        [fb(lam_re), fb(lam_im), fb(ldt2), jnp.zeros((g, SUBLANES - 3, 2 * p), F32)], axis=1)
    brt = b_re.transpose(1, 0, 3, 2)
    bit = b_im.transpose(1, 0, 3, 2)
    crt = c_re.transpose(1, 0, 2, 3)
    cit = c_im.transpose(1, 0, 2, 3)
    bcat = jnp.concatenate([brt, bit], axis=-1)
    bswp = jnp.concatenate([bit, brt], axis=-1)
    ccat = jnp.concatenate([crt, cit], axis=-1)
    cswp = jnp.concatenate([cit, crt], axis=-1)
    nsc = S5_LEVELS * 8
    spec4 = pl.BlockSpec((1, 2, S5_GROUP, LANES), lambda i: (i, 0, 0, 0))
    return pl.pallas_call(
        _s5_build_kernel,
        grid=(g,),
        in_specs=[
            pl.BlockSpec((1, 2, SUBLANES, LANES), lambda i: (i, 0, 0, 0)),
            pl.BlockSpec((1, SUBLANES, LANES), lambda i: (i, 0, 0)),
            spec4, spec4, spec4, spec4,
            pl.BlockSpec((1, 1, S5_CW), lambda i: (i, 0, 0)),
        ],
        out_specs=[
            pl.BlockSpec((1, S5_CW, S5_CW + S5_NSTATE), lambda i: (i, 0, 0)),
            pl.BlockSpec((1, S5_NSTATE, S5_CW), lambda i: (i, 0, 0)),
            pl.BlockSpec((1, nsc, SUBLANES, LANES), lambda i: (i, 0, 0, 0)),
        ],
        out_shape=[
            jax.ShapeDtypeStruct((g, S5_CW, S5_CW + S5_NSTATE), BF16),
            jax.ShapeDtypeStruct((g, S5_NSTATE, S5_CW), BF16),
            jax.ShapeDtypeStruct((g, nsc, SUBLANES, LANES), F32),
        ],
        scratch_shapes=[
            pltpu.VMEM((4, (S5_C + 1) * S5_GROUP, LANES), F32),
            pltpu.VMEM((4, S5_CW, LANES), F32),
            pltpu.VMEM((2, S5_GROUP, S5_CW), F32),
        ],
        compiler_params=_cparams(("parallel",)),
        name="s5_build",
    )(rows, rowsfb, bcat, bswp, ccat, cswp, dsk)


def _tile_scan(re, im, sc_ref, lvl, fwd):
    for k, dd in enumerate((1, 2, 4)):
        mr = sc_ref[0, lvl * 8 + 2 * k]
        mi = sc_ref[0, lvl * 8 + 2 * k + 1]
        sre = jnp.where(fwd, pltpu.roll(re, dd, 0), pltpu.roll(re, SUBLANES - dd, 0))
        sim = jnp.where(fwd, pltpu.roll(im, dd, 0), pltpu.roll(im, SUBLANES - dd, 0))
        re, im = re + mr * sre - mi * sim, im + mr * sim + mi * sre
    return re, im


def _scan_rows(bufs, lvl, base, ntiles, sc_ref, p_ref, p_base):
    re_ref, im_ref = bufs[lvl]
    fwd = lax.broadcasted_iota(jnp.int32, (SUBLANES, LANES), 1) < S5_STATE
    row = lax.broadcasted_iota(jnp.int32, (SUBLANES, LANES), 0)
    fwd1 = fwd[0:1, :]

    def carried(hre, him, cre, cim):
        pre = jnp.where(fwd, jnp.where(row == 0, cre, pltpu.roll(hre, 1, 0)),
                        jnp.where(row == SUBLANES - 1, cre, pltpu.roll(hre, SUBLANES - 1, 0)))
        pim = jnp.where(fwd, jnp.where(row == 0, cim, pltpu.roll(him, 1, 0)),
                        jnp.where(row == SUBLANES - 1, cim, pltpu.roll(him, SUBLANES - 1, 0)))
        return pre, pim

    if ntiles == 1:
        sl = pl.ds(base, SUBLANES)
        hre, him = _tile_scan(re_ref[sl, :], im_ref[sl, :], sc_ref, lvl, fwd)
        re_ref[sl, :] = hre
        im_ref[sl, :] = him
        if lvl == 0:
            zero = jnp.zeros((1, LANES), F32)
            pre, pim = carried(hre, him, zero, zero)
            p_ref[pl.ds(p_base, SUBLANES), pl.ds(0, LANES)] = pre
            p_ref[pl.ds(p_base, SUBLANES), pl.ds(LANES, LANES)] = pim
        return

    ere_ref, eim_ref = bufs[lvl + 1]
    ntn = (ntiles + SUBLANES - 1) // SUBLANES
    ere_ref[...] = jnp.zeros_like(ere_ref)
    eim_ref[...] = jnp.zeros_like(eim_ref)

    for i in range(ntiles):
        sl = pl.ds(base + i * SUBLANES, SUBLANES)
        hre, him = _tile_scan(re_ref[sl, :], im_ref[sl, :], sc_ref, lvl, fwd)
        re_ref[sl, :] = hre
        im_ref[sl, :] = him
        ere_ref[pl.ds(SUBLANES + i, 1), :] = jnp.where(fwd1, hre[SUBLANES - 1:SUBLANES, :], hre[0:1, :])
        eim_ref[pl.ds(SUBLANES + i, 1), :] = jnp.where(fwd1, him[SUBLANES - 1:SUBLANES, :], him[0:1, :])

    _scan_rows(bufs, lvl + 1, SUBLANES, ntn, sc_ref, None, 0)
    apr = sc_ref[0, lvl * 8 + 6]
    api = sc_ref[0, lvl * 8 + 7]

    for i in range(ntiles):
        sl = pl.ds(base + i * SUBLANES, SUBLANES)
        cre = jnp.where(fwd1, ere_ref[pl.ds(SUBLANES - 1 + i, 1), :], ere_ref[pl.ds(SUBLANES + 1 + i, 1), :])
        cim = jnp.where(fwd1, eim_ref[pl.ds(SUBLANES - 1 + i, 1), :], eim_ref[pl.ds(SUBLANES + 1 + i, 1), :])
        hre = re_ref[sl, :] + apr * cre - api * cim
        him = im_ref[sl, :] + apr * cim + api * cre
        if lvl == 0:
            psl = pl.ds(p_base + i * SUBLANES, SUBLANES)
            pre, pim = carried(hre, him, cre, cim)
            p_ref[psl, pl.ds(0, LANES)] = pre
            p_ref[psl, pl.ds(LANES, LANES)] = pim
        else:
            re_ref[sl, :] = hre
            im_ref[sl, :] = him


def _s5_core_kernel(nseq, u_ref, w1_ref, w2_ref, sc_ref, y_ref,
                    s_re, s_im, e1_re, e1_im, e2_re, e2_im, p_scr, yi_scr):
    rows = u_ref.shape[1]
    nk = rows // nseq
    u = u_ref[0]
    st = _dot(u, w1_ref[0, :, pl.ds(S5_CW, S5_NSTATE)])
    s_re[...] = st[:, :LANES]
    s_im[...] = st[:, LANES:]
    yi_scr[...] = _dot(u, w1_ref[0, :, pl.ds(0, S5_CW)])
    bufs = [(s_re, s_im), (e1_re, e1_im), (e2_re, e2_im)]
    for b in range(nseq):
        _scan_rows(bufs, 0, b * nk, nk // SUBLANES, sc_ref, p_scr, b * nk)
    y_ref[0] = (yi_scr[...] + _dot(p_scr[...].astype(BF16), w2_ref[0])).astype(BF16)


def _s5_core(u, w1, w2, sc, nseq):
    g, rows, cw = u.shape
    nk = rows // nseq
    nt0 = nk // SUBLANES
    e1_rows = ((nt0 + SUBLANES - 1) // SUBLANES) * SUBLANES + 2 * SUBLANES
    nt1 = (nt0 + SUBLANES - 1) // SUBLANES
    e2_rows = ((nt1 + SUBLANES - 1) // SUBLANES) * SUBLANES + 2 * SUBLANES
    assert nt1 <= SUBLANES * SUBLANES, "sequence too long for S5_LEVELS scan levels"
    return pl.pallas_call(
        functools.partial(_s5_core_kernel, nseq),
        grid=(g,),
        in_specs=[
            pl.BlockSpec((1, rows, cw), lambda i: (i, 0, 0)),
            pl.BlockSpec((1, cw, cw + S5_NSTATE), lambda i: (i, 0, 0)),
            pl.BlockSpec((1, S5_NSTATE, cw), lambda i: (i, 0, 0)),
            pl.BlockSpec((1, S5_LEVELS * 8, SUBLANES, LANES), lambda i: (i, 0, 0, 0)),
        ],
        out_specs=pl.BlockSpec((1, rows, cw), lambda i: (i, 0, 0)),
        out_shape=jax.ShapeDtypeStruct((g, rows, cw), BF16),
        scratch_shapes=[
            pltpu.VMEM((rows, LANES), F32), pltpu.VMEM((rows, LANES), F32),
            pltpu.VMEM((e1_rows, LANES), F32), pltpu.VMEM((e1_rows, LANES), F32),
            pltpu.VMEM((e2_rows, LANES), F32), pltpu.VMEM((e2_rows, LANES), F32),
            pltpu.VMEM((rows, 2 * LANES), F32),
            pltpu.VMEM((rows, cw), F32),
        ],
        compiler_params=_cparams(("parallel",)),
        name="s5_core",
    )(u, w1, w2, sc)


def _s5_post_kernel(x_ref, y_ref, gt_ref, wg_ref, o_ref, y_scr, buf_a, buf_b):
    nk = S5_TBK // S5_C
    nq = S5_C // SEGS
    seg = lax.broadcasted_iota(jnp.int32, (nk, LANES), 1) // S5_GROUP

    def load(n, gl):
        return y_ref[SEGS * (n // nq) + gl, :, pl.ds(LANES * (n % nq), LANES)]

    def store(n, i, v):
        y_scr[n // nq, pl.ds(SEGS * (n % nq) + i, nk, stride=S5_PITCH), :] = v.astype(F32)

    _seg_transpose8(D_MODEL // LANES * nq, load, store, buf_a, buf_b, seg)
    y = jnp.concatenate(
        [jnp.concatenate([y_scr[j, pl.ds(S5_PITCH * k, S5_C), :] for j in range(D_MODEL // LANES)], axis=-1)
         for k in range(nk)], axis=0)
    z = _dot(jax.nn.gelu(y).astype(BF16), wg_ref[...])
    m = z[:, :D_MODEL] * jax.nn.sigmoid(z[:, D_MODEL:])
    o_ref[0] = x_ref[0] + gt_ref[0] * m


def _s5_post(x, y, gate, w_glu):
    b, l, d = x.shape
    nb = l // S5_TBK
    nk = S5_TBK // S5_C
    mod = pl.BlockSpec((1, 1, d), lambda bi, i: (bi, 0, 0))
    return pl.pallas_call(
        _s5_post_kernel,
        grid=(b, nb),
        in_specs=[
            pl.BlockSpec((1, S5_TBK, d), lambda bi, i: (bi, i, 0)),
            pl.BlockSpec((S5_GROUPS, nk, S5_CW), lambda bi, i: (0, bi * nb + i, 0)),
            mod,
            pl.BlockSpec((d, 2 * d), lambda bi, i: (0, 0)),
        ],
        out_specs=pl.BlockSpec((1, S5_TBK, d), lambda bi, i: (bi, i, 0)),
        out_shape=jax.ShapeDtypeStruct((b, l, d), F32),
        scratch_shapes=[pltpu.VMEM((d // LANES, nk * S5_PITCH, LANES), F32)] + _seg_buffers(nk, BF16),
        compiler_params=_cparams(("parallel", "parallel")),
        name="s5_post",
    )(x, y, gate, w_glu)


def _mlp_kernel(x_ref, g_ref, sc_ref, sh_ref, gt_ref, w1_ref, w2_ref, o_ref, h_scr, acc_scr):
    j = pl.program_id(2)

    @pl.when(j == 0)
    def _():
        h_scr[...] = _norm_mod(x_ref[0], g_ref[...], sc_ref[0], sh_ref[0]).astype(BF16)
        acc_scr[...] = jnp.zeros_like(acc_scr)

    a = jnp.maximum(_dot(h_scr[...], w1_ref[...]), 0.0)
    acc_scr[...] += _dot((a * a).astype(BF16), w2_ref[...])

    @pl.when(j == pl.num_programs(2) - 1)
    def _():
        o_ref[0] = x_ref[0] + gt_ref[0] * acc_scr[...]


def _mlp(x, norm_g, scale, shift, gate, w1, w2):
    b, l, d = x.shape
    ff = w1.shape[1]
    tm = min(1024, l)
    tf = 2048
    vec = pl.BlockSpec((1, d), lambda bi, i, j: (0, 0))
    mod = pl.BlockSpec((1, 1, d), lambda bi, i, j: (bi, 0, 0))
    return pl.pallas_call(
        _mlp_kernel,
        grid=(b, l // tm, ff // tf),
        in_specs=[
            pl.BlockSpec((1, tm, d), lambda bi, i, j: (bi, i, 0)),
            vec, mod, mod, mod,
            pl.BlockSpec((d, tf), lambda bi, i, j: (0, j)),
            pl.BlockSpec((tf, d), lambda bi, i, j: (j, 0)),
        ],
        out_specs=pl.BlockSpec((1, tm, d), lambda bi, i, j: (bi, i, 0)),
        out_shape=jax.ShapeDtypeStruct((b, l, d), F32),
        scratch_shapes=[pltpu.VMEM((tm, d), BF16), pltpu.VMEM((tm, d), F32)],
        compiler_params=_cparams(("parallel", "parallel", "arbitrary")),
        name="mlp",
    )(x, norm_g.reshape(1, d), scale, shift, gate, w1, w2)


def _gla_pre_kernel(x_ref, g_ref, sc_ref, sh_ref, win_ref, wa1_ref, wa2_ref, ba_ref,
                    q_ref, k_ref, v_ref, r_ref, gf_ref, gb_ref):
    h = _norm_mod(x_ref[0], g_ref[...], sc_ref[0], sh_ref[0]).astype(BF16)
    proj = _dot(h, win_ref[...])
    q_ref[0] = proj[:, :GLA_DK] * (GLA_HEAD_K ** -0.5)
    k_ref[0] = proj[:, GLA_DK:2 * GLA_DK]
    v_ref[0] = proj[:, 2 * GLA_DK:2 * GLA_DK + GLA_DV].astype(BF16)
    r_ref[0] = proj[:, 2 * GLA_DK + GLA_DV:].astype(BF16)
    a = _dot(h, wa1_ref[...]).astype(BF16)
    z = _dot(a, wa2_ref[...]) + ba_ref[...]
    lg = (jnp.minimum(z, 0.0) - jnp.log(1.0 + jnp.exp(-jnp.abs(z)))) * (1.0 / GLA_GATE_TAU)
    gf_ref[0] = lg[:, :GLA_DK]
    gb_ref[0] = lg[:, GLA_DK:]


def _gla_pre(x, norm_g, scale, shift, w_in, w_a1c, w_a2bd, b_ac):
    b, l, d = x.shape
    tm = 512
    vec = pl.BlockSpec((1, d), lambda bi, i: (0, 0))
    mod = pl.BlockSpec((1, 1, d), lambda bi, i: (bi, 0, 0))

    def full(a):
        return pl.BlockSpec(a.shape, lambda bi, i: (0,) * a.ndim)

    def tok(w):
        return pl.BlockSpec((1, tm, w), lambda bi, i: (bi, i, 0))

    return pl.pallas_call(
        _gla_pre_kernel,
        grid=(b, l // tm),
        in_specs=[tok(d), vec, mod, mod, full(w_in), full(w_a1c), full(w_a2bd), full(b_ac)],
        out_specs=[tok(GLA_DK), tok(GLA_DK), tok(GLA_DV), tok(GLA_DV), tok(GLA_DK), tok(GLA_DK)],
        out_shape=[
            jax.ShapeDtypeStruct((b, l, GLA_DK), F32),
            jax.ShapeDtypeStruct((b, l, GLA_DK), F32),
            jax.ShapeDtypeStruct((b, l, GLA_DV), BF16),
            jax.ShapeDtypeStruct((b, l, GLA_DV), BF16),
            jax.ShapeDtypeStruct((b, l, GLA_DK), F32),
            jax.ShapeDtypeStruct((b, l, GLA_DK), F32),
        ],
        compiler_params=_cparams(("parallel", "parallel")),
        name="gla_pre",
    )(x, norm_g.reshape(1, d), scale, shift, w_in, w_a1c, w_a2bd, b_ac)


def _gla_block(q_ref, k_ref, v_ref, g_ref, st_ref, bc_ref, o_ref, fwd):
    c = GLA_CHUNK
    nb = GLA_BLOCK
    n = nb // c
    row = lax.broadcasted_iota(jnp.int32, (nb, nb), 0)
    col = lax.broadcasted_iota(jnp.int32, (nb, nb), 1)
    causal = (row >= col) if fwd else (row <= col)
    cum_m = (causal & ((row // c) == (col // c))).astype(BF16)
    g = g_ref[0]
    g1 = g.astype(BF16)
    g2 = (g - g1.astype(F32)).astype(BF16)
    bc_ref[...] = _dot(cum_m, g1) + _dot(cum_m, g2)
    last = c - 1 if fwd else 0
    sub = [slice(c * j, c * (j + 1)) for j in range(n)]
    order = list(range(n)) if fwd else list(range(n - 1, -1, -1))
    pos = {j: a for a, j in enumerate(order)}
    for h in range(GLA_HEADS):
        kl = pl.ds(GLA_HEAD_K * h, GLA_HEAD_K)
        bcum = bc_ref[:, kl]
        blast = [bcum[c * j + last:c * j + last + 1, :] for j in range(n)]

        def total(select):
            terms = [blast[m] for m in range(n) if select(m)]
            return sum(terms[1:], terms[0]) if terms else None

        def between(j, i):
            return total(lambda m: pos[j] < pos[m] < pos[i])

        def scaled(x, e):
            return x if e is None else x * jnp.exp(e)

        qh = q_ref[0, :, kl]
        kh = k_ref[0, :, kl]
        qd = qh * jnp.exp(bcum)
        kd = kh * jnp.exp(-bcum)
        kt = jnp.concatenate([kd[sub[j], :] * jnp.exp(blast[j]) for j in range(n)], axis=0)
        qd_b = qd.astype(BF16)
        kd_b = kd.astype(BF16)
        kt_b = kt.astype(BF16)
        vh = v_ref[0, :, pl.ds(GLA_HEAD_V * h, GLA_HEAD_V)]
        dec = jnp.exp(total(lambda m: True))

        def before(j):
            return total(lambda m: pos[m] < pos[j])

        def after(j):
            return total(lambda m: pos[m] > pos[j])

        q_in = jnp.concatenate([scaled(qd[sub[j], :], before(j)) for j in range(n)], axis=0)
        k_out = jnp.concatenate([scaled(kt[sub[j], :], after(j)) for j in range(n)], axis=0)
        rows = []
        for i in range(n):
            keys = []
            for j in range(n):
                if j == i or pos[j] > pos[i]:
                    keys.append(kd_b[sub[j], :])
                elif between(j, i) is None:
                    keys.append(kt_b[sub[j], :])
                else:
                    keys.append(scaled(kt[sub[j], :], between(j, i)).astype(BF16))
            rows.append(_dot_nt(qd_b[sub[i], :], jnp.concatenate(keys, axis=0)))
        scores = jnp.where(causal, jnp.concatenate(rows, axis=0), 0.0)
        s_t = st_ref[h]
        o = _dot(scores.astype(BF16), vh) + _dot_nt(q_in.astype(BF16), s_t.astype(BF16))
        o_ref[0, :, pl.ds(GLA_HEAD_V * h, GLA_HEAD_V)] = o.astype(BF16)
        st_ref[h] = s_t * dec + _dot_tn(vh, k_out.astype(BF16))


def _gla_core_kernel(qf_ref, kf_ref, vf_ref, gf_ref, qb_ref, kb_ref, vb_ref, gb_ref,
                     of_ref, ob_ref, stf_scr, stb_scr, bcf_scr, bcb_scr):
    @pl.when(pl.program_id(1) == 0)
    def _():
        stf_scr[...] = jnp.zeros_like(stf_scr)
        stb_scr[...] = jnp.zeros_like(stb_scr)

    _gla_block(qf_ref, kf_ref, vf_ref, gf_ref, stf_scr, bcf_scr, of_ref, True)
    _gla_block(qb_ref, kb_ref, vb_ref, gb_ref, stb_scr, bcb_scr, ob_ref, False)


def _gla_core(q, k, v, gf, gb):
    b, l, _ = q.shape
    c = GLA_BLOCK
    n = l // c

    def fw(w):
        return pl.BlockSpec((1, c, w), lambda bi, i: (bi, i, 0))

    def bw(w):
        return pl.BlockSpec((1, c, w), lambda bi, i: (bi, n - 1 - i, 0))

    return pl.pallas_call(
        _gla_core_kernel,
        grid=(b, n),
        in_specs=[fw(GLA_DK), fw(GLA_DK), fw(GLA_DV), fw(GLA_DK),
                  bw(GLA_DK), bw(GLA_DK), bw(GLA_DV), bw(GLA_DK)],
        out_specs=[fw(GLA_DV), bw(GLA_DV)],
        out_shape=[jax.ShapeDtypeStruct((b, l, GLA_DV), BF16),
                   jax.ShapeDtypeStruct((b, l, GLA_DV), BF16)],
        scratch_shapes=[pltpu.VMEM((GLA_HEADS, GLA_HEAD_V, GLA_HEAD_K), F32),
                        pltpu.VMEM((GLA_HEADS, GLA_HEAD_V, GLA_HEAD_K), F32),
                        pltpu.VMEM((c, GLA_DK), F32), pltpu.VMEM((c, GLA_DK), F32)],
        compiler_params=_cparams(("parallel", "arbitrary")),
        name="gla_core",
    )(q, k, v, gf, q, k, v, gb)


def _gla_out_mlp_kernel(x_ref, of_ref, ob_ref, r_ref, ng_ref, gt1_ref, wo_ref,
                        g_ref, sc_ref, sh_ref, gt2_ref, w1_ref, w2_ref, fg_ref, o_ref,
                        x_scr, h_scr, acc_scr):
    j = pl.program_id(2)

    @pl.when(j == 0)
    def _():
        o = of_ref[0].astype(F32) + ob_ref[0].astype(F32)
        parts = []
        for h in range(GLA_HEADS):
            oh = o[:, GLA_HEAD_V * h:GLA_HEAD_V * (h + 1)]
            parts.append(oh * lax.rsqrt(jnp.mean(oh * oh, axis=-1, keepdims=True) + EPS))
        on = jnp.concatenate(parts, axis=-1) * ng_ref[...]
        r = r_ref[0].astype(F32)
        gated = on * (r * jax.nn.sigmoid(r))
        x = x_ref[0] + gt1_ref[0] * _dot(gated.astype(BF16), wo_ref[...])
        x_scr[...] = x
        h_scr[...] = _norm_mod(x, g_ref[...], sc_ref[0], sh_ref[0]).astype(BF16)
        acc_scr[...] = jnp.zeros_like(acc_scr)

    a = jnp.maximum(_dot(h_scr[...], w1_ref[...]), 0.0)
    acc_scr[...] += _dot((a * a).astype(BF16), w2_ref[...])

    @pl.when(j == pl.num_programs(2) - 1)
    def _():
        out = x_scr[...] + gt2_ref[0] * acc_scr[...]
        o_ref[0] = out * lax.rsqrt(jnp.mean(out * out, axis=-1, keepdims=True) + EPS) * fg_ref[...]


def _gla_out_mlp(x, o_f, o_b, r, gla_norm_g, gate1, w_out, norm_g, scale, shift, gate2, w1, w2,
                 final_g):
    b, l, d = x.shape
    ff = w1.shape[1]
    tm = 512
    tf = 2048
    tok = pl.BlockSpec((1, tm, d), lambda bi, i, j: (bi, i, 0))
    vec = pl.BlockSpec((1, d), lambda bi, i, j: (0, 0))
    mod = pl.BlockSpec((1, 1, d), lambda bi, i, j: (bi, 0, 0))
    return pl.pallas_call(
        _gla_out_mlp_kernel,
        grid=(b, l // tm, ff // tf),
        in_specs=[
            tok, tok, tok, tok, vec, mod, pl.BlockSpec((d, d), lambda bi, i, j: (0, 0)),
            vec, mod, mod, mod,
            pl.BlockSpec((d, tf), lambda bi, i, j: (0, j)),
            pl.BlockSpec((tf, d), lambda bi, i, j: (j, 0)),
            vec,
        ],
        out_specs=tok,
        out_shape=jax.ShapeDtypeStruct((b, l, d), F32),
        scratch_shapes=[pltpu.VMEM((tm, d), F32), pltpu.VMEM((tm, d), BF16),
                        pltpu.VMEM((tm, d), F32)],
        compiler_params=_cparams(("parallel", "parallel", "arbitrary")),
        name="gla_out_mlp",
    )(x, o_f, o_b, r, gla_norm_g.reshape(1, d), gate1, w_out, norm_g.reshape(1, d), scale, shift,
      gate2, w1, w2, final_g.reshape(1, d))


def _trunk(x, mod, wts):
    b = x.shape[0]

    def mods(layer):
        m = mod[layer].reshape(b, N_MOD, 1, D_MODEL)
        return [m[:, i] for i in range(N_MOD)]

    shift1, scale1, gate1, shift2, scale2, gate2 = mods(0)
    u = _s5_pre(x, wts["norm1_g"][0], scale1, shift1)
    y = _s5_core(u, wts["s5_w1"], wts["s5_w2"], wts["s5_sc"], b)
    x = _s5_post(x, y, gate1, wts["s5_w_glu"])
    x = _mlp(x, wts["norm2_g"][0], scale2, shift2, gate2, wts["mlp_w1"][0], wts["mlp_w2"][0])
    shift1, scale1, gate1, shift2, scale2, gate2 = mods(1)
    q, k, v, r, gf, gb = _gla_pre(x, wts["norm1_g"][1], scale1, shift1, wts["gla_w_in"],
                                  wts["gla_w_a1"], wts["gla_w_a2"], wts["gla_b_a"])
    o_f, o_b = _gla_core(q, k, v, gf, gb)
    return _gla_out_mlp(x, o_f, o_b, r, wts["gla_norm_g"], gate1, wts["gla_w_out"],
                        wts["norm2_g"][1], scale2, shift2, gate2, wts["mlp_w1"][1],
                        wts["mlp_w2"][1], wts["final_g"])


def kernel(x_prompt, x_sample, c_prompt, c_sample, ada_w, ada_b, norm1_g, norm2_g, s5_lam_re, s5_lam_im, s5_log_dt, s5_b_re, s5_b_im, s5_c_re, s5_c_im, s5_d, s5_w_glu, gla_w_in, gla_w_a1, gla_w_a2, gla_b_a, gla_norm_g, gla_w_out, mlp_w1, mlp_w2, final_g):
    bp, bs = c_prompt.shape[0], c_sample.shape[0]
    pad = (-(bp + bs)) % SUBLANES
    c_all = jnp.concatenate([c_prompt, c_sample, jnp.zeros((pad, D_MODEL), F32)], axis=0)
    mod_all = _modulation(c_all, ada_w, ada_b)

    s5_w1, s5_w2, s5_sc = _s5_build(s5_lam_re[0], s5_lam_im[0], s5_log_dt[0], s5_b_re[0],
                                    s5_b_im[0], s5_c_re[0], s5_c_im[0], s5_d[0])
    r = GLA_GATE_RANK
    w_a2bd = jnp.zeros((2 * r, 2 * GLA_DK), F32)
    w_a2bd = w_a2bd.at[:r, :GLA_DK].set(gla_w_a2[0, 0]).at[r:, GLA_DK:].set(gla_w_a2[0, 1])
    wts = {
        "norm1_g": norm1_g, "norm2_g": norm2_g, "final_g": final_g,
        "s5_w1": s5_w1, "s5_w2": s5_w2, "s5_sc": s5_sc,
        "s5_w_glu": s5_w_glu[0].astype(BF16),
        "mlp_w1": mlp_w1.astype(BF16), "mlp_w2": mlp_w2.astype(BF16),
        "gla_w_in": gla_w_in[0].astype(BF16),
        "gla_w_a1": jnp.concatenate([gla_w_a1[0, 0], gla_w_a1[0, 1]], axis=1).astype(BF16),
        "gla_w_a2": w_a2bd.astype(BF16),
        "gla_b_a": jnp.concatenate([gla_b_a[0, 0], gla_b_a[0, 1]], axis=0).reshape(1, 2 * GLA_DK),
        "gla_norm_g": gla_norm_g[0], "gla_w_out": gla_w_out[0].astype(BF16),
    }
    y_prompt = _trunk(x_prompt, mod_all[:, :bp], wts)
    y_sample = _trunk(x_sample, mod_all[:, bp:bp + bs], wts)
    return (y_prompt, y_sample)
```

```python
import functools

import jax
import jax.numpy as jnp
from jax import lax
from jax.experimental import pallas as pl
from jax.experimental.pallas import tpu as pltpu

F32 = jnp.float32
BF16 = jnp.bfloat16

D_MODEL = 1024
S5_GROUP = 16
S5_GROUPS = D_MODEL // S5_GROUP
S5_STATE = 64
GLA_HEADS = 4
GLA_DK = D_MODEL // 2
GLA_DV = D_MODEL
GLA_HEAD_K = GLA_DK // GLA_HEADS
GLA_HEAD_V = GLA_DV // GLA_HEADS
GLA_GATE_RANK = 16
GLA_GATE_TAU = 16.0
GLA_CHUNK = 64
GLA_BLOCK = 256
N_MOD = 6
EPS = 1e-6

LANES = 128
SUBLANES = 8
SEGS = LANES // S5_GROUP

S5_C = 32
S5_CW = S5_C * S5_GROUP
S5_NSTATE = 4 * S5_STATE
S5_LEVELS = 3
S5_TBK = 1024
S5_PITCH = S5_C + 4

MOD_TN = 1536
MLP_TM = 1024
MLP_TF = 2048
GLA_TM = 512

VMEM_LIMIT = 56 * 1024 * 1024


def _cparams(sem):
    return pltpu.CompilerParams(dimension_semantics=sem, vmem_limit_bytes=VMEM_LIMIT)


def _dot(a, b, precision=None):
    return jnp.dot(a, b, preferred_element_type=F32, precision=precision)


def _dot_nt(a, b, precision=None):
    return lax.dot_general(a, b, (((1,), (1,)), ((), ())),
                           preferred_element_type=F32, precision=precision)


def _dot_tn(a, b, precision=None):
    return lax.dot_general(a, b, (((0,), (0,)), ((), ())),
                           preferred_element_type=F32, precision=precision)


def _norm_mod(x, g, scale, shift):
    y = x * lax.rsqrt(jnp.mean(x * x, axis=-1, keepdims=True) + EPS)
    return y * (g * (1.0 + scale)) + shift


def _mod_kernel(c_ref, w_ref, b_ref, o_ref):
    c = c_ref[...]
    s = c * jax.nn.sigmoid(c)
    o_ref[0] = _dot(s.astype(BF16), w_ref[0].astype(BF16)) + b_ref[0]


def _modulation(c_all, ada_w, ada_b):
    depth, d, n = ada_w.shape
    rows = c_all.shape[0]
    tn = MOD_TN
    return pl.pallas_call(
        _mod_kernel,
        grid=(depth, n // tn),
        in_specs=[
            pl.BlockSpec((rows, d), lambda l, j: (0, 0)),
            pl.BlockSpec((1, d, tn), lambda l, j: (l, 0, j)),
            pl.BlockSpec((1, 1, tn), lambda l, j: (l, 0, j)),
        ],
        out_specs=pl.BlockSpec((1, rows, tn), lambda l, j: (l, 0, j)),
        out_shape=jax.ShapeDtypeStruct((depth, rows, n), F32),
        compiler_params=_cparams(("parallel", "parallel")),
        name="adaln_mod",
    )(c_all, ada_w, ada_b.reshape(depth, 1, n))


def _seg_exchange(lo, hi, d, seg):
    keep = (seg & d) == 0
    return (jnp.where(keep, lo, pltpu.roll(hi, S5_GROUP * d, 1)),
            jnp.where(keep, pltpu.roll(lo, LANES - S5_GROUP * d, 1), hi))


def _seg_transpose8(ngroups, load, store, buf_a, buf_b, seg):
    def put_a(n, i, v):
        buf_a[n, i] = v

    def put_b(n, i, v):
        buf_b[n, i] = v

    stages = ((4, load, put_a),
              (2, lambda n, i: buf_a[n, i], put_b),
              (1, lambda n, i: buf_b[n, i], store))
    for d, get, put in stages:
        for n in range(ngroups):
            for i in range(SEGS):
                if i & d == 0:
                    lo, hi = _seg_exchange(get(n, i), get(n, i + d), d, seg)
                    put(n, i, lo)
                    put(n, i + d, hi)


def _seg_buffers(rows, dtype):
    ngroups = D_MODEL // LANES * (S5_C // SEGS)
    return [pltpu.VMEM((ngroups, SEGS, rows, LANES), dtype) for _ in range(2)]


def _s5_pre_kernel(x_ref, g_ref, sc_ref, sh_ref, u_ref, h_scr, buf_a, buf_b):
    h = _norm_mod(x_ref[0], g_ref[...], sc_ref[0], sh_ref[0])
    nk = S5_TBK // S5_C
    nq = S5_C // SEGS
    for k in range(nk):
        for j in range(D_MODEL // LANES):
            h_scr[j, pl.ds(S5_PITCH * k, S5_C), :] = h[S5_C * k:S5_C * (k + 1), LANES * j:LANES * (j + 1)]
    seg = lax.broadcasted_iota(jnp.int32, (nk, LANES), 1) // S5_GROUP

    def load(n, i):
        return h_scr[n // nq, pl.ds(SEGS * (n % nq) + i, nk, stride=S5_PITCH), :]

    def store(n, gl, v):
        u_ref[SEGS * (n // nq) + gl, :, pl.ds(LANES * (n % nq), LANES)] = v.astype(BF16)

    _seg_transpose8(D_MODEL // LANES * nq, load, store, buf_a, buf_b, seg)


def _s5_pre(x, norm_g, scale, shift):
    b, l, d = x.shape
    nb = l // S5_TBK
    nk = S5_TBK // S5_C
    return pl.pallas_call(
        _s5_pre_kernel,
        grid=(b, nb),
        in_specs=[
            pl.BlockSpec((1, S5_TBK, d), lambda bi, i: (bi, i, 0)),
            pl.BlockSpec((1, d), lambda bi, i: (0, 0)),
            pl.BlockSpec((1, 1, d), lambda bi, i: (bi, 0, 0)),
            pl.BlockSpec((1, 1, d), lambda bi, i: (bi, 0, 0)),
        ],
        out_specs=pl.BlockSpec((S5_GROUPS, nk, S5_CW), lambda bi, i: (0, bi * nb + i, 0)),
        out_shape=jax.ShapeDtypeStruct((S5_GROUPS, b * l // S5_C, S5_CW), BF16),
        scratch_shapes=[pltpu.VMEM((d // LANES, nk * S5_PITCH, LANES), F32)] + _seg_buffers(nk, F32),
        compiler_params=_cparams(("parallel", "parallel")),
        name="s5_pre",
    )(x, norm_g.reshape(1, d), scale, shift)


def _s5_build_kernel(rows_ref, rowsfb_ref, bcat_ref, bswp_ref, ccat_ref, cswp_ref, dsk_ref,
                     w1_ref, w2_ref, sc_ref, ex_scr, mat_scr, kt_scr):
    c = S5_C
    cw = S5_CW
    hi = lax.Precision.HIGHEST
    nt = ((c + 1 + SUBLANES - 1) // SUBLANES) * SUBLANES
    lane1 = lax.broadcasted_iota(jnp.int32, (1, LANES), 1)
    sgn = jnp.where(lane1 < S5_STATE, -1.0, 1.0).astype(F32)
    nrow = lax.broadcasted_iota(jnp.int32, (nt, LANES), 0).astype(F32)

    for d in range(2):
        lr = rows_ref[0, d, 0:1, :]
        li = rows_ref[0, d, 1:2, :]
        dt = jnp.exp(rows_ref[0, d, 2:3, :])
        mag = jnp.exp(nrow * (lr * dt))
        ang = nrow * (li * dt)
        p_re = mag * jnp.cos(ang)
        p_im = mag * jnp.sin(ang)
        ab_re = p_re[1:2, :]
        ab_im = p_im[1:2, :]
        den = lr * lr + li * li
        z_re = ((ab_re - 1.0) * lr + ab_im * li) / den
        z_im = (ab_im * lr - (ab_re - 1.0) * li) / den
        bbar = z_re * bcat_ref[0, d] + (z_im * sgn) * bswp_ref[0, d]
        p_is = p_im * sgn
        for n in range(c + 1):
            blk = pl.ds(S5_GROUP * n, S5_GROUP)
            ex_scr[0, blk, :] = jnp.broadcast_to(p_re[n:n + 1, :], (S5_GROUP, LANES))
            ex_scr[1, blk, :] = jnp.broadcast_to(p_is[n:n + 1, :], (S5_GROUP, LANES))
            ex_scr[2, blk, :] = jnp.broadcast_to(p_re[c - n:c - n + 1, :], (S5_GROUP, LANES))
            ex_scr[3, blk, :] = jnp.broadcast_to(p_is[c - n:c - n + 1, :], (S5_GROUP, LANES))
        c_t = jnp.concatenate([ccat_ref[0, d]] * c, axis=0)
        c_s = jnp.concatenate([cswp_ref[0, d]] * c, axis=0)
        bswap = z_re * bswp_ref[0, d] - (z_im * sgn) * bcat_ref[0, d]
        b_t = jnp.concatenate([bbar] * c, axis=0)
        b_s = jnp.concatenate([bswap] * c, axis=0)
        lo = pl.ds(0, cw)
        up = pl.ds(S5_GROUP, cw)
        bneg = bbar * (-sgn)
        if d == 0:
            ct = ex_scr[0, lo, :] * c_t + ex_scr[1, lo, :] * c_s
            kt_scr[0] = _dot_nt(bneg, ct, hi)
            mat_scr[0] = ex_scr[2, up, :] * b_t + ex_scr[3, up, :] * b_s
            mat_scr[2] = ex_scr[0, up, :] * c_t + ex_scr[1, up, :] * c_s
        else:
            ct = ex_scr[2, up, :] * c_t + ex_scr[3, up, :] * c_s
            kt_scr[1] = _dot_nt(bneg, ct, hi)
            mat_scr[1] = ex_scr[0, lo, :] * b_t + ex_scr[1, lo, :] * b_s
            mat_scr[3] = ex_scr[2, lo, :] * c_t + ex_scr[3, lo, :] * c_s

    lane_cw = lax.broadcasted_iota(jnp.int32, (S5_GROUP, cw), 1)
    chan = lax.broadcasted_iota(jnp.int32, (S5_GROUP, cw), 0)
    dsk = dsk_ref[0]
    kf = kt_scr[0]
    kb = kt_scr[1]
    for s in range(c):
        rf = kf if s == 0 else pltpu.roll(kf, S5_GROUP * s, 1)
        rb = kb if s == c - 1 else pltpu.roll(kb, S5_GROUP * (s + 1), 1)
        blk = (jnp.where(lane_cw >= S5_GROUP * s, rf, 0.0)
               + jnp.where(lane_cw < S5_GROUP * (s + 1), rb, 0.0)
               + jnp.where(lane_cw == S5_GROUP * s + chan, dsk, 0.0))
        w1_ref[0, pl.ds(S5_GROUP * s, S5_GROUP), pl.ds(0, cw)] = blk.astype(BF16)

    lane_m = lax.broadcasted_iota(jnp.int32, (cw, LANES), 1) < S5_STATE
    bmf = mat_scr[0]
    bmb = mat_scr[1]
    w1_ref[0, :, pl.ds(cw, LANES)] = jnp.where(
        lane_m, bmf, pltpu.roll(bmb, S5_STATE, 1)).astype(BF16)
    w1_ref[0, :, pl.ds(cw + LANES, LANES)] = jnp.where(
        lane_m, pltpu.roll(bmf, S5_STATE, 1), bmb).astype(BF16)
    caf = mat_scr[2]
    cab = mat_scr[3]
    w2t_re = jnp.where(lane_m, caf, pltpu.roll(cab, S5_STATE, 1))
    w2t_im = -jnp.where(lane_m, pltpu.roll(caf, S5_STATE, 1), cab)
    w2_ref[0, pl.ds(0, LANES), :] = w2t_re.T.astype(BF16)
    w2_ref[0, pl.ds(LANES, LANES), :] = w2t_im.T.astype(BF16)

    lrfb = rowsfb_ref[0, 0:1, :]
    lifb = rowsfb_ref[0, 1:2, :]
    dtfb = jnp.exp(rowsfb_ref[0, 2:3, :])
    row8 = lax.broadcasted_iota(jnp.int32, (SUBLANES, LANES), 0)
    fwd8 = lax.broadcasted_iota(jnp.int32, (SUBLANES, LANES), 1) < S5_STATE
    mag_c = jnp.exp(float(c) * (lrfb * dtfb))
    ang_c = float(c) * (lifb * dtfb)
    base = (mag_c * jnp.cos(ang_c), mag_c * jnp.sin(ang_c))

    def cmul(a, b):
        return a[0] * b[0] - a[1] * b[1], a[0] * b[1] + a[1] * b[0]

    for lvl in range(S5_LEVELS):
        pw = [base]
        for _ in range(SUBLANES - 1):
            pw.append(cmul(pw[-1], base))
        for k, dd in enumerate((1, 2, 4)):
            ok = (fwd8 & (row8 >= dd)) | (jnp.logical_not(fwd8) & (row8 < SUBLANES - dd))
            sc_ref[0, lvl * 8 + 2 * k] = jnp.where(ok, pw[dd - 1][0], 0.0)
            sc_ref[0, lvl * 8 + 2 * k + 1] = jnp.where(ok, pw[dd - 1][1], 0.0)
        ap_re = jnp.zeros((SUBLANES, LANES), F32)
        ap_im = jnp.zeros((SUBLANES, LANES), F32)
        for n in range(1, SUBLANES + 1):
            here = (fwd8 & (row8 == n - 1)) | (jnp.logical_not(fwd8) & (row8 == SUBLANES - n))
            ap_re = jnp.where(here, pw[n - 1][0], ap_re)
            ap_im = jnp.where(here, pw[n - 1][1], ap_im)
        sc_ref[0, lvl * 8 + 6] = ap_re
        sc_ref[0, lvl * 8 + 7] = ap_im
        base = pw[SUBLANES - 1]


def _s5_build(lam_re, lam_im, log_dt, b_re, b_im, c_re, c_im, d_skip):
    g, p = S5_GROUPS, S5_STATE
    dsk = jnp.tile(d_skip.reshape(g, 1, S5_GROUP), (1, 1, S5_C))

    def dup(a):
        return jnp.concatenate([a, a], axis=-1).transpose(1, 0, 2)[:, :, None, :]

    ldt = jnp.broadcast_to(log_dt.T[:, :, None, None], (g, 2, 1, 2 * p))
    rows = jnp.concatenate(
        [dup(lam_re), dup(lam_im), ldt, jnp.zeros((g, 2, SUBLANES - 3, 2 * p), F32)], axis=2)

    def fb(a):
        return jnp.concatenate([a[0], a[1]], axis=-1)[:, None, :]

    ldt2 = jnp.broadcast_to(log_dt[:, :, None], (2, g, p))
    rowsfb = jnp.concatenate(
        [fb(lam_re), fb(lam_im), fb(ldt2), jnp.zeros((g, SUBLANES - 3, 2 * p), F32)], axis=1)
    brt = b_re.transpose(1, 0, 3, 2)
    bit = b_im.transpose(1, 0, 3, 2)
    crt = c_re.transpose(1, 0, 2, 3)
    cit = c_im.transpose(1, 0, 2, 3)
    bcat = jnp.concatenate([brt, bit], axis=-1)
    bswp = jnp.concatenate([bit, brt], axis=-1)
    ccat = jnp.concatenate([crt, cit], axis=-1)
    cswp = jnp.concatenate([cit, crt], axis=-1)
    nsc = S5_LEVELS * 8
    spec4 = pl.BlockSpec((1, 2, S5_GROUP, LANES), lambda i: (i, 0, 0, 0))
    return pl.pallas_call(
        _s5_build_kernel,
        grid=(g,),
        in_specs=[
            pl.BlockSpec((1, 2, SUBLANES, LANES), lambda i: (i, 0, 0, 0)),
            pl.BlockSpec((1, SUBLANES, LANES), lambda i: (i, 0, 0)),
            spec4, spec4, spec4, spec4,
            pl.BlockSpec((1, 1, S5_CW), lambda i: (i, 0, 0)),
        ],
        out_specs=[
            pl.BlockSpec((1, S5_CW, S5_CW + S5_NSTATE), lambda i: (i, 0, 0)),
            pl.BlockSpec((1, S5_NSTATE, S5_CW), lambda i: (i, 0, 0)),
            pl.BlockSpec((1, nsc, SUBLANES, LANES), lambda i: (i, 0, 0, 0)),
        ],
        out_shape=[
            jax.ShapeDtypeStruct((g, S5_CW, S5_CW + S5_NSTATE), BF16),
            jax.ShapeDtypeStruct((g, S5_NSTATE, S5_CW), BF16),
            jax.ShapeDtypeStruct((g, nsc, SUBLANES, LANES), F32),
        ],
        scratch_shapes=[
            pltpu.VMEM((4, (S5_C + 1) * S5_GROUP, LANES), F32),
            pltpu.VMEM((4, S5_CW, LANES), F32),
            pltpu.VMEM((2, S5_GROUP, S5_CW), F32),
        ],
        compiler_params=_cparams(("parallel",)),
        name="s5_build",
    )(rows, rowsfb, bcat, bswp, ccat, cswp, dsk)


def _tile_scan(re, im, sc_ref, lvl, fwd):
    for k, dd in enumerate((1, 2, 4)):
        mr = sc_ref[0, lvl * 8 + 2 * k]
        mi = sc_ref[0, lvl * 8 + 2 * k + 1]
        sre = jnp.where(fwd, pltpu.roll(re, dd, 0), pltpu.roll(re, SUBLANES - dd, 0))
        sim = jnp.where(fwd, pltpu.roll(im, dd, 0), pltpu.roll(im, SUBLANES - dd, 0))
        re, im = re + mr * sre - mi * sim, im + mr * sim + mi * sre
    return re, im


def _scan_rows(bufs, lvl, base, ntiles, sc_ref, p_ref, p_base):
    re_ref, im_ref = bufs[lvl]
    fwd = lax.broadcasted_iota(jnp.int32, (SUBLANES, LANES), 1) < S5_STATE
    row = lax.broadcasted_iota(jnp.int32, (SUBLANES, LANES), 0)
    fwd1 = fwd[0:1, :]

    def carried(hre, him, cre, cim):
        pre = jnp.where(fwd, jnp.where(row == 0, cre, pltpu.roll(hre, 1, 0)),
                        jnp.where(row == SUBLANES - 1, cre, pltpu.roll(hre, SUBLANES - 1, 0)))
        pim = jnp.where(fwd, jnp.where(row == 0, cim, pltpu.roll(him, 1, 0)),
                        jnp.where(row == SUBLANES - 1, cim, pltpu.roll(him, SUBLANES - 1, 0)))
        return pre, pim

    if ntiles == 1:
        sl = pl.ds(base, SUBLANES)
        hre, him = _tile_scan(re_ref[sl, :], im_ref[sl, :], sc_ref, lvl, fwd)
        re_ref[sl, :] = hre
        im_ref[sl, :] = him
        if lvl == 0:
            zero = jnp.zeros((1, LANES), F32)
            pre, pim = carried(hre, him, zero, zero)
            p_ref[pl.ds(p_base, SUBLANES), pl.ds(0, LANES)] = pre
            p_ref[pl.ds(p_base, SUBLANES), pl.ds(LANES, LANES)] = pim
        return

    ere_ref, eim_ref = bufs[lvl + 1]
    ntn = (ntiles + SUBLANES - 1) // SUBLANES
    ere_ref[...] = jnp.zeros_like(ere_ref)
    eim_ref[...] = jnp.zeros_like(eim_ref)

    for i in range(ntiles):
        sl = pl.ds(base + i * SUBLANES, SUBLANES)
        hre, him = _tile_scan(re_ref[sl, :], im_ref[sl, :], sc_ref, lvl, fwd)
        re_ref[sl, :] = hre
        im_ref[sl, :] = him
        ere_ref[pl.ds(SUBLANES + i, 1), :] = jnp.where(fwd1, hre[SUBLANES - 1:SUBLANES, :], hre[0:1, :])
        eim_ref[pl.ds(SUBLANES + i, 1), :] = jnp.where(fwd1, him[SUBLANES - 1:SUBLANES, :], him[0:1, :])

    _scan_rows(bufs, lvl + 1, SUBLANES, ntn, sc_ref, None, 0)
    apr = sc_ref[0, lvl * 8 + 6]
    api = sc_ref[0, lvl * 8 + 7]

    for i in range(ntiles):
        sl = pl.ds(base + i * SUBLANES, SUBLANES)
        cre = jnp.where(fwd1, ere_ref[pl.ds(SUBLANES - 1 + i, 1), :], ere_ref[pl.ds(SUBLANES + 1 + i, 1), :])
        cim = jnp.where(fwd1, eim_ref[pl.ds(SUBLANES - 1 + i, 1), :], eim_ref[pl.ds(SUBLANES + 1 + i, 1), :])
        hre = re_ref[sl, :] + apr * cre - api * cim
        him = im_ref[sl, :] + apr * cim + api * cre
        if lvl == 0:
            psl = pl.ds(p_base + i * SUBLANES, SUBLANES)
            pre, pim = carried(hre, him, cre, cim)
            p_ref[psl, pl.ds(0, LANES)] = pre
            p_ref[psl, pl.ds(LANES, LANES)] = pim
        else:
            re_ref[sl, :] = hre
            im_ref[sl, :] = him


def _s5_core_kernel(nseq, u_ref, w1_ref, w2_ref, sc_ref, y_ref,
                    s_re, s_im, e1_re, e1_im, e2_re, e2_im, p_scr, yi_scr):
    rows = u_ref.shape[1]
    nk = rows // nseq
    u = u_ref[0]
    st = _dot(u, w1_ref[0, :, pl.ds(S5_CW, S5_NSTATE)])
    s_re[...] = st[:, :LANES]
    s_im[...] = st[:, LANES:]
    yi_scr[...] = _dot(u, w1_ref[0, :, pl.ds(0, S5_CW)])
    bufs = [(s_re, s_im), (e1_re, e1_im), (e2_re, e2_im)]
    for b in range(nseq):
        _scan_rows(bufs, 0, b * nk, nk // SUBLANES, sc_ref, p_scr, b * nk)
    y_ref[0] = (yi_scr[...] + _dot(p_scr[...].astype(BF16), w2_ref[0])).astype(BF16)


def _s5_core(u, w1, w2, sc, nseq):
    g, rows, cw = u.shape
    nk = rows // nseq
    nt0 = nk // SUBLANES
    e1_rows = ((nt0 + SUBLANES - 1) // SUBLANES) * SUBLANES + 2 * SUBLANES
    nt1 = (nt0 + SUBLANES - 1) // SUBLANES
    e2_rows = ((nt1 + SUBLANES - 1) // SUBLANES) * SUBLANES + 2 * SUBLANES
    assert nt1 <= SUBLANES * SUBLANES, "sequence too long for S5_LEVELS scan levels"
    return pl.pallas_call(
        functools.partial(_s5_core_kernel, nseq),
        grid=(g,),
        in_specs=[
            pl.BlockSpec((1, rows, cw), lambda i: (i, 0, 0)),
            pl.BlockSpec((1, cw, cw + S5_NSTATE), lambda i: (i, 0, 0)),
            pl.BlockSpec((1, S5_NSTATE, cw), lambda i: (i, 0, 0)),
            pl.BlockSpec((1, S5_LEVELS * 8, SUBLANES, LANES), lambda i: (i, 0, 0, 0)),
        ],
        out_specs=pl.BlockSpec((1, rows, cw), lambda i: (i, 0, 0)),
        out_shape=jax.ShapeDtypeStruct((g, rows, cw), BF16),
        scratch_shapes=[
            pltpu.VMEM((rows, LANES), F32), pltpu.VMEM((rows, LANES), F32),
            pltpu.VMEM((e1_rows, LANES), F32), pltpu.VMEM((e1_rows, LANES), F32),
            pltpu.VMEM((e2_rows, LANES), F32), pltpu.VMEM((e2_rows, LANES), F32),
            pltpu.VMEM((rows, 2 * LANES), F32),
            pltpu.VMEM((rows, cw), F32),
        ],
        compiler_params=_cparams(("parallel",)),
        name="s5_core",
    )(u, w1, w2, sc)


def _s5_post_kernel(x_ref, y_ref, gt_ref, wg_ref, o_ref, y_scr, buf_a, buf_b):
    nk = S5_TBK // S5_C
    nq = S5_C // SEGS
    seg = lax.broadcasted_iota(jnp.int32, (nk, LANES), 1) // S5_GROUP

    def load(n, gl):
        return y_ref[SEGS * (n // nq) + gl, :, pl.ds(LANES * (n % nq), LANES)]

    def store(n, i, v):
        y_scr[n // nq, pl.ds(SEGS * (n % nq) + i, nk, stride=S5_PITCH), :] = v.astype(F32)

    _seg_transpose8(D_MODEL // LANES * nq, load, store, buf_a, buf_b, seg)
    y = jnp.concatenate(
        [jnp.concatenate([y_scr[j, pl.ds(S5_PITCH * k, S5_C), :] for j in range(D_MODEL // LANES)], axis=-1)
         for k in range(nk)], axis=0)
    z = _dot(jax.nn.gelu(y).astype(BF16), wg_ref[...])
    m = z[:, :D_MODEL] * jax.nn.sigmoid(z[:, D_MODEL:])
    o_ref[0] = x_ref[0] + gt_ref[0] * m


def _s5_post(x, y, gate, w_glu):
    b, l, d = x.shape
    nb = l // S5_TBK
    nk = S5_TBK // S5_C
    mod = pl.BlockSpec((1, 1, d), lambda bi, i: (bi, 0, 0))
    return pl.pallas_call(
        _s5_post_kernel,
        grid=(b, nb),
        in_specs=[
            pl.BlockSpec((1, S5_TBK, d), lambda bi, i: (bi, i, 0)),
            pl.BlockSpec((S5_GROUPS, nk, S5_CW), lambda bi, i: (0, bi * nb + i, 0)),
            mod,
            pl.BlockSpec((d, 2 * d), lambda bi, i: (0, 0)),
        ],
        out_specs=pl.BlockSpec((1, S5_TBK, d), lambda bi, i: (bi, i, 0)),
        out_shape=jax.ShapeDtypeStruct((b, l, d), F32),
        scratch_shapes=[pltpu.VMEM((d // LANES, nk * S5_PITCH, LANES), F32)] + _seg_buffers(nk, BF16),
        compiler_params=_cparams(("parallel", "parallel")),
        name="s5_post",
    )(x, y, gate, w_glu)


def _mlp_kernel(x_ref, g_ref, sc_ref, sh_ref, gt_ref, w1_ref, w2_ref, o_ref, h_scr, acc_scr):
    j = pl.program_id(2)

    @pl.when(j == 0)
    def _():
        h_scr[...] = _norm_mod(x_ref[0], g_ref[...], sc_ref[0], sh_ref[0]).astype(BF16)
        acc_scr[...] = jnp.zeros_like(acc_scr)

    a = jnp.maximum(_dot(h_scr[...], w1_ref[...]), 0.0)
    acc_scr[...] += _dot((a * a).astype(BF16), w2_ref[...])

    @pl.when(j == pl.num_programs(2) - 1)
    def _():
        o_ref[0] = x_ref[0] + gt_ref[0] * acc_scr[...]


def _mlp(x, norm_g, scale, shift, gate, w1, w2):
    b, l, d = x.shape
    ff = w1.shape[1]
    tm = min(MLP_TM, l)
    tf = MLP_TF
    vec = pl.BlockSpec((1, d), lambda bi, i, j: (0, 0))
    mod = pl.BlockSpec((1, 1, d), lambda bi, i, j: (bi, 0, 0))
    return pl.pallas_call(
        _mlp_kernel,
        grid=(b, l // tm, ff // tf),
        in_specs=[
            pl.BlockSpec((1, tm, d), lambda bi, i, j: (bi, i, 0)),
            vec, mod, mod, mod,
            pl.BlockSpec((d, tf), lambda bi, i, j: (0, j)),
            pl.BlockSpec((tf, d), lambda bi, i, j: (j, 0)),
        ],
        out_specs=pl.BlockSpec((1, tm, d), lambda bi, i, j: (bi, i, 0)),
        out_shape=jax.ShapeDtypeStruct((b, l, d), F32),
        scratch_shapes=[pltpu.VMEM((tm, d), BF16), pltpu.VMEM((tm, d), F32)],
        compiler_params=_cparams(("parallel", "parallel", "arbitrary")),
        name="mlp",
    )(x, norm_g.reshape(1, d), scale, shift, gate, w1, w2)


def _gla_pre_kernel(x_ref, g_ref, sc_ref, sh_ref, win_ref, wa1_ref, wa2_ref, ba_ref,
                    q_ref, k_ref, v_ref, r_ref, gf_ref, gb_ref):
    h = _norm_mod(x_ref[0], g_ref[...], sc_ref[0], sh_ref[0]).astype(BF16)
    proj = _dot(h, win_ref[...])
    q_ref[0] = proj[:, :GLA_DK] * (GLA_HEAD_K ** -0.5)
    k_ref[0] = proj[:, GLA_DK:2 * GLA_DK]
    v_ref[0] = proj[:, 2 * GLA_DK:2 * GLA_DK + GLA_DV].astype(BF16)
    r_ref[0] = proj[:, 2 * GLA_DK + GLA_DV:].astype(BF16)
    a = _dot(h, wa1_ref[...]).astype(BF16)
    z = _dot(a, wa2_ref[...]) + ba_ref[...]
    lg = (jnp.minimum(z, 0.0) - jnp.log(1.0 + jnp.exp(-jnp.abs(z)))) * (1.0 / GLA_GATE_TAU)
    gf_ref[0] = lg[:, :GLA_DK]
    gb_ref[0] = lg[:, GLA_DK:]


def _gla_pre(x, norm_g, scale, shift, w_in, w_a1c, w_a2bd, b_ac):
    b, l, d = x.shape
    tm = GLA_TM
    vec =pl.BlockSpec((1, d), lambda bi, i: (0, 0))
    mod = pl.BlockSpec((1, 1, d), lambda bi, i: (bi, 0, 0))

    def full(a):
        return pl.BlockSpec(a.shape, lambda bi, i: (0,) * a.ndim)

    def tok(w):
        return pl.BlockSpec((1, tm, w), lambda bi, i: (bi, i, 0))

    return pl.pallas_call(
        _gla_pre_kernel,
        grid=(b, l // tm),
        in_specs=[tok(d), vec, mod, mod, full(w_in), full(w_a1c), full(w_a2bd), full(b_ac)],
        out_specs=[tok(GLA_DK), tok(GLA_DK), tok(GLA_DV), tok(GLA_DV), tok(GLA_DK), tok(GLA_DK)],
        out_shape=[
            jax.ShapeDtypeStruct((b, l, GLA_DK), F32),
            jax.ShapeDtypeStruct((b, l, GLA_DK), F32),
            jax.ShapeDtypeStruct((b, l, GLA_DV), BF16),
            jax.ShapeDtypeStruct((b, l, GLA_DV), BF16),
            jax.ShapeDtypeStruct((b, l, GLA_DK), F32),
            jax.ShapeDtypeStruct((b, l, GLA_DK), F32),
        ],
        compiler_params=_cparams(("parallel", "parallel")),
        name="gla_pre",
    )(x, norm_g.reshape(1, d), scale, shift, w_in, w_a1c, w_a2bd, b_ac)


def _gla_block(q_ref, k_ref, v_ref, g_ref, st_ref, bc_ref, o_ref, fwd):
    c = GLA_CHUNK
    nb = GLA_BLOCK
    n = nb // c
    row = lax.broadcasted_iota(jnp.int32, (nb, nb), 0)
    col = lax.broadcasted_iota(jnp.int32, (nb, nb), 1)
    causal = (row >= col) if fwd else (row <= col)
    cum_m = (causal & ((row // c) == (col // c))).astype(BF16)
    g = g_ref[0]
    g1 = g.astype(BF16)
    g2 = (g - g1.astype(F32)).astype(BF16)
    bc_ref[...] = _dot(cum_m, g1) + _dot(cum_m, g2)
    last = c - 1 if fwd else 0
    sub = [slice(c * j, c * (j + 1)) for j in range(n)]
    order = list(range(n)) if fwd else list(range(n - 1, -1, -1))
    pos = {j: a for a, j in enumerate(order)}
    for h in range(GLA_HEADS):
        kl = pl.ds(GLA_HEAD_K * h, GLA_HEAD_K)
        bcum = bc_ref[:, kl]
        blast = [bcum[c * j + last:c * j + last + 1, :] for j in range(n)]

        def total(select):
            terms = [blast[m] for m in range(n) if select(m)]
            return sum(terms[1:], terms[0]) if terms else None

        def between(j, i):
            return total(lambda m: pos[j] < pos[m] < pos[i])

        def scaled(x, e):
            return x if e is None else x * jnp.exp(e)

        qh = q_ref[0, :, kl]
        kh = k_ref[0, :, kl]
        qd = qh * jnp.exp(bcum)
        kd = kh * jnp.exp(-bcum)
        kt = jnp.concatenate([kd[sub[j], :] * jnp.exp(blast[j]) for j in range(n)], axis=0)
        qd_b = qd.astype(BF16)
        kd_b = kd.astype(BF16)
        kt_b = kt.astype(BF16)
        vh = v_ref[0, :, pl.ds(GLA_HEAD_V * h, GLA_HEAD_V)]
        dec = jnp.exp(total(lambda m: True))

        def before(j):
            return total(lambda m: pos[m] < pos[j])

        def after(j):
            return total(lambda m: pos[m] > pos[j])

        q_in = jnp.concatenate([scaled(qd[sub[j], :], before(j)) for j in range(n)], axis=0)
        k_out = jnp.concatenate([scaled(kt[sub[j], :], after(j)) for j in range(n)], axis=0)
        rows = []
        for i in range(n):
            keys = []
            for j in range(n):
                if j == i or pos[j] > pos[i]:
                    keys.append(kd_b[sub[j], :])
                elif between(j, i) is None:
                    keys.append(kt_b[sub[j], :])
                else:
                    keys.append(scaled(kt[sub[j], :], between(j, i)).astype(BF16))
            rows.append(_dot_nt(qd_b[sub[i], :], jnp.concatenate(keys, axis=0)))
        scores = jnp.where(causal, jnp.concatenate(rows, axis=0), 0.0)
        s_t = st_ref[h]
        o = _dot(scores.astype(BF16), vh) + _dot_nt(q_in.astype(BF16), s_t.astype(BF16))
        o_ref[0, :, pl.ds(GLA_HEAD_V * h, GLA_HEAD_V)] = o.astype(BF16)
        st_ref[h] = s_t * dec + _dot_tn(vh, k_out.astype(BF16))


def _gla_core_kernel(qf_ref, kf_ref, vf_ref, gf_ref, qb_ref, kb_ref, vb_ref, gb_ref,
                     of_ref, ob_ref, stf_scr, stb_scr, bcf_scr, bcb_scr):
    @pl.when(pl.program_id(1) == 0)
    def _():
        stf_scr[...] = jnp.zeros_like(stf_scr)
        stb_scr[...] = jnp.zeros_like(stb_scr)

    _gla_block(qf_ref, kf_ref, vf_ref, gf_ref, stf_scr, bcf_scr, of_ref, True)
    _gla_block(qb_ref, kb_ref, vb_ref, gb_ref, stb_scr, bcb_scr, ob_ref, False)


def _gla_core(q, k, v, gf, gb):
    b, l, _ = q.shape
    c = GLA_BLOCK
    n = l // c

    def fw(w):
        return pl.BlockSpec((1, c, w), lambda bi, i: (bi, i, 0))

    def bw(w):
        return pl.BlockSpec((1, c, w), lambda bi, i: (bi, n - 1 - i, 0))

    return pl.pallas_call(
        _gla_core_kernel,
        grid=(b, n),
        in_specs=[fw(GLA_DK), fw(GLA_DK), fw(GLA_DV), fw(GLA_DK),
                  bw(GLA_DK), bw(GLA_DK), bw(GLA_DV), bw(GLA_DK)],
        out_specs=[fw(GLA_DV), bw(GLA_DV)],
        out_shape=[jax.ShapeDtypeStruct((b, l, GLA_DV), BF16),
                   jax.ShapeDtypeStruct((b, l, GLA_DV), BF16)],
        scratch_shapes=[pltpu.VMEM((GLA_HEADS, GLA_HEAD_V, GLA_HEAD_K), F32),
                        pltpu.VMEM((GLA_HEADS, GLA_HEAD_V, GLA_HEAD_K), F32),
                        pltpu.VMEM((c, GLA_DK), F32), pltpu.VMEM((c, GLA_DK), F32)],
        compiler_params=_cparams(("parallel", "arbitrary")),
        name="gla_core",
    )(q, k, v, gf, q, k, v, gb)


def _gla_out_mlp_kernel(x_ref, of_ref, ob_ref, r_ref, ng_ref, gt1_ref, wo_ref,
                        g_ref, sc_ref, sh_ref, gt2_ref, w1_ref, w2_ref, fg_ref, o_ref,
                        x_scr, h_scr, acc_scr):
    j = pl.program_id(2)

    @pl.when(j == 0)
    def _():
        o = of_ref[0].astype(F32) + ob_ref[0].astype(F32)
        parts = []
        for h in range(GLA_HEADS):
            oh = o[:, GLA_HEAD_V * h:GLA_HEAD_V * (h + 1)]
            parts.append(oh * lax.rsqrt(jnp.mean(oh * oh, axis=-1, keepdims=True) + EPS))
        on = jnp.concatenate(parts, axis=-1) * ng_ref[...]
        r = r_ref[0].astype(F32)
        gated = on * (r * jax.nn.sigmoid(r))
        x = x_ref[0] + gt1_ref[0] * _dot(gated.astype(BF16), wo_ref[...])
        x_scr[...] = x
        h_scr[...] = _norm_mod(x, g_ref[...], sc_ref[0], sh_ref[0]).astype(BF16)
        acc_scr[...] = jnp.zeros_like(acc_scr)

    a = jnp.maximum(_dot(h_scr[...], w1_ref[...]), 0.0)
    acc_scr[...] += _dot((a * a).astype(BF16), w2_ref[...])

    @pl.when(j == pl.num_programs(2) - 1)
    def _():
        out = x_scr[...] + gt2_ref[0] * acc_scr[...]
        o_ref[0] = out * lax.rsqrt(jnp.mean(out * out, axis=-1, keepdims=True) + EPS) * fg_ref[...]


def _gla_out_mlp(x, o_f, o_b, r, gla_norm_g, gate1, w_out, norm_g, scale, shift, gate2, w1, w2,
                 final_g):
    b, l, d = x.shape
    ff = w1.shape[1]
    tm = GLA_TM
    tf = MLP_TF
    tok = pl.BlockSpec((1, tm, d), lambda bi, i, j: (bi, i, 0))
    vec = pl.BlockSpec((1, d), lambda bi, i, j: (0, 0))
    mod = pl.BlockSpec((1, 1, d), lambda bi, i, j: (bi, 0, 0))
    return pl.pallas_call(
        _gla_out_mlp_kernel,
        grid=(b, l // tm, ff // tf),
        in_specs=[
            tok, tok, tok, tok, vec, mod, pl.BlockSpec((d, d), lambda bi, i, j: (0, 0)),
            vec, mod, mod, mod,
            pl.BlockSpec((d, tf), lambda bi, i, j: (0, j)),
            pl.BlockSpec((tf, d), lambda bi, i, j: (j, 0)),
            vec,
        ],
        out_specs=tok,
        out_shape=jax.ShapeDtypeStruct((b, l, d), F32),
        scratch_shapes=[pltpu.VMEM((tm, d), F32), pltpu.VMEM((tm, d), BF16),
                        pltpu.VMEM((tm, d), F32)],
        compiler_params=_cparams(("parallel", "parallel", "arbitrary")),
        name="gla_out_mlp",
    )(x, o_f, o_b, r, gla_norm_g.reshape(1, d), gate1, w_out, norm_g.reshape(1, d), scale, shift,
      gate2, w1, w2, final_g.reshape(1, d))


def _trunk(x, mod, wts):
    b = x.shape[0]

    def mods(layer):
        m = mod[layer].reshape(b, N_MOD, 1, D_MODEL)
        return [m[:, i] for i in range(N_MOD)]

    shift1, scale1, gate1, shift2, scale2, gate2 = mods(0)
    u = _s5_pre(x, wts["norm1_g"][0], scale1, shift1)
    y = _s5_core(u, wts["s5_w1"], wts["s5_w2"], wts["s5_sc"], b)
    x = _s5_post(x, y, gate1, wts["s5_w_glu"])
    x = _mlp(x, wts["norm2_g"][0], scale2, shift2, gate2, wts["mlp_w1"][0], wts["mlp_w2"][0])
    shift1, scale1, gate1, shift2, scale2, gate2 = mods(1)
    q, k, v, r, gf, gb = _gla_pre(x, wts["norm1_g"][1], scale1, shift1, wts["gla_w_in"],
                                  wts["gla_w_a1"], wts["gla_w_a2"], wts["gla_b_a"])
    o_f, o_b = _gla_core(q, k, v, gf, gb)
    return _gla_out_mlp(x, o_f, o_b, r, wts["gla_norm_g"], gate1, wts["gla_w_out"],
                        wts["norm2_g"][1], scale2, shift2, gate2, wts["mlp_w1"][1],
                        wts["mlp_w2"][1], wts["final_g"])


def kernel(x_prompt, x_sample, c_prompt, c_sample, ada_w, ada_b, norm1_g, norm2_g, s5_lam_re, s5_lam_im, s5_log_dt, s5_b_re, s5_b_im, s5_c_re, s5_c_im, s5_d, s5_w_glu, gla_w_in, gla_w_a1, gla_w_a2, gla_b_a, gla_norm_g, gla_w_out, mlp_w1, mlp_w2, final_g):
    bp, bs = c_prompt.shape[0], c_sample.shape[0]
    pad = (-(bp + bs)) % SUBLANES
    c_all = jnp.concatenate([c_prompt, c_sample, jnp.zeros((pad, D_MODEL), F32)], axis=0)
    mod_all = _modulation(c_all, ada_w, ada_b)

    s5_w1, s5_w2, s5_sc = _s5_build(s5_lam_re[0], s5_lam_im[0], s5_log_dt[0], s5_b_re[0],
                                    s5_b_im[0], s5_c_re[0], s5_c_im[0], s5_d[0])
    r = GLA_GATE_RANK
    w_a2bd = jnp.zeros((2 * r, 2 * GLA_DK), F32)
    w_a2bd = w_a2bd.at[:r, :GLA_DK].set(gla_w_a2[0, 0]).at[r:, GLA_DK:].set(gla_w_a2[0, 1])
    wts = {
        "norm1_g": norm1_g, "norm2_g": norm2_g, "final_g": final_g,
        "s5_w1": s5_w1, "s5_w2": s5_w2, "s5_sc": s5_sc,
        "s5_w_glu": s5_w_glu[0].astype(BF16),
        "mlp_w1": mlp_w1.astype(BF16), "mlp_w2": mlp_w2.astype(BF16),
        "gla_w_in": gla_w_in[0].astype(BF16),
        "gla_w_a1": jnp.concatenate([gla_w_a1[0, 0], gla_w_a1[0, 1]], axis=1).astype(BF16),
        "gla_w_a2": w_a2bd.astype(BF16),
        "gla_b_a": jnp.concatenate([gla_b_a[0, 0], gla_b_a[0, 1]], axis=0).reshape(1, 2 * GLA_DK),
        "gla_norm_g": gla_norm_g[0], "gla_w_out": gla_w_out[0].astype(BF16),
    }
    y_prompt = _trunk(x_prompt, mod_all[:, :bp], wts)
    y_sample = _trunk(x_sample, mod_all[:, bp:bp + bs], wts)
    return (y_prompt, y_sample)
```

```python
import functools

import jax
import jax.numpy as jnp
from jax import lax
from jax.experimental import pallas as pl
from jax.experimental.pallas import tpu as pltpu

F32 = jnp.float32
BF16 = jnp.bfloat16

D_MODEL = 1024
S5_GROUP = 16
S5_GROUPS = D_MODEL // S5_GROUP
S5_STATE = 64
GLA_HEADS = 4
GLA_DK = D_MODEL // 2
GLA_DV = D_MODEL
GLA_HEAD_K = GLA_DK // GLA_HEADS
GLA_HEAD_V = GLA_DV // GLA_HEADS
GLA_GATE_RANK = 16
GLA_GATE_TAU = 16.0
GLA_CHUNK = 64
GLA_BLOCK = 256
GLA_SEQS = 2
N_MOD = 6
EPS = 1e-6

LANES = 128
SUBLANES = 8
SEGS = LANES // S5_GROUP

S5_C = 32
S5_CW = S5_C * S5_GROUP
S5_NSTATE = 4 * S5_STATE
S5_CORE_GROUPS = 2
S5_LEVELS = 3
S5_TBK = 1024
S5_PITCH = S5_C + 4

MOD_TN = 1536
MLP_TM = 1024
MLP_TF = 2048
GLA_TM = 512

VMEM_LIMIT = 56 * 1024 * 1024


def _cparams(sem):
    return pltpu.CompilerParams(dimension_semantics=sem, vmem_limit_bytes=VMEM_LIMIT)


def _dot(a, b, precision=None):
    return jnp.dot(a, b, preferred_element_type=F32, precision=precision)


def _dot_nt(a, b, precision=None):
    return lax.dot_general(a, b, (((1,), (1,)), ((), ())),
                           preferred_element_type=F32, precision=precision)


def _dot_tn(a, b, precision=None):
    return lax.dot_general(a, b, (((0,), (0,)), ((), ())),
                           preferred_element_type=F32, precision=precision)


def _norm_mod(x, g, scale, shift):
    y = x * lax.rsqrt(jnp.mean(x * x, axis=-1, keepdims=True) + EPS)
    return y * (g * (1.0 + scale)) + shift


def _mod_kernel(c_ref, w_ref, b_ref, o_ref):
    c = c_ref[...]
    s = c * jax.nn.sigmoid(c)
    o_ref[0] = _dot(s.astype(BF16), w_ref[0].astype(BF16)) + b_ref[0]


def _modulation(c_all, ada_w, ada_b):
    depth, d, n = ada_w.shape
    rows = c_all.shape[0]
    tn = MOD_TN
    return pl.pallas_call(
        _mod_kernel,
        grid=(depth, n // tn),
        in_specs=[
            pl.BlockSpec((rows, d), lambda l, j: (0, 0)),
            pl.BlockSpec((1, d, tn), lambda l, j: (l, 0, j)),
            pl.BlockSpec((1, 1, tn), lambda l, j: (l, 0, j)),
        ],
        out_specs=pl.BlockSpec((1, rows, tn), lambda l, j: (l, 0, j)),
        out_shape=jax.ShapeDtypeStruct((depth, rows, n), F32),
        compiler_params=_cparams(("parallel", "parallel")),
        name="adaln_mod",
    )(c_all, ada_w, ada_b.reshape(depth, 1, n))


def _seg_exchange(lo, hi, d, seg):
    keep = (seg & d) == 0
    return (jnp.where(keep, lo, pltpu.roll(hi, S5_GROUP * d, 1)),
            jnp.where(keep, pltpu.roll(lo, LANES - S5_GROUP * d, 1), hi))


def _seg_transpose8(ngroups, load, store, buf_a, buf_b, seg):
    def put_a(n, i, v):
        buf_a[n, i] = v

    def put_b(n, i, v):
        buf_b[n, i] = v

    stages = ((4, load, put_a),
              (2, lambda n, i: buf_a[n, i], put_b),
              (1, lambda n, i: buf_b[n, i], store))
    for d, get, put in stages:
        for n in range(ngroups):
            for i in range(SEGS):
                if i & d == 0:
                    lo, hi = _seg_exchange(get(n, i), get(n, i + d), d, seg)
                    put(n, i, lo)
                    put(n, i + d, hi)


def _seg_buffers(rows, dtype):
    ngroups = D_MODEL // LANES * (S5_C // SEGS)
    return [pltpu.VMEM((ngroups, SEGS, rows, LANES), dtype) for _ in range(2)]


def _s5_pre_kernel(x_ref, g_ref, sc_ref, sh_ref, u_ref, h_scr, buf_a, buf_b):
    h = _norm_mod(x_ref[0], g_ref[...], sc_ref[0], sh_ref[0])
    nk = S5_TBK // S5_C
    nq = S5_C // SEGS
    for k in range(nk):
        for j in range(D_MODEL // LANES):
            h_scr[j, pl.ds(S5_PITCH * k, S5_C), :] = h[S5_C * k:S5_C * (k + 1), LANES * j:LANES * (j + 1)]
    seg = lax.broadcasted_iota(jnp.int32, (nk, LANES), 1) // S5_GROUP

    def load(n, i):
        return h_scr[n // nq, pl.ds(SEGS * (n % nq) + i, nk, stride=S5_PITCH), :]

    def store(n, gl, v):
        u_ref[SEGS * (n // nq) + gl, :, pl.ds(LANES * (n % nq), LANES)] = v.astype(BF16)

    _seg_transpose8(D_MODEL // LANES * nq, load, store, buf_a, buf_b, seg)


def _s5_pre(x, norm_g, scale, shift):
    b, l, d = x.shape
    nb = l // S5_TBK
    nk = S5_TBK // S5_C
    return pl.pallas_call(
        _s5_pre_kernel,
        grid=(b, nb),
        in_specs=[
            pl.BlockSpec((1, S5_TBK, d), lambda bi, i: (bi, i, 0)),
            pl.BlockSpec((1, d), lambda bi, i: (0, 0)),
            pl.BlockSpec((1, 1, d), lambda bi, i: (bi, 0, 0)),
            pl.BlockSpec((1, 1, d), lambda bi, i: (bi, 0, 0)),
        ],
        out_specs=pl.BlockSpec((S5_GROUPS, nk, S5_CW), lambda bi, i: (0, bi * nb + i, 0)),
        out_shape=jax.ShapeDtypeStruct((S5_GROUPS, b * l // S5_C, S5_CW), BF16),
        scratch_shapes=[pltpu.VMEM((d // LANES, nk * S5_PITCH, LANES), F32)] + _seg_buffers(nk, F32),
        compiler_params=_cparams(("parallel", "parallel")),
        name="s5_pre",
    )(x, norm_g.reshape(1, d), scale, shift)


def _s5_build_kernel(rows_ref, rowsfb_ref, bcat_ref, bswp_ref, ccat_ref, cswp_ref, dsk_ref,
                     w1_ref, w2_ref, sc_ref, ex_scr, mat_scr, kt_scr):
    c = S5_C
    cw = S5_CW
    hi = lax.Precision.HIGHEST
    nt = ((c + 1 + SUBLANES - 1) // SUBLANES) * SUBLANES
    lane1 = lax.broadcasted_iota(jnp.int32, (1, LANES), 1)
    sgn = jnp.where(lane1 < S5_STATE, -1.0, 1.0).astype(F32)
    nrow = lax.broadcasted_iota(jnp.int32, (nt, LANES), 0).astype(F32)

    for d in range(2):
        lr = rows_ref[0, d, 0:1, :]
        li = rows_ref[0, d, 1:2, :]
        dt = jnp.exp(rows_ref[0, d, 2:3, :])
        mag = jnp.exp(nrow * (lr * dt))
        ang = nrow * (li * dt)
        p_re = mag * jnp.cos(ang)
        p_im = mag * jnp.sin(ang)
        ab_re = p_re[1:2, :]
        ab_im = p_im[1:2, :]
        den = lr * lr + li * li
        z_re = ((ab_re - 1.0) * lr + ab_im * li) / den
        z_im = (ab_im * lr - (ab_re - 1.0) * li) / den
        bbar = z_re * bcat_ref[0, d] + (z_im * sgn) * bswp_ref[0, d]
        p_is = p_im * sgn
        for n in range(c + 1):
            blk = pl.ds(S5_GROUP * n, S5_GROUP)
            ex_scr[0, blk, :] = jnp.broadcast_to(p_re[n:n + 1, :], (S5_GROUP, LANES))
            ex_scr[1, blk, :] = jnp.broadcast_to(p_is[n:n + 1, :], (S5_GROUP, LANES))
            ex_scr[2, blk, :] = jnp.broadcast_to(p_re[c - n:c - n + 1, :], (S5_GROUP, LANES))
            ex_scr[3, blk, :] = jnp.broadcast_to(p_is[c - n:c - n + 1, :], (S5_GROUP, LANES))
        c_t = jnp.concatenate([ccat_ref[0, d]] * c, axis=0)
        c_s = jnp.concatenate([cswp_ref[0, d]] * c, axis=0)
        bswap = z_re * bswp_ref[0, d] - (z_im * sgn) * bcat_ref[0, d]
        b_t = jnp.concatenate([bbar] * c, axis=0)
        b_s = jnp.concatenate([bswap] * c, axis=0)
        lo = pl.ds(0, cw)
        up = pl.ds(S5_GROUP, cw)
        bneg = bbar * (-sgn)
        if d == 0:
            ct = ex_scr[0, lo, :] * c_t + ex_scr[1, lo, :] * c_s
            kt_scr[0] = _dot_nt(bneg, ct, hi)
            mat_scr[0] = ex_scr[2, up, :] * b_t + ex_scr[3, up, :] * b_s
            mat_scr[2] = ex_scr[0, up, :] * c_t + ex_scr[1, up, :] * c_s
        else:
            ct = ex_scr[2, up, :] * c_t + ex_scr[3, up, :] * c_s
            kt_scr[1] = _dot_nt(bneg, ct, hi)
            mat_scr[1] = ex_scr[0, lo, :] * b_t + ex_scr[1, lo, :] * b_s
            mat_scr[3] = ex_scr[2, lo, :] * c_t + ex_scr[3, lo, :] * c_s

    lane_cw = lax.broadcasted_iota(jnp.int32, (S5_GROUP, cw), 1)
    chan = lax.broadcasted_iota(jnp.int32, (S5_GROUP, cw), 0)
    dsk = dsk_ref[0]
    kf = kt_scr[0]
    kb = kt_scr[1]
    for s in range(c):
        rf = kf if s == 0 else pltpu.roll(kf, S5_GROUP * s, 1)
        rb = kb if s == c - 1 else pltpu.roll(kb, S5_GROUP * (s + 1), 1)
        blk = (jnp.where(lane_cw >= S5_GROUP * s, rf, 0.0)
               + jnp.where(lane_cw < S5_GROUP * (s + 1), rb, 0.0)
               + jnp.where(lane_cw == S5_GROUP * s + chan, dsk, 0.0))
        w1_ref[0, pl.ds(S5_GROUP * s, S5_GROUP), pl.ds(0, cw)] = blk.astype(BF16)

    lane_m = lax.broadcasted_iota(jnp.int32, (cw, LANES), 1) < S5_STATE
    bmf = mat_scr[0]
    bmb = mat_scr[1]
    w1_ref[0, :, pl.ds(cw, LANES)] = jnp.where(
        lane_m, bmf, pltpu.roll(bmb, S5_STATE, 1)).astype(BF16)
    w1_ref[0, :, pl.ds(cw + LANES, LANES)] = jnp.where(
        lane_m, pltpu.roll(bmf, S5_STATE, 1), bmb).astype(BF16)
    caf = mat_scr[2]
    cab = mat_scr[3]
    w2t_re = jnp.where(lane_m, caf, pltpu.roll(cab, S5_STATE, 1))
    w2t_im = -jnp.where(lane_m, pltpu.roll(caf, S5_STATE, 1), cab)
    w2_ref[0, pl.ds(0, LANES), :] = w2t_re.T.astype(BF16)
    w2_ref[0, pl.ds(LANES, LANES), :] = w2t_im.T.astype(BF16)

    lrfb = rowsfb_ref[0, 0:1, :]
    lifb = rowsfb_ref[0, 1:2, :]
    dtfb = jnp.exp(rowsfb_ref[0, 2:3, :])
    row8 = lax.broadcasted_iota(jnp.int32, (SUBLANES, LANES), 0)
    fwd8 = lax.broadcasted_iota(jnp.int32, (SUBLANES, LANES), 1) < S5_STATE
    mag_c = jnp.exp(float(c) * (lrfb * dtfb))
    ang_c = float(c) * (lifb * dtfb)
    base = (mag_c * jnp.cos(ang_c), mag_c * jnp.sin(ang_c))

    def cmul(a, b):
        return a[0] * b[0] - a[1] * b[1], a[0] * b[1] + a[1] * b[0]

    for lvl in range(S5_LEVELS):
        pw = [base]
        for _ in range(SUBLANES - 1):
            pw.append(cmul(pw[-1], base))
        for k, dd in enumerate((1, 2, 4)):
            ok = (fwd8 & (row8 >= dd)) | (jnp.logical_not(fwd8) & (row8 < SUBLANES - dd))
            sc_ref[0, lvl * 8 + 2 * k] = jnp.where(ok, pw[dd - 1][0], 0.0)
            sc_ref[0, lvl * 8 + 2 * k + 1] = jnp.where(ok, pw[dd - 1][1], 0.0)
        ap_re = jnp.zeros((SUBLANES, LANES), F32)
        ap_im = jnp.zeros((SUBLANES, LANES), F32)
        for n in range(1, SUBLANES + 1):
            here = (fwd8 & (row8 == n - 1)) | (jnp.logical_not(fwd8) & (row8 == SUBLANES - n))
            ap_re = jnp.where(here, pw[n - 1][0], ap_re)
            ap_im = jnp.where(here, pw[n - 1][1], ap_im)
        sc_ref[0, lvl * 8 + 6] = ap_re
        sc_ref[0, lvl * 8 + 7] = ap_im
        base = pw[SUBLANES - 1]


def _s5_build(lam_re, lam_im, log_dt, b_re, b_im, c_re, c_im, d_skip):
    g, p = S5_GROUPS, S5_STATE
    dsk = jnp.tile(d_skip.reshape(g, 1, S5_GROUP), (1, 1, S5_C))

    def dup(a):
        return jnp.concatenate([a, a], axis=-1).transpose(1, 0, 2)[:, :, None, :]

    ldt = jnp.broadcast_to(log_dt.T[:, :, None, None], (g, 2, 1, 2 * p))
    rows = jnp.concatenate(
        [dup(lam_re), dup(lam_im), ldt, jnp.zeros((g, 2, SUBLANES - 3, 2 * p), F32)], axis=2)

    def fb(a):
        return jnp.concatenate([a[0], a[1]], axis=-1)[:, None, :]

    ldt2 = jnp.broadcast_to(log_dt[:, :, None], (2, g, p))
    rowsfb = jnp.concatenate(
        [fb(lam_re), fb(lam_im), fb(ldt2), jnp.zeros((g, SUBLANES - 3, 2 * p), F32)], axis=1)
    brt = b_re.transpose(1, 0, 3, 2)
    bit = b_im.transpose(1, 0, 3, 2)
    crt = c_re.transpose(1, 0, 2, 3)
    cit = c_im.transpose(1, 0, 2, 3)
    bcat = jnp.concatenate([brt, bit], axis=-1)
    bswp = jnp.concatenate([bit, brt], axis=-1)
    ccat = jnp.concatenate([crt, cit], axis=-1)
    cswp = jnp.concatenate([cit, crt], axis=-1)
    nsc = S5_LEVELS * 8
    spec4 = pl.BlockSpec((1, 2, S5_GROUP, LANES), lambda i: (i, 0, 0, 0))
    return pl.pallas_call(
        _s5_build_kernel,
        grid=(g,),
        in_specs=[
            pl.BlockSpec((1, 2, SUBLANES, LANES), lambda i: (i, 0, 0, 0)),
            pl.BlockSpec((1, SUBLANES, LANES), lambda i: (i, 0, 0)),
            spec4, spec4, spec4, spec4,
            pl.BlockSpec((1, 1, S5_CW), lambda i: (i, 0, 0)),
        ],
        out_specs=[
            pl.BlockSpec((1, S5_CW, S5_CW + S5_NSTATE), lambda i: (i, 0, 0)),
            pl.BlockSpec((1, S5_NSTATE, S5_CW), lambda i: (i, 0, 0)),
            pl.BlockSpec((1, nsc, SUBLANES, LANES), lambda i: (i, 0, 0, 0)),
        ],
        out_shape=[
            jax.ShapeDtypeStruct((g, S5_CW, S5_CW + S5_NSTATE), BF16),
            jax.ShapeDtypeStruct((g, S5_NSTATE, S5_CW), BF16),
            jax.ShapeDtypeStruct((g, nsc, SUBLANES, LANES), F32),
        ],
        scratch_shapes=[
            pltpu.VMEM((4, (S5_C + 1) * S5_GROUP, LANES), F32),
            pltpu.VMEM((4, S5_CW, LANES), F32),
            pltpu.VMEM((2, S5_GROUP, S5_CW), F32),
        ],
        compiler_params=_cparams(("parallel",)),
        name="s5_build",
    )(rows, rowsfb, bcat, bswp, ccat, cswp, dsk)


def _tile_scan(re, im, sc_ref, lvl, fwd):
    for k, dd in enumerate((1, 2, 4)):
        mr = sc_ref[0, lvl * 8 + 2 * k]
        mi = sc_ref[0, lvl * 8 + 2 * k + 1]
        sre = jnp.where(fwd, pltpu.roll(re, dd, 0), pltpu.roll(re, SUBLANES - dd, 0))
        sim = jnp.where(fwd, pltpu.roll(im, dd, 0), pltpu.roll(im, SUBLANES - dd, 0))
        re, im = re + mr * sre - mi * sim, im + mr * sim + mi * sre
    return re, im


def _scan_rows(bufs, lvl, base, ntiles, sc_ref, p_ref, p_base):
    re_ref, im_ref = bufs[lvl]
    fwd = lax.broadcasted_iota(jnp.int32, (SUBLANES, LANES), 1) < S5_STATE
    row = lax.broadcasted_iota(jnp.int32, (SUBLANES, LANES), 0)
    fwd1 = fwd[0:1, :]

    def carried(hre, him, cre, cim):
        pre = jnp.where(fwd, jnp.where(row == 0, cre, pltpu.roll(hre, 1, 0)),
                        jnp.where(row == SUBLANES - 1, cre, pltpu.roll(hre, SUBLANES - 1, 0)))
        pim = jnp.where(fwd, jnp.where(row == 0, cim, pltpu.roll(him, 1, 0)),
                        jnp.where(row == SUBLANES - 1, cim, pltpu.roll(him, SUBLANES - 1, 0)))
        return pre, pim

    if ntiles == 1:
        sl = pl.ds(base, SUBLANES)
        hre, him = _tile_scan(re_ref[sl, :], im_ref[sl, :], sc_ref, lvl, fwd)
        re_ref[sl, :] = hre
        im_ref[sl, :] = him
        if lvl == 0:
            zero = jnp.zeros((1, LANES), F32)
            pre, pim = carried(hre, him, zero, zero)
            p_ref[pl.ds(p_base, SUBLANES), pl.ds(0, LANES)] = pre
            p_ref[pl.ds(p_base, SUBLANES), pl.ds(LANES, LANES)] = pim
        return

    ere_ref, eim_ref = bufs[lvl + 1]
    ntn = (ntiles + SUBLANES - 1) // SUBLANES
    ere_ref[...] = jnp.zeros_like(ere_ref)
    eim_ref[...] = jnp.zeros_like(eim_ref)

    for i in range(ntiles):
        sl = pl.ds(base + i * SUBLANES, SUBLANES)
        hre, him = _tile_scan(re_ref[sl, :], im_ref[sl, :], sc_ref, lvl, fwd)
        re_ref[sl, :] = hre
        im_ref[sl, :] = him
        ere_ref[pl.ds(SUBLANES + i, 1), :] = jnp.where(fwd1, hre[SUBLANES - 1:SUBLANES, :], hre[0:1, :])
        eim_ref[pl.ds(SUBLANES + i, 1), :] = jnp.where(fwd1, him[SUBLANES - 1:SUBLANES, :], him[0:1, :])

    _scan_rows(bufs, lvl + 1, SUBLANES, ntn, sc_ref, None, 0)
    apr = sc_ref[0, lvl * 8 + 6]
    api = sc_ref[0, lvl * 8 + 7]

    for i in range(ntiles):
        sl = pl.ds(base + i * SUBLANES, SUBLANES)
        cre = jnp.where(fwd1, ere_ref[pl.ds(SUBLANES - 1 + i, 1), :], ere_ref[pl.ds(SUBLANES + 1 + i, 1), :])
        cim = jnp.where(fwd1, eim_ref[pl.ds(SUBLANES - 1 + i, 1), :], eim_ref[pl.ds(SUBLANES + 1 + i, 1), :])
        hre = re_ref[sl, :] + apr * cre - api * cim
        him = im_ref[sl, :] + apr * cim + api * cre
        if lvl == 0:
            psl = pl.ds(p_base + i * SUBLANES, SUBLANES)
            pre, pim = carried(hre, him, cre, cim)
            p_ref[psl, pl.ds(0, LANES)] = pre
            p_ref[psl, pl.ds(LANES, LANES)] = pim
        else:
            re_ref[sl, :] = hre
            im_ref[sl, :] = him


def _s5_core_kernel(nseq, u_ref, w1_ref, w2_ref, sc_ref, y_ref,
                    s_re, s_im, e1_re, e1_im, e2_re, e2_im, p_scr, yi_scr):
    rows = u_ref.shape[1]
    nk = rows // nseq
    bufs = [(s_re, s_im), (e1_re, e1_im), (e2_re, e2_im)]
    for gi in range(S5_CORE_GROUPS):
        u = u_ref[gi]
        st = _dot(u, w1_ref[gi, :, pl.ds(S5_CW, S5_NSTATE)])
        s_re[...] = st[:, :LANES]
        s_im[...] = st[:, LANES:]
        yi_scr[...] = _dot(u, w1_ref[gi, :, pl.ds(0, S5_CW)])
        sc_g = sc_ref.at[pl.ds(gi, 1)]
        for b in range(nseq):
            _scan_rows(bufs, 0, b * nk, nk // SUBLANES, sc_g, p_scr, b * nk)
        y_ref[gi] = (yi_scr[...] + _dot(p_scr[...].astype(BF16), w2_ref[gi])).astype(BF16)


def _s5_core(u, w1, w2, sc, nseq):
    g, rows, cw = u.shape
    gs = S5_CORE_GROUPS
    nk = rows // nseq
    nt0 = nk // SUBLANES
    e1_rows = ((nt0 + SUBLANES - 1) // SUBLANES) * SUBLANES + 2 * SUBLANES
    nt1 = (nt0 + SUBLANES - 1) // SUBLANES
    e2_rows = ((nt1 + SUBLANES - 1) // SUBLANES) * SUBLANES + 2 * SUBLANES
    assert nt1 <= SUBLANES * SUBLANES, "sequence too long for S5_LEVELS scan levels"
    return pl.pallas_call(
        functools.partial(_s5_core_kernel, nseq),
        grid=(g // gs,),
        in_specs=[
            pl.BlockSpec((gs, rows, cw), lambda i: (i, 0, 0)),
            pl.BlockSpec((gs, cw, cw + S5_NSTATE), lambda i: (i, 0, 0)),
            pl.BlockSpec((gs, S5_NSTATE, cw), lambda i: (i, 0, 0)),
            pl.BlockSpec((gs, S5_LEVELS * 8, SUBLANES, LANES), lambda i: (i, 0, 0, 0)),
        ],
        out_specs=pl.BlockSpec((gs, rows, cw), lambda i: (i, 0, 0)),
        out_shape=jax.ShapeDtypeStruct((g, rows, cw), BF16),
        scratch_shapes=[
            pltpu.VMEM((rows, LANES), F32), pltpu.VMEM((rows, LANES), F32),
            pltpu.VMEM((e1_rows, LANES), F32), pltpu.VMEM((e1_rows, LANES), F32),
            pltpu.VMEM((e2_rows, LANES), F32), pltpu.VMEM((e2_rows, LANES), F32),
            pltpu.VMEM((rows, 2 * LANES), F32),
            pltpu.VMEM((rows, cw), F32),
        ],
        compiler_params=_cparams(("parallel",)),
        name="s5_core",
    )(u, w1, w2, sc)


def _s5_post_kernel(x_ref, y_ref, gt_ref, wg_ref, o_ref, y_scr, buf_a, buf_b):
    nk = S5_TBK // S5_C
    nq = S5_C // SEGS
    seg = lax.broadcasted_iota(jnp.int32, (nk, LANES), 1) // S5_GROUP

    def load(n, gl):
        return y_ref[SEGS * (n // nq) + gl, :, pl.ds(LANES * (n % nq), LANES)]

    def store(n, i, v):
        y_scr[n // nq, pl.ds(SEGS * (n % nq) + i, nk, stride=S5_PITCH), :] = v.astype(F32)

    _seg_transpose8(D_MODEL // LANES * nq, load, store, buf_a, buf_b, seg)
    y = jnp.concatenate(
        [jnp.concatenate([y_scr[j, pl.ds(S5_PITCH * k, S5_C), :] for j in range(D_MODEL // LANES)], axis=-1)
         for k in range(nk)], axis=0)
    z = _dot(jax.nn.gelu(y).astype(BF16), wg_ref[...])
    m = z[:, :D_MODEL] * jax.nn.sigmoid(z[:, D_MODEL:])
    o_ref[0] = x_ref[0] + gt_ref[0] * m


def _s5_post(x, y, gate, w_glu):
    b, l, d = x.shape
    nb = l // S5_TBK
    nk = S5_TBK // S5_C
    mod = pl.BlockSpec((1, 1, d), lambda bi, i: (bi, 0, 0))
    return pl.pallas_call(
        _s5_post_kernel,
        grid=(b, nb),
        in_specs=[
            pl.BlockSpec((1, S5_TBK, d), lambda bi, i: (bi, i, 0)),
            pl.BlockSpec((S5_GROUPS, nk, S5_CW), lambda bi, i: (0, bi * nb + i, 0)),
            mod,
            pl.BlockSpec((d, 2 * d), lambda bi, i: (0, 0)),
        ],
        out_specs=pl.BlockSpec((1, S5_TBK, d), lambda bi, i: (bi, i, 0)),
        out_shape=jax.ShapeDtypeStruct((b, l, d), F32),
        scratch_shapes=[pltpu.VMEM((d // LANES, nk * S5_PITCH, LANES), F32)] + _seg_buffers(nk, BF16),
        compiler_params=_cparams(("parallel", "parallel")),
        name="s5_post",
    )(x, y, gate, w_glu)


def _mlp_kernel(x_ref, g_ref, sc_ref, sh_ref, gt_ref, w1_ref, w2_ref, o_ref, h_scr, acc_scr):
    j = pl.program_id(2)

    @pl.when(j == 0)
    def _():
        h_scr[...] = _norm_mod(x_ref[0], g_ref[...], sc_ref[0], sh_ref[0]).astype(BF16)
        acc_scr[...] = jnp.zeros_like(acc_scr)

    a = jnp.maximum(_dot(h_scr[...], w1_ref[...]), 0.0)
    acc_scr[...] += _dot((a * a).astype(BF16), w2_ref[...])

    @pl.when(j == pl.num_programs(2) - 1)
    def _():
        o_ref[0] = x_ref[0] + gt_ref[0] * acc_scr[...]


def _mlp(x, norm_g, scale, shift, gate, w1, w2):
    b, l, d = x.shape
    ff = w1.shape[1]
    tm = min(MLP_TM, l)
    tf = MLP_TF
    vec = pl.BlockSpec((1, d), lambda bi, i, j: (0, 0))
    mod = pl.BlockSpec((1, 1, d), lambda bi, i, j: (bi, 0, 0))
    return pl.pallas_call(
        _mlp_kernel,
        grid=(b, l // tm, ff // tf),
        in_specs=[
            pl.BlockSpec((1, tm, d), lambda bi, i, j: (bi, i, 0)),
            vec, mod, mod, mod,
            pl.BlockSpec((d, tf), lambda bi, i, j: (0, j)),
            pl.BlockSpec((tf, d), lambda bi, i, j: (j, 0)),
        ],
        out_specs=pl.BlockSpec((1, tm, d), lambda bi, i, j: (bi, i, 0)),
        out_shape=jax.ShapeDtypeStruct((b, l, d), F32),
        scratch_shapes=[pltpu.VMEM((tm, d), BF16), pltpu.VMEM((tm, d), F32)],
        compiler_params=_cparams(("parallel", "parallel", "arbitrary")),
        name="mlp",
    )(x, norm_g.reshape(1, d), scale, shift, gate, w1, w2)


def _gla_pre_kernel(x_ref, g_ref, sc_ref, sh_ref, win_ref, wa1_ref, wa2_ref, ba_ref,
                    q_ref, k_ref, v_ref, r_ref, gf_ref, gb_ref):
    h = _norm_mod(x_ref[0], g_ref[...], sc_ref[0], sh_ref[0]).astype(BF16)
    proj = _dot(h, win_ref[...])
    q_ref[0] = proj[:, :GLA_DK] * (GLA_HEAD_K ** -0.5)
    k_ref[0] = proj[:, GLA_DK:2 * GLA_DK]
    v_ref[0] = proj[:, 2 * GLA_DK:2 * GLA_DK + GLA_DV].astype(BF16)
    r_ref[0] = proj[:, 2 * GLA_DK + GLA_DV:].astype(BF16)
    a = _dot(h, wa1_ref[...]).astype(BF16)
    z = _dot(a, wa2_ref[...]) + ba_ref[...]
    lg = (jnp.minimum(z, 0.0) - jnp.log(1.0 + jnp.exp(-jnp.abs(z)))) * (1.0 / GLA_GATE_TAU)
    gf_ref[0] = lg[:, :GLA_DK]
    gb_ref[0] = lg[:, GLA_DK:]


def _gla_pre(x, norm_g, scale, shift, w_in, w_a1c, w_a2bd, b_ac):
    b, l, d = x.shape
    tm = GLA_TM
    vec =pl.BlockSpec((1, d), lambda bi, i: (0, 0))
    mod = pl.BlockSpec((1, 1, d), lambda bi, i: (bi, 0, 0))

    def full(a):
        return pl.BlockSpec(a.shape, lambda bi, i: (0,) * a.ndim)

    def tok(w):
        return pl.BlockSpec((1, tm, w), lambda bi, i: (bi, i, 0))

    return pl.pallas_call(
        _gla_pre_kernel,
        grid=(b, l // tm),
        in_specs=[tok(d), vec, mod, mod, full(w_in), full(w_a1c), full(w_a2bd), full(b_ac)],
        out_specs=[tok(GLA_DK), tok(GLA_DK), tok(GLA_DV), tok(GLA_DV), tok(GLA_DK), tok(GLA_DK)],
        out_shape=[
            jax.ShapeDtypeStruct((b, l, GLA_DK), F32),
            jax.ShapeDtypeStruct((b, l, GLA_DK), F32),
            jax.ShapeDtypeStruct((b, l, GLA_DV), BF16),
            jax.ShapeDtypeStruct((b, l, GLA_DV), BF16),
            jax.ShapeDtypeStruct((b, l, GLA_DK), F32),
            jax.ShapeDtypeStruct((b, l, GLA_DK), F32),
        ],
        compiler_params=_cparams(("parallel", "parallel")),
        name="gla_pre",
    )(x, norm_g.reshape(1, d), scale, shift, w_in, w_a1c, w_a2bd, b_ac)


def _gla_block(e, q_ref, k_ref, v_ref, g_ref, st_ref, bc_ref, o_ref, fwd):
    c = GLA_CHUNK
    nb = GLA_BLOCK
    n = nb // c
    row = lax.broadcasted_iota(jnp.int32, (nb, nb), 0)
    col = lax.broadcasted_iota(jnp.int32, (nb, nb), 1)
    causal = (row >= col) if fwd else (row <= col)
    cum_m = (causal & ((row // c) == (col // c))).astype(BF16)
    g = g_ref[e]
    g1 = g.astype(BF16)
    g2 = (g - g1.astype(F32)).astype(BF16)
    bc_ref[...] = _dot(cum_m, g1) + _dot(cum_m, g2)
    last = c - 1 if fwd else 0
    sub = [slice(c * j, c * (j + 1)) for j in range(n)]
    order = list(range(n)) if fwd else list(range(n - 1, -1, -1))
    pos = {j: a for a, j in enumerate(order)}
    for h in range(GLA_HEADS):
        kl = pl.ds(GLA_HEAD_K * h, GLA_HEAD_K)
        bcum = bc_ref[:, kl]
        blast = [bcum[c * j + last:c * j + last + 1, :] for j in range(n)]

        def total(select):
            terms = [blast[m] for m in range(n) if select(m)]
            return sum(terms[1:], terms[0]) if terms else None

        def between(j, i):
            return total(lambda m: pos[j] < pos[m] < pos[i])

        def scaled(x, e):
            return x if e is None else x * jnp.exp(e)

        qh = q_ref[e, :, kl]
        kh = k_ref[e, :, kl]
        qd = qh * jnp.exp(bcum)
        kd = kh * jnp.exp(-bcum)
        kt = jnp.concatenate([kd[sub[j], :] * jnp.exp(blast[j]) for j in range(n)], axis=0)
        qd_b = qd.astype(BF16)
        kd_b = kd.astype(BF16)
        kt_b = kt.astype(BF16)
        vh = v_ref[e, :, pl.ds(GLA_HEAD_V * h, GLA_HEAD_V)]
        dec = jnp.exp(total(lambda m: True))

        def before(j):
            return total(lambda m: pos[m] < pos[j])

        def after(j):
            return total(lambda m: pos[m] > pos[j])

        q_in = jnp.concatenate([scaled(qd[sub[j], :], before(j)) for j in range(n)], axis=0)
        k_out = jnp.concatenate([scaled(kt[sub[j], :], after(j)) for j in range(n)], axis=0)
        rows = []
        for i in range(n):
            keys = []
            for j in range(n):
                if j == i or pos[j] > pos[i]:
                    keys.append(kd_b[sub[j], :])
                elif between(j, i) is None:
                    keys.append(kt_b[sub[j], :])
                else:
                    keys.append(scaled(kt[sub[j], :], between(j, i)).astype(BF16))
            rows.append(_dot_nt(qd_b[sub[i], :], jnp.concatenate(keys, axis=0)))
        scores = jnp.where(causal, jnp.concatenate(rows, axis=0), 0.0)
        s_t = st_ref[e, h]
        o = _dot(scores.astype(BF16), vh) + _dot_nt(q_in.astype(BF16), s_t.astype(BF16))
        o_ref[e, :, pl.ds(GLA_HEAD_V * h, GLA_HEAD_V)] = o.astype(BF16)
        st_ref[e, h] = s_t * dec + _dot_tn(vh, k_out.astype(BF16))


def _gla_core_kernel(qf_ref, kf_ref, vf_ref, gf_ref, qb_ref, kb_ref, vb_ref, gb_ref,
                     of_ref, ob_ref, stf_scr, stb_scr, bcf_scr, bcb_scr):
    @pl.when(pl.program_id(1) == 0)
    def _():
        stf_scr[...] = jnp.zeros_like(stf_scr)
        stb_scr[...] = jnp.zeros_like(stb_scr)

    for e in range(GLA_SEQS):
        _gla_block(e, qf_ref, kf_ref, vf_ref, gf_ref, stf_scr, bcf_scr, of_ref, True)
        _gla_block(e, qb_ref, kb_ref, vb_ref, gb_ref, stb_scr, bcb_scr, ob_ref, False)


def _gla_core(q, k, v, gf, gb):
    b, l, _ = q.shape
    c = GLA_BLOCK
    n = l // c
    ns = GLA_SEQS
    assert b % ns == 0 and l % c == 0

    def fw(w):
        return pl.BlockSpec((ns, c, w), lambda bi, i: (bi, i, 0))

    def bw(w):
        return pl.BlockSpec((ns, c, w), lambda bi, i: (bi, n - 1 - i, 0))

    return pl.pallas_call(
        _gla_core_kernel,
        grid=(b // ns, n),
        in_specs=[fw(GLA_DK), fw(GLA_DK), fw(GLA_DV), fw(GLA_DK),
                  bw(GLA_DK), bw(GLA_DK), bw(GLA_DV), bw(GLA_DK)],
        out_specs=[fw(GLA_DV), bw(GLA_DV)],
        out_shape=[jax.ShapeDtypeStruct((b, l, GLA_DV), BF16),
                   jax.ShapeDtypeStruct((b, l, GLA_DV), BF16)],
        scratch_shapes=[pltpu.VMEM((ns, GLA_HEADS, GLA_HEAD_V, GLA_HEAD_K), F32),
                        pltpu.VMEM((ns, GLA_HEADS, GLA_HEAD_V, GLA_HEAD_K), F32),
                        pltpu.VMEM((c, GLA_DK), F32), pltpu.VMEM((c, GLA_DK), F32)],
        compiler_params=_cparams(("parallel", "arbitrary")),
        name="gla_core",
    )(q, k, v, gf, q, k, v, gb)


def _gla_out_mlp_kernel(x_ref, of_ref, ob_ref, r_ref, ng_ref, gt1_ref, wo_ref,
                        g_ref, sc_ref, sh_ref, gt2_ref, w1_ref, w2_ref, fg_ref, o_ref,
                        x_scr, h_scr, acc_scr):
    j = pl.program_id(2)

    @pl.when(j == 0)
    def _():
        o = of_ref[0].astype(F32) + ob_ref[0].astype(F32)
        parts = []
        for h in range(GLA_HEADS):
            oh = o[:, GLA_HEAD_V * h:GLA_HEAD_V * (h + 1)]
            parts.append(oh * lax.rsqrt(jnp.mean(oh * oh, axis=-1, keepdims=True) + EPS))
        on = jnp.concatenate(parts, axis=-1) * ng_ref[...]
        r = r_ref[0].astype(F32)
        gated = on * (r * jax.nn.sigmoid(r))
        x = x_ref[0] + gt1_ref[0] * _dot(gated.astype(BF16), wo_ref[...])
        x_scr[...] = x
        h_scr[...] = _norm_mod(x, g_ref[...], sc_ref[0], sh_ref[0]).astype(BF16)
        acc_scr[...] = jnp.zeros_like(acc_scr)

    a = jnp.maximum(_dot(h_scr[...], w1_ref[...]), 0.0)
    acc_scr[...] += _dot((a * a).astype(BF16), w2_ref[...])

    @pl.when(j == pl.num_programs(2) - 1)
    def _():
        out = x_scr[...] + gt2_ref[0] * acc_scr[...]
        o_ref[0] = out * lax.rsqrt(jnp.mean(out * out, axis=-1, keepdims=True) + EPS) * fg_ref[...]


def _gla_out_mlp(x, o_f, o_b, r, gla_norm_g, gate1, w_out, norm_g, scale, shift, gate2, w1, w2,
                 final_g):
    b, l, d = x.shape
    ff = w1.shape[1]
    tm = GLA_TM
    tf = MLP_TF
    tok = pl.BlockSpec((1, tm, d), lambda bi, i, j: (bi, i, 0))
    vec = pl.BlockSpec((1, d), lambda bi, i, j: (0, 0))
    mod = pl.BlockSpec((1, 1, d), lambda bi, i, j: (bi, 0, 0))
    return pl.pallas_call(
        _gla_out_mlp_kernel,
        grid=(b, l // tm, ff // tf),
        in_specs=[
            tok, tok, tok, tok, vec, mod, pl.BlockSpec((d, d), lambda bi, i, j: (0, 0)),
            vec, mod, mod, mod,
            pl.BlockSpec((d, tf), lambda bi, i, j: (0, j)),
            pl.BlockSpec((tf, d), lambda bi, i, j: (j, 0)),
            vec,
        ],
        out_specs=tok,
        out_shape=jax.ShapeDtypeStruct((b, l, d), F32),
        scratch_shapes=[pltpu.VMEM((tm, d), F32), pltpu.VMEM((tm, d), BF16),
                        pltpu.VMEM((tm, d), F32)],
        compiler_params=_cparams(("parallel", "parallel", "arbitrary")),
        name="gla_out_mlp",
    )(x, o_f, o_b, r, gla_norm_g.reshape(1, d), gate1, w_out, norm_g.reshape(1, d), scale, shift,
      gate2, w1, w2, final_g.reshape(1, d))


def _trunk(x, mod, wts):
    b = x.shape[0]

    def mods(layer):
        m = mod[layer].reshape(b, N_MOD, 1, D_MODEL)
        return [m[:, i] for i in range(N_MOD)]

    shift1, scale1, gate1, shift2, scale2, gate2 = mods(0)
    u = _s5_pre(x, wts["norm1_g"][0], scale1, shift1)
    y = _s5_core(u, wts["s5_w1"], wts["s5_w2"], wts["s5_sc"], b)
    x = _s5_post(x, y, gate1, wts["s5_w_glu"])
    x = _mlp(x, wts["norm2_g"][0], scale2, shift2, gate2, wts["mlp_w1"][0], wts["mlp_w2"][0])
    shift1, scale1, gate1, shift2, scale2, gate2 = mods(1)
    q, k, v, r, gf, gb = _gla_pre(x, wts["norm1_g"][1], scale1, shift1, wts["gla_w_in"],
                                  wts["gla_w_a1"], wts["gla_w_a2"], wts["gla_b_a"])
    o_f, o_b = _gla_core(q, k, v, gf, gb)
    return _gla_out_mlp(x, o_f, o_b, r, wts["gla_norm_g"], gate1, wts["gla_w_out"],
                        wts["norm2_g"][1], scale2, shift2, gate2, wts["mlp_w1"][1],
                        wts["mlp_w2"][1], wts["final_g"])


def kernel(x_prompt, x_sample, c_prompt, c_sample, ada_w, ada_b, norm1_g, norm2_g, s5_lam_re, s5_lam_im, s5_log_dt, s5_b_re, s5_b_im, s5_c_re, s5_c_im, s5_d, s5_w_glu, gla_w_in, gla_w_a1, gla_w_a2, gla_b_a, gla_norm_g, gla_w_out, mlp_w1, mlp_w2, final_g):
    bp, bs = c_prompt.shape[0], c_sample.shape[0]
    pad = (-(bp + bs)) % SUBLANES
    c_all = jnp.concatenate([c_prompt, c_sample, jnp.zeros((pad, D_MODEL), F32)], axis=0)
    mod_all = _modulation(c_all, ada_w, ada_b)

    s5_w1, s5_w2, s5_sc = _s5_build(s5_lam_re[0], s5_lam_im[0], s5_log_dt[0], s5_b_re[0],
                                    s5_b_im[0], s5_c_re[0], s5_c_im[0], s5_d[0])
    r = GLA_GATE_RANK
    w_a2bd = jnp.zeros((2 * r, 2 * GLA_DK), F32)
    w_a2bd = w_a2bd.at[:r, :GLA_DK].set(gla_w_a2[0, 0]).at[r:, GLA_DK:].set(gla_w_a2[0, 1])
    wts = {
        "norm1_g": norm1_g, "norm2_g": norm2_g, "final_g": final_g,
        "s5_w1": s5_w1, "s5_w2": s5_w2, "s5_sc": s5_sc,
        "s5_w_glu": s5_w_glu[0].astype(BF16),
        "mlp_w1": mlp_w1.astype(BF16), "mlp_w2": mlp_w2.astype(BF16),
        "gla_w_in": gla_w_in[0].astype(BF16),
        "gla_w_a1": jnp.concatenate([gla_w_a1[0, 0], gla_w_a1[0, 1]], axis=1).astype(BF16),
        "gla_w_a2": w_a2bd.astype(BF16),
        "gla_b_a": jnp.concatenate([gla_b_a[0, 0], gla_b_a[0, 1]], axis=0).reshape(1, 2 * GLA_DK),
        "gla_norm_g": gla_norm_g[0], "gla_w_out": gla_w_out[0].astype(BF16),
    }
    y_prompt = _trunk(x_prompt, mod_all[:, :bp], wts)
    y_sample = _trunk(x_sample, mod_all[:, bp:bp + bs], wts)
    return (y_prompt, y_sample)
```

```python
import functools

import jax
import jax.numpy as jnp
from jax import lax
from jax.experimental import pallas as pl
from jax.experimental.pallas import tpu as pltpu

F32 = jnp.float32
BF16 = jnp.bfloat16

D_MODEL = 1024
S5_GROUP = 16
S5_GROUPS = D_MODEL // S5_GROUP
S5_STATE = 64
GLA_HEADS = 4
GLA_DK = D_MODEL // 2
GLA_DV = D_MODEL
GLA_HEAD_K = GLA_DK // GLA_HEADS
GLA_HEAD_V = GLA_DV // GLA_HEADS
GLA_GATE_RANK = 16
GLA_GATE_TAU = 16.0
GLA_CHUNK = 64
GLA_BLOCK = 256
GLA_SEQS = 2
N_MOD = 6
EPS = 1e-6

LANES = 128
SUBLANES = 8
SEGS = LANES // S5_GROUP

S5_C = 32
S5_CW = S5_C * S5_GROUP
S5_NSTATE = 4 * S5_STATE
S5_CORE_GROUPS = 2
S5_LEVELS = 3
S5_TBK = 1024
S5_PITCH = S5_C + 4

MOD_TN = 1536
MLP_TM = 1024
MLP_TF = 2048
GLA_TM = 512

VMEM_LIMIT = 56 * 1024 * 1024


def _cparams(sem):
    return pltpu.CompilerParams(dimension_semantics=sem, vmem_limit_bytes=VMEM_LIMIT)


def _dot(a, b, precision=None):
    return jnp.dot(a, b, preferred_element_type=F32, precision=precision)


def _dot_nt(a, b, precision=None):
    return lax.dot_general(a, b, (((1,), (1,)), ((), ())),
                           preferred_element_type=F32, precision=precision)


def _dot_tn(a, b, precision=None):
    return lax.dot_general(a, b, (((0,), (0,)), ((), ())),
                           preferred_element_type=F32, precision=precision)


def _norm_mod(x, g, scale, shift):
    y = x * lax.rsqrt(jnp.mean(x * x, axis=-1, keepdims=True) + EPS)
    return y * (g * (1.0 + scale)) + shift


def _mod_kernel(c_ref, w_ref, b_ref, o_ref):
    c = c_ref[...]
    s = c * jax.nn.sigmoid(c)
    o_ref[0] = _dot(s.astype(BF16), w_ref[0].astype(BF16)) + b_ref[0]


def _modulation(c_all, ada_w, ada_b):
    depth, d, n = ada_w.shape
    rows = c_all.shape[0]
    tn = MOD_TN
    return pl.pallas_call(
        _mod_kernel,
        grid=(depth, n // tn),
        in_specs=[
            pl.BlockSpec((rows, d), lambda l, j: (0, 0)),
            pl.BlockSpec((1, d, tn), lambda l, j: (l, 0, j)),
            pl.BlockSpec((1, 1, tn), lambda l, j: (l, 0, j)),
        ],
        out_specs=pl.BlockSpec((1, rows, tn), lambda l, j: (l, 0, j)),
        out_shape=jax.ShapeDtypeStruct((depth, rows, n), F32),
        compiler_params=_cparams(("parallel", "parallel")),
        name="adaln_mod",
    )(c_all, ada_w, ada_b.reshape(depth, 1, n))


def _seg_exchange(lo, hi, d, seg):
    keep = (seg & d) == 0
    return (jnp.where(keep, lo, pltpu.roll(hi, S5_GROUP * d, 1)),
            jnp.where(keep, pltpu.roll(lo, LANES - S5_GROUP * d, 1), hi))


def _seg_transpose8(ngroups, load, store, buf_a, buf_b, seg):
    def put_a(n, i, v):
        buf_a[n, i] = v

    def put_b(n, i, v):
        buf_b[n, i] = v

    stages = ((4, load, put_a),
              (2, lambda n, i: buf_a[n, i], put_b),
              (1, lambda n, i: buf_b[n, i], store))
    for d, get, put in stages:
        for n in range(ngroups):
            for i in range(SEGS):
                if i & d == 0:
                    lo, hi = _seg_exchange(get(n, i), get(n, i + d), d, seg)
                    put(n, i, lo)
                    put(n, i + d, hi)


def _seg_buffers(rows, dtype):
    ngroups = D_MODEL // LANES * (S5_C // SEGS)
    return [pltpu.VMEM((ngroups, SEGS, rows, LANES), dtype) for _ in range(2)]


def _s5_pre_kernel(x_ref, g_ref, sc_ref, sh_ref, u_ref, h_scr, buf_a, buf_b):
    h = _norm_mod(x_ref[0], g_ref[...], sc_ref[0], sh_ref[0])
    nk = S5_TBK // S5_C
    nq = S5_C // SEGS
    for k in range(nk):
        for j in range(D_MODEL // LANES):
            h_scr[j, pl.ds(S5_PITCH * k, S5_C), :] = h[S5_C * k:S5_C * (k + 1), LANES * j:LANES * (j + 1)]
    seg = lax.broadcasted_iota(jnp.int32, (nk, LANES), 1) // S5_GROUP

    def load(n, i):
        return h_scr[n // nq, pl.ds(SEGS * (n % nq) + i, nk, stride=S5_PITCH), :]

    def store(n, gl, v):
        u_ref[SEGS * (n // nq) + gl, :, pl.ds(LANES * (n % nq), LANES)] = v.astype(BF16)

    _seg_transpose8(D_MODEL // LANES * nq, load, store, buf_a, buf_b, seg)


def _s5_pre(x, norm_g, scale, shift):
    b, l, d = x.shape
    nb = l // S5_TBK
    nk = S5_TBK // S5_C
    return pl.pallas_call(
        _s5_pre_kernel,
        grid=(b, nb),
        in_specs=[
            pl.BlockSpec((1, S5_TBK, d), lambda bi, i: (bi, i, 0)),
            pl.BlockSpec((1, d), lambda bi, i: (0, 0)),
            pl.BlockSpec((1, 1, d), lambda bi, i: (bi, 0, 0)),
            pl.BlockSpec((1, 1, d), lambda bi, i: (bi, 0, 0)),
        ],
        out_specs=pl.BlockSpec((S5_GROUPS, nk, S5_CW), lambda bi, i: (0, bi * nb + i, 0)),
        out_shape=jax.ShapeDtypeStruct((S5_GROUPS, b * l // S5_C, S5_CW), BF16),
        scratch_shapes=[pltpu.VMEM((d // LANES, nk * S5_PITCH, LANES), F32)] + _seg_buffers(nk, F32),
        compiler_params=_cparams(("parallel", "parallel")),
        name="s5_pre",
    )(x, norm_g.reshape(1, d), scale, shift)


def _s5_build_kernel(rows_ref, rowsfb_ref, bcat_ref, bswp_ref, ccat_ref, cswp_ref, dsk_ref,
                     w1_ref, w2_ref, sc_ref, ex_scr, mat_scr, kt_scr):
    c = S5_C
    cw = S5_CW
    hi = lax.Precision.HIGHEST
    nt = ((c + 1 + SUBLANES - 1) // SUBLANES) * SUBLANES
    lane1 = lax.broadcasted_iota(jnp.int32, (1, LANES), 1)
    sgn = jnp.where(lane1 < S5_STATE, -1.0, 1.0).astype(F32)
    nrow = lax.broadcasted_iota(jnp.int32, (nt, LANES), 0).astype(F32)

    for d in range(2):
        lr = rows_ref[0, d, 0:1, :]
        li = rows_ref[0, d, 1:2, :]
        dt = jnp.exp(rows_ref[0, d, 2:3, :])
        mag = jnp.exp(nrow * (lr * dt))
        ang = nrow * (li * dt)
        p_re = mag * jnp.cos(ang)
        p_im = mag * jnp.sin(ang)
        ab_re = p_re[1:2, :]
        ab_im = p_im[1:2, :]
        den = lr * lr + li * li
        z_re = ((ab_re - 1.0) * lr + ab_im * li) / den
        z_im = (ab_im * lr - (ab_re - 1.0) * li) / den
        bbar = z_re * bcat_ref[0, d] + (z_im * sgn) * bswp_ref[0, d]
        p_is = p_im * sgn
        for n in range(c + 1):
            blk = pl.ds(S5_GROUP * n, S5_GROUP)
            ex_scr[0, blk, :] = jnp.broadcast_to(p_re[n:n + 1, :], (S5_GROUP, LANES))
            ex_scr[1, blk, :] = jnp.broadcast_to(p_is[n:n + 1, :], (S5_GROUP, LANES))
            ex_scr[2, blk, :] = jnp.broadcast_to(p_re[c - n:c - n + 1, :], (S5_GROUP, LANES))
            ex_scr[3, blk, :] = jnp.broadcast_to(p_is[c - n:c - n + 1, :], (S5_GROUP, LANES))
        c_t = jnp.concatenate([ccat_ref[0, d]] * c, axis=0)
        c_s = jnp.concatenate([cswp_ref[0, d]] * c, axis=0)
        bswap = z_re * bswp_ref[0, d] - (z_im * sgn) * bcat_ref[0, d]
        b_t = jnp.concatenate([bbar] * c, axis=0)
        b_s = jnp.concatenate([bswap] * c, axis=0)
        lo = pl.ds(0, cw)
        up = pl.ds(S5_GROUP, cw)
        bneg = bbar * (-sgn)
        if d == 0:
            ct = ex_scr[0, lo, :] * c_t + ex_scr[1, lo, :] * c_s
            kt_scr[0] = _dot_nt(bneg, ct, hi)
            mat_scr[0] = ex_scr[2, up, :] * b_t + ex_scr[3, up, :] * b_s
            mat_scr[2] = ex_scr[0, up, :] * c_t + ex_scr[1, up, :] * c_s
        else:
            ct = ex_scr[2, up, :] * c_t + ex_scr[3, up, :] * c_s
            kt_scr[1] = _dot_nt(bneg, ct, hi)
            mat_scr[1] = ex_scr[0, lo, :] * b_t + ex_scr[1, lo, :] * b_s
            mat_scr[3] = ex_scr[2, lo, :] * c_t + ex_scr[3, lo, :] * c_s

    lane_cw = lax.broadcasted_iota(jnp.int32, (S5_GROUP, cw), 1)
    chan = lax.broadcasted_iota(jnp.int32, (S5_GROUP, cw), 0)
    dsk = dsk_ref[0]
    kf = kt_scr[0]
    kb = kt_scr[1]
    for s in range(c):
        rf = kf if s == 0 else pltpu.roll(kf, S5_GROUP * s, 1)
        rb = kb if s == c - 1 else pltpu.roll(kb, S5_GROUP * (s + 1), 1)
        blk = (jnp.where(lane_cw >= S5_GROUP * s, rf, 0.0)
               + jnp.where(lane_cw < S5_GROUP * (s + 1), rb, 0.0)
               + jnp.where(lane_cw == S5_GROUP * s + chan, dsk, 0.0))
        w1_ref[0, pl.ds(S5_GROUP * s, S5_GROUP), pl.ds(0, cw)] = blk.astype(BF16)

    lane_m = lax.broadcasted_iota(jnp.int32, (cw, LANES), 1) < S5_STATE
    bmf = mat_scr[0]
    bmb = mat_scr[1]
    w1_ref[0, :, pl.ds(cw, LANES)] = jnp.where(
        lane_m, bmf, pltpu.roll(bmb, S5_STATE, 1)).astype(BF16)
    w1_ref[0, :, pl.ds(cw + LANES, LANES)] = jnp.where(
        lane_m, pltpu.roll(bmf, S5_STATE, 1), bmb).astype(BF16)
    caf = mat_scr[2]
    cab = mat_scr[3]
    w2t_re = jnp.where(lane_m, caf, pltpu.roll(cab, S5_STATE, 1))
    w2t_im = -jnp.where(lane_m, pltpu.roll(caf, S5_STATE, 1), cab)
    w2_ref[0, pl.ds(0, LANES), :] = w2t_re.T.astype(BF16)
    w2_ref[0, pl.ds(LANES, LANES), :] = w2t_im.T.astype(BF16)

    lrfb = rowsfb_ref[0, 0:1, :]
    lifb = rowsfb_ref[0, 1:2, :]
    dtfb = jnp.exp(rowsfb_ref[0, 2:3, :])
    row8 = lax.broadcasted_iota(jnp.int32, (SUBLANES, LANES), 0)
    fwd8 = lax.broadcasted_iota(jnp.int32, (SUBLANES, LANES), 1) < S5_STATE
    mag_c = jnp.exp(float(c) * (lrfb * dtfb))
    ang_c = float(c) * (lifb * dtfb)
    base = (mag_c * jnp.cos(ang_c), mag_c * jnp.sin(ang_c))

    def cmul(a, b):
        return a[0] * b[0] - a[1] * b[1], a[0] * b[1] + a[1] * b[0]

    for lvl in range(S5_LEVELS):
        pw = [base]
        for _ in range(SUBLANES - 1):
            pw.append(cmul(pw[-1], base))
        for k, dd in enumerate((1, 2, 4)):
            ok = (fwd8 & (row8 >= dd)) | (jnp.logical_not(fwd8) & (row8 < SUBLANES - dd))
            sc_ref[0, lvl * 8 + 2 * k] = jnp.where(ok, pw[dd - 1][0], 0.0)
            sc_ref[0, lvl * 8 + 2 * k + 1] = jnp.where(ok, pw[dd - 1][1], 0.0)
        ap_re = jnp.zeros((SUBLANES, LANES), F32)
        ap_im = jnp.zeros((SUBLANES, LANES), F32)
        for n in range(1, SUBLANES + 1):
            here = (fwd8 & (row8 == n - 1)) | (jnp.logical_not(fwd8) & (row8 == SUBLANES - n))
            ap_re = jnp.where(here, pw[n - 1][0], ap_re)
            ap_im = jnp.where(here, pw[n - 1][1], ap_im)
        sc_ref[0, lvl * 8 + 6] = ap_re
        sc_ref[0, lvl * 8 + 7] = ap_im
        base = pw[SUBLANES - 1]


def _s5_build(lam_re, lam_im, log_dt, b_re, b_im, c_re, c_im, d_skip):
    g, p = S5_GROUPS, S5_STATE
    dsk = jnp.tile(d_skip.reshape(g, 1, S5_GROUP), (1, 1, S5_C))

    def dup(a):
        return jnp.concatenate([a, a], axis=-1).transpose(1, 0, 2)[:, :, None, :]

    ldt = jnp.broadcast_to(log_dt.T[:, :, None, None], (g, 2, 1, 2 * p))
    rows = jnp.concatenate(
        [dup(lam_re), dup(lam_im), ldt, jnp.zeros((g, 2, SUBLANES - 3, 2 * p), F32)], axis=2)

    def fb(a):
        return jnp.concatenate([a[0], a[1]], axis=-1)[:, None, :]

    ldt2 = jnp.broadcast_to(log_dt[:, :, None], (2, g, p))
    rowsfb = jnp.concatenate(
        [fb(lam_re), fb(lam_im), fb(ldt2), jnp.zeros((g, SUBLANES - 3, 2 * p), F32)], axis=1)
    brt = b_re.transpose(1, 0, 3, 2)
    bit = b_im.transpose(1, 0, 3, 2)
    crt = c_re.transpose(1, 0, 2, 3)
    cit = c_im.transpose(1, 0, 2, 3)
    bcat = jnp.concatenate([brt, bit], axis=-1)
    bswp = jnp.concatenate([bit, brt], axis=-1)
    ccat = jnp.concatenate([crt, cit], axis=-1)
    cswp = jnp.concatenate([cit, crt], axis=-1)
    nsc = S5_LEVELS * 8
    spec4 = pl.BlockSpec((1, 2, S5_GROUP, LANES), lambda i: (i, 0, 0, 0))
    return pl.pallas_call(
        _s5_build_kernel,
        grid=(g,),
        in_specs=[
            pl.BlockSpec((1, 2, SUBLANES, LANES), lambda i: (i, 0, 0, 0)),
            pl.BlockSpec((1, SUBLANES, LANES), lambda i: (i, 0, 0)),
            spec4, spec4, spec4, spec4,
            pl.BlockSpec((1, 1, S5_CW), lambda i: (i, 0, 0)),
        ],
        out_specs=[
            pl.BlockSpec((1, S5_CW, S5_CW + S5_NSTATE), lambda i: (i, 0, 0)),
            pl.BlockSpec((1, S5_NSTATE, S5_CW), lambda i: (i, 0, 0)),
            pl.BlockSpec((1, nsc, SUBLANES, LANES), lambda i: (i, 0, 0, 0)),
        ],
        out_shape=[
            jax.ShapeDtypeStruct((g, S5_CW, S5_CW + S5_NSTATE), BF16),
            jax.ShapeDtypeStruct((g, S5_NSTATE, S5_CW), BF16),
            jax.ShapeDtypeStruct((g, nsc, SUBLANES, LANES), F32),
        ],
        scratch_shapes=[
            pltpu.VMEM((4, (S5_C + 1) * S5_GROUP, LANES), F32),
            pltpu.VMEM((4, S5_CW, LANES), F32),
            pltpu.VMEM((2, S5_GROUP, S5_CW), F32),
        ],
        compiler_params=_cparams(("parallel",)),
        name="s5_build",
    )(rows, rowsfb, bcat, bswp, ccat, cswp, dsk)


def _tile_scan(re, im, sc_ref, lvl, fwd):
    for k, dd in enumerate((1, 2, 4)):
        mr = sc_ref[0, lvl * 8 + 2 * k]
        mi = sc_ref[0, lvl * 8 + 2 * k + 1]
        sre = jnp.where(fwd, pltpu.roll(re, dd, 0), pltpu.roll(re, SUBLANES - dd, 0))
        sim = jnp.where(fwd, pltpu.roll(im, dd, 0), pltpu.roll(im, SUBLANES - dd, 0))
        re, im = re + mr * sre - mi * sim, im + mr * sim + mi * sre
    return re, im


def _scan_rows(bufs, lvl, base, ntiles, sc_ref, p_ref, p_base):
    re_ref, im_ref = bufs[lvl]
    fwd = lax.broadcasted_iota(jnp.int32, (SUBLANES, LANES), 1) < S5_STATE
    row = lax.broadcasted_iota(jnp.int32, (SUBLANES, LANES), 0)
    fwd1 = fwd[0:1, :]

    def carried(hre, him, cre, cim):
        pre = jnp.where(fwd, jnp.where(row == 0, cre, pltpu.roll(hre, 1, 0)),
                        jnp.where(row == SUBLANES - 1, cre, pltpu.roll(hre, SUBLANES - 1, 0)))
        pim = jnp.where(fwd, jnp.where(row == 0, cim, pltpu.roll(him, 1, 0)),
                        jnp.where(row == SUBLANES - 1, cim, pltpu.roll(him, SUBLANES - 1, 0)))
        return pre, pim

    if ntiles == 1:
        sl = pl.ds(base, SUBLANES)
        hre, him = _tile_scan(re_ref[sl, :], im_ref[sl, :], sc_ref, lvl, fwd)
        re_ref[sl, :] = hre
        im_ref[sl, :] = him
        if lvl == 0:
            zero = jnp.zeros((1, LANES), F32)
            pre, pim = carried(hre, him, zero, zero)
            p_ref[pl.ds(p_base, SUBLANES), pl.ds(0, LANES)] = pre
            p_ref[pl.ds(p_base, SUBLANES), pl.ds(LANES, LANES)] = pim
        return

    ere_ref, eim_ref = bufs[lvl + 1]
    ntn = (ntiles + SUBLANES - 1) // SUBLANES
    ere_ref[...] = jnp.zeros_like(ere_ref)
    eim_ref[...] = jnp.zeros_like(eim_ref)

    for i in range(ntiles):
        sl = pl.ds(base + i * SUBLANES, SUBLANES)
        hre, him = _tile_scan(re_ref[sl, :], im_ref[sl, :], sc_ref, lvl, fwd)
        re_ref[sl, :] = hre
        im_ref[sl, :] = him
        ere_ref[pl.ds(SUBLANES + i, 1), :] = jnp.where(fwd1, hre[SUBLANES - 1:SUBLANES, :], hre[0:1, :])
        eim_ref[pl.ds(SUBLANES + i, 1), :] = jnp.where(fwd1, him[SUBLANES - 1:SUBLANES, :], him[0:1, :])

    _scan_rows(bufs, lvl + 1, SUBLANES, ntn, sc_ref, None, 0)
    apr = sc_ref[0, lvl * 8 + 6]
    api = sc_ref[0, lvl * 8 + 7]

    for i in range(ntiles):
        sl = pl.ds(base + i * SUBLANES, SUBLANES)
        cre = jnp.where(fwd1, ere_ref[pl.ds(SUBLANES - 1 + i, 1), :], ere_ref[pl.ds(SUBLANES + 1 + i, 1), :])
        cim = jnp.where(fwd1, eim_ref[pl.ds(SUBLANES - 1 + i, 1), :], eim_ref[pl.ds(SUBLANES + 1 + i, 1), :])
        hre = re_ref[sl, :] + apr * cre - api * cim
        him = im_ref[sl, :] + apr * cim + api * cre
        if lvl == 0:
            psl = pl.ds(p_base + i * SUBLANES, SUBLANES)
            pre, pim = carried(hre, him, cre, cim)
            p_ref[psl, pl.ds(0, LANES)] = pre
            p_ref[psl, pl.ds(LANES, LANES)] = pim
        else:
            re_ref[sl, :] = hre
            im_ref[sl, :] = him


def _s5_core_kernel(nseq, u_ref, w1_ref, w2_ref, sc_ref, y_ref,
                    s_re, s_im, e1_re, e1_im, e2_re, e2_im, p_scr, yi_scr):
    rows = u_ref.shape[1]
    nk = rows // nseq
    bufs = [(s_re, s_im), (e1_re, e1_im), (e2_re, e2_im)]
    for gi in range(S5_CORE_GROUPS):
        u = u_ref[gi]
        st = _dot(u, w1_ref[gi, :, pl.ds(S5_CW, S5_NSTATE)])
        s_re[...] = st[:, :LANES]
        s_im[...] = st[:, LANES:]
        yi_scr[...] = _dot(u, w1_ref[gi, :, pl.ds(0, S5_CW)])
        sc_g = sc_ref.at[pl.ds(gi, 1)]
        for b in range(nseq):
            _scan_rows(bufs, 0, b * nk, nk // SUBLANES, sc_g, p_scr, b * nk)
        y_ref[gi] = (yi_scr[...] + _dot(p_scr[...].astype(BF16), w2_ref[gi])).astype(BF16)


def _s5_core(u, w1, w2, sc, nseq):
    g, rows, cw = u.shape
    gs = S5_CORE_GROUPS
    nk = rows // nseq
    nt0 = nk // SUBLANES
    e1_rows = ((nt0 + SUBLANES - 1) // SUBLANES) * SUBLANES + 2 * SUBLANES
    nt1 = (nt0 + SUBLANES - 1) // SUBLANES
    e2_rows = ((nt1 + SUBLANES - 1) // SUBLANES) * SUBLANES + 2 * SUBLANES
    assert nt1 <= SUBLANES * SUBLANES, "sequence too long for S5_LEVELS scan levels"
    return pl.pallas_call(
        functools.partial(_s5_core_kernel, nseq),
        grid=(g // gs,),
        in_specs=[
            pl.BlockSpec((gs, rows, cw), lambda i: (i, 0, 0)),
            pl.BlockSpec((gs, cw, cw + S5_NSTATE), lambda i: (i, 0, 0)),
            pl.BlockSpec((gs, S5_NSTATE, cw), lambda i: (i, 0, 0)),
            pl.BlockSpec((gs, S5_LEVELS * 8, SUBLANES, LANES), lambda i: (i, 0, 0, 0)),
        ],
        out_specs=pl.BlockSpec((gs, rows, cw), lambda i: (i, 0, 0)),
        out_shape=jax.ShapeDtypeStruct((g, rows, cw), BF16),
        scratch_shapes=[
            pltpu.VMEM((rows, LANES), F32), pltpu.VMEM((rows, LANES), F32),
            pltpu.VMEM((e1_rows, LANES), F32), pltpu.VMEM((e1_rows, LANES), F32),
            pltpu.VMEM((e2_rows, LANES), F32), pltpu.VMEM((e2_rows, LANES), F32),
            pltpu.VMEM((rows, 2 * LANES), F32),
            pltpu.VMEM((rows, cw), F32),
        ],
        compiler_params=_cparams(("parallel",)),
        name="s5_core",
    )(u, w1, w2, sc)


def _s5_post_kernel(x_ref, y_ref, gt_ref, wg_ref, o_ref, y_scr, buf_a, buf_b):
    nk = S5_TBK // S5_C
    nq = S5_C // SEGS
    seg = lax.broadcasted_iota(jnp.int32, (nk, LANES), 1) // S5_GROUP

    def load(n, gl):
        return y_ref[SEGS * (n // nq) + gl, :, pl.ds(LANES * (n % nq), LANES)]

    def store(n, i, v):
        y_scr[n // nq, pl.ds(S5_PITCH * (SEGS * (n % nq) + i), nk), :] = v.astype(F32)

    _seg_transpose8(D_MODEL // LANES * nq, load, store, buf_a, buf_b, seg)
    y = jnp.concatenate(
        [jnp.concatenate([y_scr[j, pl.ds(k, S5_C, stride=S5_PITCH), :] for j in range(D_MODEL // LANES)],
                         axis=-1)
         for k in range(nk)], axis=0)
    z = _dot(jax.nn.gelu(y).astype(BF16), wg_ref[...])
    m = z[:, :D_MODEL] * jax.nn.sigmoid(z[:, D_MODEL:])
    o_ref[0] = x_ref[0] + gt_ref[0] * m


def _s5_post(x, y, gate, w_glu):
    b, l, d = x.shape
    nb = l // S5_TBK
    nk = S5_TBK // S5_C
    assert nk <= S5_PITCH
    mod = pl.BlockSpec((1, 1, d), lambda bi, i: (bi, 0, 0))
    return pl.pallas_call(
        _s5_post_kernel,
        grid=(b, nb),
        in_specs=[
            pl.BlockSpec((1, S5_TBK, d), lambda bi, i: (bi, i, 0)),
            pl.BlockSpec((S5_GROUPS, nk, S5_CW), lambda bi, i: (0, bi * nb + i, 0)),
            mod,
            pl.BlockSpec((d, 2 * d), lambda bi, i: (0, 0)),
        ],
        out_specs=pl.BlockSpec((1, S5_TBK, d), lambda bi, i: (bi, i, 0)),
        out_shape=jax.ShapeDtypeStruct((b, l, d), F32),
        scratch_shapes=[pltpu.VMEM((d // LANES, S5_C * S5_PITCH, LANES), F32)] + _seg_buffers(nk, BF16),
        compiler_params=_cparams(("parallel", "parallel")),
        name="s5_post",
    )(x, y, gate, w_glu)


def _mlp_kernel(x_ref, g_ref, sc_ref, sh_ref, gt_ref, w1_ref, w2_ref, o_ref, h_scr, acc_scr):
    j = pl.program_id(2)

    @pl.when(j == 0)
    def _():
        h_scr[...] = _norm_mod(x_ref[0], g_ref[...], sc_ref[0], sh_ref[0]).astype(BF16)
        acc_scr[...] = jnp.zeros_like(acc_scr)

    a = jnp.maximum(_dot(h_scr[...], w1_ref[...]), 0.0)
    acc_scr[...] += _dot((a * a).astype(BF16), w2_ref[...])

    @pl.when(j == pl.num_programs(2) - 1)
    def _():
        o_ref[0] = x_ref[0] + gt_ref[0] * acc_scr[...]


def _mlp(x, norm_g, scale, shift, gate, w1, w2):
    b, l, d = x.shape
    ff = w1.shape[1]
    tm = min(MLP_TM, l)
    tf = MLP_TF
    vec = pl.BlockSpec((1, d), lambda bi, i, j: (0, 0))
    mod = pl.BlockSpec((1, 1, d), lambda bi, i, j: (bi, 0, 0))
    return pl.pallas_call(
        _mlp_kernel,
        grid=(b, l // tm, ff // tf),
        in_specs=[
            pl.BlockSpec((1, tm, d), lambda bi, i, j: (bi, i, 0)),
            vec, mod, mod, mod,
            pl.BlockSpec((d, tf), lambda bi, i, j: (0, j)),
            pl.BlockSpec((tf, d), lambda bi, i, j: (j, 0)),
        ],
        out_specs=pl.BlockSpec((1, tm, d), lambda bi, i, j: (bi, i, 0)),
        out_shape=jax.ShapeDtypeStruct((b, l, d), F32),
        scratch_shapes=[pltpu.VMEM((tm, d), BF16), pltpu.VMEM((tm, d), F32)],
        compiler_params=_cparams(("parallel", "parallel", "arbitrary")),
        name="mlp",
    )(x, norm_g.reshape(1, d), scale, shift, gate, w1, w2)


def _gla_pre_kernel(x_ref, g_ref, sc_ref, sh_ref, win_ref, wa1_ref, wa2_ref, ba_ref,
                    q_ref, k_ref, v_ref, r_ref, gf_ref, gb_ref):
    h = _norm_mod(x_ref[0], g_ref[...], sc_ref[0], sh_ref[0]).astype(BF16)
    proj = _dot(h, win_ref[...])
    q_ref[0] = proj[:, :GLA_DK] * (GLA_HEAD_K ** -0.5)
    k_ref[0] = proj[:, GLA_DK:2 * GLA_DK]
    v_ref[0] = proj[:, 2 * GLA_DK:2 * GLA_DK + GLA_DV].astype(BF16)
    r_ref[0] = proj[:, 2 * GLA_DK + GLA_DV:].astype(BF16)
    a = _dot(h, wa1_ref[...]).astype(BF16)
    z = _dot(a, wa2_ref[...]) + ba_ref[...]
    lg = (jnp.minimum(z, 0.0) - jnp.log(1.0 + jnp.exp(-jnp.abs(z)))) * (1.0 / GLA_GATE_TAU)
    gf_ref[0] = lg[:, :GLA_DK]
    gb_ref[0] = lg[:, GLA_DK:]


def _gla_pre(x, norm_g, scale, shift, w_in, w_a1c, w_a2bd, b_ac):
    b, l, d = x.shape
    tm = GLA_TM
    vec =pl.BlockSpec((1, d), lambda bi, i: (0, 0))
    mod = pl.BlockSpec((1, 1, d), lambda bi, i: (bi, 0, 0))

    def full(a):
        return pl.BlockSpec(a.shape, lambda bi, i: (0,) * a.ndim)

    def tok(w):
        return pl.BlockSpec((1, tm, w), lambda bi, i: (bi, i, 0))

    return pl.pallas_call(
        _gla_pre_kernel,
        grid=(b, l // tm),
        in_specs=[tok(d), vec, mod, mod, full(w_in), full(w_a1c), full(w_a2bd), full(b_ac)],
        out_specs=[tok(GLA_DK), tok(GLA_DK), tok(GLA_DV), tok(GLA_DV), tok(GLA_DK), tok(GLA_DK)],
        out_shape=[
            jax.ShapeDtypeStruct((b, l, GLA_DK), F32),
            jax.ShapeDtypeStruct((b, l, GLA_DK), F32),
            jax.ShapeDtypeStruct((b, l, GLA_DV), BF16),
            jax.ShapeDtypeStruct((b, l, GLA_DV), BF16),
            jax.ShapeDtypeStruct((b, l, GLA_DK), F32),
            jax.ShapeDtypeStruct((b, l, GLA_DK), F32),
        ],
        compiler_params=_cparams(("parallel", "parallel")),
        name="gla_pre",
    )(x, norm_g.reshape(1, d), scale, shift, w_in, w_a1c, w_a2bd, b_ac)


def _gla_block(e, q_ref, k_ref, v_ref, g_ref, st_ref, bc_ref, o_ref, fwd):
    c = GLA_CHUNK
    nb = GLA_BLOCK
    n = nb // c
    row = lax.broadcasted_iota(jnp.int32, (nb, nb), 0)
    col = lax.broadcasted_iota(jnp.int32, (nb, nb), 1)
    causal = (row >= col) if fwd else (row <= col)
    cum_m = (causal & ((row // c) == (col // c))).astype(BF16)
    g = g_ref[e]
    g1 = g.astype(BF16)
    g2 = (g - g1.astype(F32)).astype(BF16)
    bc_ref[...] = _dot(cum_m, g1) + _dot(cum_m, g2)
    last = c - 1 if fwd else 0
    sub = [slice(c * j, c * (j + 1)) for j in range(n)]
    order = list(range(n)) if fwd else list(range(n - 1, -1, -1))
    pos = {j: a for a, j in enumerate(order)}
    for h in range(GLA_HEADS):
        kl = pl.ds(GLA_HEAD_K * h, GLA_HEAD_K)
        bcum = bc_ref[:, kl]
        blast = [bcum[c * j + last:c * j + last + 1, :] for j in range(n)]

        def total(select):
            terms = [blast[m] for m in range(n) if select(m)]
            return sum(terms[1:], terms[0]) if terms else None

        def between(j, i):
            return total(lambda m: pos[j] < pos[m] < pos[i])

        def scaled(x, e):
            return x if e is None else x * jnp.exp(e)

        qh = q_ref[e, :, kl]
        kh = k_ref[e, :, kl]
        qd = qh * jnp.exp(bcum)
        kd = kh * jnp.exp(-bcum)
        kt = jnp.concatenate([kd[sub[j], :] * jnp.exp(blast[j]) for j in range(n)], axis=0)
        qd_b = qd.astype(BF16)
        kd_b = kd.astype(BF16)
        kt_b = kt.astype(BF16)
        vh = v_ref[e, :, pl.ds(GLA_HEAD_V * h, GLA_HEAD_V)]
        dec = jnp.exp(total(lambda m: True))

        def before(j):
            return total(lambda m: pos[m] < pos[j])

        def after(j):
            return total(lambda m: pos[m] > pos[j])

        q_in = jnp.concatenate([scaled(qd[sub[j], :], before(j)) for j in range(n)], axis=0)
        k_out = jnp.concatenate([scaled(kt[sub[j], :], after(j)) for j in range(n)], axis=0)
        rows = []
        for i in range(n):
            keys = []
            for j in range(n):
                if j == i or pos[j] > pos[i]:
                    keys.append(kd_b[sub[j], :])
                elif between(j, i) is None:
                    keys.append(kt_b[sub[j], :])
                else:
                    keys.append(scaled(kt[sub[j], :], between(j, i)).astype(BF16))
            rows.append(_dot_nt(qd_b[sub[i], :], jnp.concatenate(keys, axis=0)))
        scores = jnp.where(causal, jnp.concatenate(rows, axis=0), 0.0)
        s_t = st_ref[e, h]
        o = _dot(scores.astype(BF16), vh) + _dot_nt(q_in.astype(BF16), s_t.astype(BF16))
        o_ref[e, :, pl.ds(GLA_HEAD_V * h, GLA_HEAD_V)] = o.astype(BF16)
        st_ref[e, h] = s_t * dec + _dot_tn(vh, k_out.astype(BF16))


def _gla_core_kernel(qf_ref, kf_ref, vf_ref, gf_ref, qb_ref, kb_ref, vb_ref, gb_ref,
                     of_ref, ob_ref, stf_scr, stb_scr, bcf_scr, bcb_scr):
    @pl.when(pl.program_id(1) == 0)
    def _():
        stf_scr[...] = jnp.zeros_like(stf_scr)
        stb_scr[...] = jnp.zeros_like(stb_scr)

    for e in range(GLA_SEQS):
        _gla_block(e, qf_ref, kf_ref, vf_ref, gf_ref, stf_scr, bcf_scr, of_ref, True)
        _gla_block(e, qb_ref, kb_ref, vb_ref, gb_ref, stb_scr, bcb_scr, ob_ref, False)


def _gla_core(q, k, v, gf, gb):
    b, l, _ = q.shape
    c = GLA_BLOCK
    n = l // c
    ns = GLA_SEQS
    assert b % ns == 0 and l % c == 0

    def fw(w):
        return pl.BlockSpec((ns, c, w), lambda bi, i: (bi, i, 0))

    def bw(w):
        return pl.BlockSpec((ns, c, w), lambda bi, i: (bi, n - 1 - i, 0))

    return pl.pallas_call(
        _gla_core_kernel,
        grid=(b // ns, n),
        in_specs=[fw(GLA_DK), fw(GLA_DK), fw(GLA_DV), fw(GLA_DK),
                  bw(GLA_DK), bw(GLA_DK), bw(GLA_DV), bw(GLA_DK)],
        out_specs=[fw(GLA_DV), bw(GLA_DV)],
        out_shape=[jax.ShapeDtypeStruct((b, l, GLA_DV), BF16),
                   jax.ShapeDtypeStruct((b, l, GLA_DV), BF16)],
        scratch_shapes=[pltpu.VMEM((ns, GLA_HEADS, GLA_HEAD_V, GLA_HEAD_K), F32),
                        pltpu.VMEM((ns, GLA_HEADS, GLA_HEAD_V, GLA_HEAD_K), F32),
                        pltpu.VMEM((c, GLA_DK), F32), pltpu.VMEM((c, GLA_DK), F32)],
        compiler_params=_cparams(("parallel", "arbitrary")),
        name="gla_core",
    )(q, k, v, gf, q, k, v, gb)


def _gla_out_mlp_kernel(x_ref, of_ref, ob_ref, r_ref, ng_ref, gt1_ref, wo_ref,
                        g_ref, sc_ref, sh_ref, gt2_ref, w1_ref, w2_ref, fg_ref, o_ref,
                        x_scr, h_scr, acc_scr):
    j = pl.program_id(2)

    @pl.when(j == 0)
    def _():
        o = of_ref[0].astype(F32) + ob_ref[0].astype(F32)
        parts = []
        for h in range(GLA_HEADS):
            oh = o[:, GLA_HEAD_V * h:GLA_HEAD_V * (h + 1)]
            parts.append(oh * lax.rsqrt(jnp.mean(oh * oh, axis=-1, keepdims=True) + EPS))
        on = jnp.concatenate(parts, axis=-1) * ng_ref[...]
        r = r_ref[0].astype(F32)
        gated = on * (r * jax.nn.sigmoid(r))
        x = x_ref[0] + gt1_ref[0] * _dot(gated.astype(BF16), wo_ref[...])
        x_scr[...] = x
        h_scr[...] = _norm_mod(x, g_ref[...], sc_ref[0], sh_ref[0]).astype(BF16)
        acc_scr[...] = jnp.zeros_like(acc_scr)

    a = jnp.maximum(_dot(h_scr[...], w1_ref[...]), 0.0)
    acc_scr[...] += _dot((a * a).astype(BF16), w2_ref[...])

    @pl.when(j == pl.num_programs(2) - 1)
    def _():
        out = x_scr[...] + gt2_ref[0] * acc_scr[...]
        o_ref[0] = out * lax.rsqrt(jnp.mean(out * out, axis=-1, keepdims=True) + EPS) * fg_ref[...]


def _gla_out_mlp(x, o_f, o_b, r, gla_norm_g, gate1, w_out, norm_g, scale, shift, gate2, w1, w2,
                 final_g):
    b, l, d = x.shape
    ff = w1.shape[1]
    tm = GLA_TM
    tf = MLP_TF
    tok = pl.BlockSpec((1, tm, d), lambda bi, i, j: (bi, i, 0))
    vec = pl.BlockSpec((1, d), lambda bi, i, j: (0, 0))
    mod = pl.BlockSpec((1, 1, d), lambda bi, i, j: (bi, 0, 0))
    return pl.pallas_call(
        _gla_out_mlp_kernel,
        grid=(b, l // tm, ff // tf),
        in_specs=[
            tok, tok, tok, tok, vec, mod, pl.BlockSpec((d, d), lambda bi, i, j: (0, 0)),
            vec, mod, mod, mod,
            pl.BlockSpec((d, tf), lambda bi, i, j: (0, j)),
            pl.BlockSpec((tf, d), lambda bi, i, j: (j, 0)),
            vec,
        ],
        out_specs=tok,
        out_shape=jax.ShapeDtypeStruct((b, l, d), F32),
        scratch_shapes=[pltpu.VMEM((tm, d), F32), pltpu.VMEM((tm, d), BF16),
                        pltpu.VMEM((tm, d), F32)],
        compiler_params=_cparams(("parallel", "parallel", "arbitrary")),
        name="gla_out_mlp",
    )(x, o_f, o_b, r, gla_norm_g.reshape(1, d), gate1, w_out, norm_g.reshape(1, d), scale, shift,
      gate2, w1, w2, final_g.reshape(1, d))


def _trunk(x, mod, wts):
    b = x.shape[0]

    def mods(layer):
        m = mod[layer].reshape(b, N_MOD, 1, D_MODEL)
        return [m[:, i] for i in range(N_MOD)]

    shift1, scale1, gate1, shift2, scale2, gate2 = mods(0)
    u = _s5_pre(x, wts["norm1_g"][0], scale1, shift1)
    y = _s5_core(u, wts["s5_w1"], wts["s5_w2"], wts["s5_sc"], b)
    x = _s5_post(x, y, gate1, wts["s5_w_glu"])
    x = _mlp(x, wts["norm2_g"][0], scale2, shift2, gate2, wts["mlp_w1"][0], wts["mlp_w2"][0])
    shift1, scale1, gate1, shift2, scale2, gate2 = mods(1)
    q, k, v, r, gf, gb = _gla_pre(x, wts["norm1_g"][1], scale1, shift1, wts["gla_w_in"],
                                  wts["gla_w_a1"], wts["gla_w_a2"], wts["gla_b_a"])
    o_f, o_b = _gla_core(q, k, v, gf, gb)
    return _gla_out_mlp(x, o_f, o_b, r, wts["gla_norm_g"], gate1, wts["gla_w_out"],
                        wts["norm2_g"][1], scale2, shift2, gate2, wts["mlp_w1"][1],
                        wts["mlp_w2"][1], wts["final_g"])


def kernel(x_prompt, x_sample, c_prompt, c_sample, ada_w, ada_b, norm1_g, norm2_g, s5_lam_re, s5_lam_im, s5_log_dt, s5_b_re, s5_b_im, s5_c_re, s5_c_im, s5_d, s5_w_glu, gla_w_in, gla_w_a1, gla_w_a2, gla_b_a, gla_norm_g, gla_w_out, mlp_w1, mlp_w2, final_g):
    bp, bs = c_prompt.shape[0], c_sample.shape[0]
    pad = (-(bp + bs)) % SUBLANES
    c_all = jnp.concatenate([c_prompt, c_sample, jnp.zeros((pad, D_MODEL), F32)], axis=0)
    mod_all = _modulation(c_all, ada_w, ada_b)

    s5_w1, s5_w2, s5_sc = _s5_build(s5_lam_re[0], s5_lam_im[0], s5_log_dt[0], s5_b_re[0],
                                    s5_b_im[0], s5_c_re[0], s5_c_im[0], s5_d[0])
    r = GLA_GATE_RANK
    w_a2bd = jnp.zeros((2 * r, 2 * GLA_DK), F32)
    w_a2bd = w_a2bd.at[:r, :GLA_DK].set(gla_w_a2[0, 0]).at[r:, GLA_DK:].set(gla_w_a2[0, 1])
    wts = {
        "norm1_g": norm1_g, "norm2_g": norm2_g, "final_g": final_g,
        "s5_w1": s5_w1, "s5_w2": s5_w2, "s5_sc": s5_sc,
        "s5_w_glu": s5_w_glu[0].astype(BF16),
        "mlp_w1": [w.astype(BF16) for w in mlp_w1], "mlp_w2": [w.astype(BF16) for w in mlp_w2],
        "gla_w_in": gla_w_in[0].astype(BF16),
        "gla_w_a1": jnp.concatenate([gla_w_a1[0, 0], gla_w_a1[0, 1]], axis=1).astype(BF16),
        "gla_w_a2": w_a2bd.astype(BF16),
        "gla_b_a": jnp.concatenate([gla_b_a[0, 0], gla_b_a[0, 1]], axis=0).reshape(1, 2 * GLA_DK),
        "gla_norm_g": gla_norm_g[0], "gla_w_out": gla_w_out[0].astype(BF16),
    }
    y_prompt = _trunk(x_prompt, mod_all[:, :bp], wts)
    y_sample = _trunk(x_sample, mod_all[:, bp:bp + bs], wts)
    return (y_prompt, y_sample)
```

```python
import functools

import jax
import jax.numpy as jnp
from jax import lax
from jax.experimental import pallas as pl
from jax.experimental.pallas import tpu as pltpu

F32 = jnp.float32
BF16 = jnp.bfloat16

D_MODEL = 1024
S5_GROUP = 16
S5_GROUPS = D_MODEL // S5_GROUP
S5_STATE = 64
GLA_HEADS = 4
GLA_DK = D_MODEL // 2
GLA_DV = D_MODEL
GLA_HEAD_K = GLA_DK // GLA_HEADS
GLA_HEAD_V = GLA_DV // GLA_HEADS
GLA_GATE_RANK = 16
GLA_GATE_TAU = 16.0
GLA_CHUNK = 64
GLA_BLOCK = 256
GLA_SEQS = 2
N_MOD = 6
EPS = 1e-6

LANES = 128
SUBLANES = 8
SEGS = LANES // S5_GROUP

S5_C = 32
S5_CW = S5_C * S5_GROUP
S5_NSTATE = 4 * S5_STATE
S5_CORE_GROUPS = 2
S5_LEVELS = 3
S5_TBK = 1024
S5_PITCH = S5_C + 4

MOD_TN = 1536
MLP_TM = 1024
MLP_TF = 2048
GLA_TM = 512

VMEM_LIMIT = 56 * 1024 * 1024


def _cparams(sem):
    return pltpu.CompilerParams(dimension_semantics=sem, vmem_limit_bytes=VMEM_LIMIT)


def _dot(a, b, precision=None):
    return jnp.dot(a, b, preferred_element_type=F32, precision=precision)


def _dot_nt(a, b, precision=None):
    return lax.dot_general(a, b, (((1,), (1,)), ((), ())),
                           preferred_element_type=F32, precision=precision)


def _dot_tn(a, b, precision=None):
    return lax.dot_general(a, b, (((0,), (0,)), ((), ())),
                           preferred_element_type=F32, precision=precision)


def _norm_mod(x, g, scale, shift):
    y = x * lax.rsqrt(jnp.mean(x * x, axis=-1, keepdims=True) + EPS)
    return y * (g * (1.0 + scale)) + shift


def _mod_kernel(c_ref, w_ref, b_ref, o_ref):
    c = c_ref[...]
    s = c * jax.nn.sigmoid(c)
    o_ref[0] = _dot(s.astype(BF16), w_ref[0].astype(BF16)) + b_ref[0]


def _modulation(c_all, ada_w, ada_b):
    depth, d, n = ada_w.shape
    rows = c_all.shape[0]
    tn = MOD_TN
    return pl.pallas_call(
        _mod_kernel,
        grid=(depth, n // tn),
        in_specs=[
            pl.BlockSpec((rows, d), lambda l, j: (0, 0)),
            pl.BlockSpec((1, d, tn), lambda l, j: (l, 0, j)),
            pl.BlockSpec((1, 1, tn), lambda l, j: (l, 0, j)),
        ],
        out_specs=pl.BlockSpec((1, rows, tn), lambda l, j: (l, 0, j)),
        out_shape=jax.ShapeDtypeStruct((depth, rows, n), F32),
        compiler_params=_cparams(("parallel", "parallel")),
        name="adaln_mod",
    )(c_all, ada_w, ada_b.reshape(depth, 1, n))


def _seg_exchange(lo, hi, d, seg):
    keep = (seg & d) == 0
    return (jnp.where(keep, lo, pltpu.roll(hi, S5_GROUP * d, 1)),
            jnp.where(keep, pltpu.roll(lo, LANES - S5_GROUP * d, 1), hi))


def _seg_transpose8(ngroups, load, store, buf_a, buf_b, seg):
    def put_a(n, i, v):
        buf_a[n, i] = v

    def put_b(n, i, v):
        buf_b[n, i] = v

    stages = ((4, load, put_a),
              (2, lambda n, i: buf_a[n, i], put_b),
              (1, lambda n, i: buf_b[n, i], store))
    for d, get, put in stages:
        for n in range(ngroups):
            for i in range(SEGS):
                if i & d == 0:
                    lo, hi = _seg_exchange(get(n, i), get(n, i + d), d, seg)
                    put(n, i, lo)
                    put(n, i + d, hi)


def _seg_buffers(rows, dtype):
    ngroups = D_MODEL // LANES * (S5_C // SEGS)
    return [pltpu.VMEM((ngroups, SEGS, rows, LANES), dtype) for _ in range(2)]


def _s5_pre_kernel(x_ref, g_ref, sc_ref, sh_ref, u_ref, h_scr, buf_a, buf_b):
    h = _norm_mod(x_ref[0], g_ref[...], sc_ref[0], sh_ref[0])
    nk = S5_TBK // S5_C
    nq = S5_C // SEGS
    for k in range(nk):
        for j in range(D_MODEL // LANES):
            h_scr[j, pl.ds(S5_PITCH * k, S5_C), :] = h[S5_C * k:S5_C * (k + 1), LANES * j:LANES * (j + 1)]
    seg = lax.broadcasted_iota(jnp.int32, (nk, LANES), 1) // S5_GROUP

    def load(n, i):
        return h_scr[n // nq, pl.ds(SEGS * (n % nq) + i, nk, stride=S5_PITCH), :]

    def store(n, gl, v):
        u_ref[SEGS * (n // nq) + gl, :, pl.ds(LANES * (n % nq), LANES)] = v.astype(BF16)

    _seg_transpose8(D_MODEL // LANES * nq, load, store, buf_a, buf_b, seg)


def _s5_pre(x, norm_g, scale, shift):
    b, l, d = x.shape
    nb = l // S5_TBK
    nk = S5_TBK // S5_C
    return pl.pallas_call(
        _s5_pre_kernel,
        grid=(b, nb),
        in_specs=[
            pl.BlockSpec((1, S5_TBK, d), lambda bi, i: (bi, i, 0)),
            pl.BlockSpec((1, d), lambda bi, i: (0, 0)),
            pl.BlockSpec((1, 1, d), lambda bi, i: (bi, 0, 0)),
            pl.BlockSpec((1, 1, d), lambda bi, i: (bi, 0, 0)),
        ],
        out_specs=pl.BlockSpec((S5_GROUPS, nk, S5_CW), lambda bi, i: (0, bi * nb + i, 0)),
        out_shape=jax.ShapeDtypeStruct((S5_GROUPS, b * l // S5_C, S5_CW), BF16),
        scratch_shapes=[pltpu.VMEM((d // LANES, nk * S5_PITCH, LANES), F32)] + _seg_buffers(nk, F32),
        compiler_params=_cparams(("parallel", "parallel")),
        name="s5_pre",
    )(x, norm_g.reshape(1, d), scale, shift)


def _s5_build_kernel(rows_ref, rowsfb_ref, bcat_ref, bswp_ref, ccat_ref, cswp_ref, dsk_ref,
                     w1_ref, w2_ref, sc_ref, ex_scr, mat_scr, kt_scr):
    c = S5_C
    cw = S5_CW
    hi = lax.Precision.HIGHEST
    nt = ((c + 1 + SUBLANES - 1) // SUBLANES) * SUBLANES
    lane1 = lax.broadcasted_iota(jnp.int32, (1, LANES), 1)
    sgn = jnp.where(lane1 < S5_STATE, -1.0, 1.0).astype(F32)
    nrow = lax.broadcasted_iota(jnp.int32, (nt, LANES), 0).astype(F32)

    for d in range(2):
        lr = rows_ref[0, d, 0:1, :]
        li = rows_ref[0, d, 1:2, :]
        dt = jnp.exp(rows_ref[0, d, 2:3, :])
        mag = jnp.exp(nrow * (lr * dt))
        ang = nrow * (li * dt)
        p_re = mag * jnp.cos(ang)
        p_im = mag * jnp.sin(ang)
        ab_re = p_re[1:2, :]
        ab_im = p_im[1:2, :]
        den = lr * lr + li * li
        z_re = ((ab_re - 1.0) * lr + ab_im * li) / den
        z_im = (ab_im * lr - (ab_re - 1.0) * li) / den
        bbar = z_re * bcat_ref[0, d] + (z_im * sgn) * bswp_ref[0, d]
        p_is = p_im * sgn
        for n in range(c + 1):
            blk = pl.ds(S5_GROUP * n, S5_GROUP)
            ex_scr[0, blk, :] = jnp.broadcast_to(p_re[n:n + 1, :], (S5_GROUP, LANES))
            ex_scr[1, blk, :] = jnp.broadcast_to(p_is[n:n + 1, :], (S5_GROUP, LANES))
            ex_scr[2, blk, :] = jnp.broadcast_to(p_re[c - n:c - n + 1, :], (S5_GROUP, LANES))
            ex_scr[3, blk, :] = jnp.broadcast_to(p_is[c - n:c - n + 1, :], (S5_GROUP, LANES))
        c_t = jnp.concatenate([ccat_ref[0, d]] * c, axis=0)
        c_s = jnp.concatenate([cswp_ref[0, d]] * c, axis=0)
        bswap = z_re * bswp_ref[0, d] - (z_im * sgn) * bcat_ref[0, d]
        b_t = jnp.concatenate([bbar] * c, axis=0)
        b_s = jnp.concatenate([bswap] * c, axis=0)
        lo = pl.ds(0, cw)
        up = pl.ds(S5_GROUP, cw)
        bneg = bbar * (-sgn)
        if d == 0:
            ct = ex_scr[0, lo, :] * c_t + ex_scr[1, lo, :] * c_s
            kt_scr[0] = _dot_nt(bneg, ct, hi)
            mat_scr[0] = ex_scr[2, up, :] * b_t + ex_scr[3, up, :] * b_s
            mat_scr[2] = ex_scr[0, up, :] * c_t + ex_scr[1, up, :] * c_s
        else:
            ct = ex_scr[2, up, :] * c_t + ex_scr[3, up, :] * c_s
            kt_scr[1] = _dot_nt(bneg, ct, hi)
            mat_scr[1] = ex_scr[0, lo, :] * b_t + ex_scr[1, lo, :] * b_s
            mat_scr[3] = ex_scr[2, lo, :] * c_t + ex_scr[3, lo, :] * c_s

    lane_cw = lax.broadcasted_iota(jnp.int32, (S5_GROUP, cw), 1)
    chan = lax.broadcasted_iota(jnp.int32, (S5_GROUP, cw), 0)
    dsk = dsk_ref[0]
    kf = kt_scr[0]
    kb = kt_scr[1]
    for s in range(c):
        rf = kf if s == 0 else pltpu.roll(kf, S5_GROUP * s, 1)
        rb = kb if s == c - 1 else pltpu.roll(kb, S5_GROUP * (s + 1), 1)
        blk = (jnp.where(lane_cw >= S5_GROUP * s, rf, 0.0)
               + jnp.where(lane_cw < S5_GROUP * (s + 1), rb, 0.0)
               + jnp.where(lane_cw == S5_GROUP * s + chan, dsk, 0.0))
        w1_ref[0, pl.ds(S5_GROUP * s, S5_GROUP), pl.ds(0, cw)] = blk.astype(BF16)

    lane_m = lax.broadcasted_iota(jnp.int32, (cw, LANES), 1) < S5_STATE
    bmf = mat_scr[0]
    bmb = mat_scr[1]
    w1_ref[0, :, pl.ds(cw, LANES)] = jnp.where(
        lane_m, bmf, pltpu.roll(bmb, S5_STATE, 1)).astype(BF16)
    w1_ref[0, :, pl.ds(cw + LANES, LANES)] = jnp.where(
        lane_m, pltpu.roll(bmf, S5_STATE, 1), bmb).astype(BF16)
    caf = mat_scr[2]
    cab = mat_scr[3]
    w2t_re = jnp.where(lane_m, caf, pltpu.roll(cab, S5_STATE, 1))
    w2t_im = -jnp.where(lane_m, pltpu.roll(caf, S5_STATE, 1), cab)
    w2_ref[0, pl.ds(0, LANES), :] = w2t_re.T.astype(BF16)
    w2_ref[0, pl.ds(LANES, LANES), :] = w2t_im.T.astype(BF16)

    lrfb = rowsfb_ref[0, 0:1, :]
    lifb = rowsfb_ref[0, 1:2, :]
    dtfb = jnp.exp(rowsfb_ref[0, 2:3, :])
    row8 = lax.broadcasted_iota(jnp.int32, (SUBLANES, LANES), 0)
    fwd8 = lax.broadcasted_iota(jnp.int32, (SUBLANES, LANES), 1) < S5_STATE
    mag_c = jnp.exp(float(c) * (lrfb * dtfb))
    ang_c = float(c) * (lifb * dtfb)
    base = (mag_c * jnp.cos(ang_c), mag_c * jnp.sin(ang_c))

    def cmul(a, b):
        return a[0] * b[0] - a[1] * b[1], a[0] * b[1] + a[1] * b[0]

    for lvl in range(S5_LEVELS):
        pw = [base]
        for _ in range(SUBLANES - 1):
            pw.append(cmul(pw[-1], base))
        for k, dd in enumerate((1, 2, 4)):
            ok = (fwd8 & (row8 >= dd)) | (jnp.logical_not(fwd8) & (row8 < SUBLANES - dd))
            sc_ref[0, lvl * 8 + 2 * k] = jnp.where(ok, pw[dd - 1][0], 0.0)
            sc_ref[0, lvl * 8 + 2 * k + 1] = jnp.where(ok, pw[dd - 1][1], 0.0)
        ap_re = jnp.zeros((SUBLANES, LANES), F32)
        ap_im = jnp.zeros((SUBLANES, LANES), F32)
        for n in range(1, SUBLANES + 1):
            here = (fwd8 & (row8 == n - 1)) | (jnp.logical_not(fwd8) & (row8 == SUBLANES - n))
            ap_re = jnp.where(here, pw[n - 1][0], ap_re)
            ap_im = jnp.where(here, pw[n - 1][1], ap_im)
        sc_ref[0, lvl * 8 + 6] = ap_re
        sc_ref[0, lvl * 8 + 7] = ap_im
        base = pw[SUBLANES - 1]


def _s5_build(lam_re, lam_im, log_dt, b_re, b_im, c_re, c_im, d_skip):
    g, p = S5_GROUPS, S5_STATE
    dsk = jnp.tile(d_skip.reshape(g, 1, S5_GROUP), (1, 1, S5_C))

    def dup(a):
        return jnp.concatenate([a, a], axis=-1).transpose(1, 0, 2)[:, :, None, :]

    ldt = jnp.broadcast_to(log_dt.T[:, :, None, None], (g, 2, 1, 2 * p))
    rows = jnp.concatenate(
        [dup(lam_re), dup(lam_im), ldt, jnp.zeros((g, 2, SUBLANES - 3, 2 * p), F32)], axis=2)

    def fb(a):
        return jnp.concatenate([a[0], a[1]], axis=-1)[:, None, :]

    ldt2 = jnp.broadcast_to(log_dt[:, :, None], (2, g, p))
    rowsfb = jnp.concatenate(
        [fb(lam_re), fb(lam_im), fb(ldt2), jnp.zeros((g, SUBLANES - 3, 2 * p), F32)], axis=1)
    brt = b_re.transpose(1, 0, 3, 2)
    bit = b_im.transpose(1, 0, 3, 2)
    crt = c_re.transpose(1, 0, 2, 3)
    cit = c_im.transpose(1, 0, 2, 3)
    bcat = jnp.concatenate([brt, bit], axis=-1)
    bswp = jnp.concatenate([bit, brt], axis=-1)
    ccat = jnp.concatenate([crt, cit], axis=-1)
    cswp = jnp.concatenate([cit, crt], axis=-1)
    nsc = S5_LEVELS * 8
    spec4 = pl.BlockSpec((1, 2, S5_GROUP, LANES), lambda i: (i, 0, 0, 0))
    return pl.pallas_call(
        _s5_build_kernel,
        grid=(g,),
        in_specs=[
            pl.BlockSpec((1, 2, SUBLANES, LANES), lambda i: (i, 0, 0, 0)),
            pl.BlockSpec((1, SUBLANES, LANES), lambda i: (i, 0, 0)),
            spec4, spec4, spec4, spec4,
            pl.BlockSpec((1, 1, S5_CW), lambda i: (i, 0, 0)),
        ],
        out_specs=[
            pl.BlockSpec((1, S5_CW, S5_CW + S5_NSTATE), lambda i: (i, 0, 0)),
            pl.BlockSpec((1, S5_NSTATE, S5_CW), lambda i: (i, 0, 0)),
            pl.BlockSpec((1, nsc, SUBLANES, LANES), lambda i: (i, 0, 0, 0)),
        ],
        out_shape=[
            jax.ShapeDtypeStruct((g, S5_CW, S5_CW + S5_NSTATE), BF16),
            jax.ShapeDtypeStruct((g, S5_NSTATE, S5_CW), BF16),
            jax.ShapeDtypeStruct((g, nsc, SUBLANES, LANES), F32),
        ],
        scratch_shapes=[
            pltpu.VMEM((4, (S5_C + 1) * S5_GROUP, LANES), F32),
            pltpu.VMEM((4, S5_CW, LANES), F32),
            pltpu.VMEM((2, S5_GROUP, S5_CW), F32),
        ],
        compiler_params=_cparams(("parallel",)),
        name="s5_build",
    )(rows, rowsfb, bcat, bswp, ccat, cswp, dsk)


def _tile_scan(re, im, sc_ref, lvl, fwd):
    for k, dd in enumerate((1, 2, 4)):
        mr = sc_ref[0, lvl * 8 + 2 * k]
        mi = sc_ref[0, lvl * 8 + 2 * k + 1]
        sre = jnp.where(fwd, pltpu.roll(re, dd, 0), pltpu.roll(re, SUBLANES - dd, 0))
        sim = jnp.where(fwd, pltpu.roll(im, dd, 0), pltpu.roll(im, SUBLANES - dd, 0))
        re, im = re + mr * sre - mi * sim, im + mr * sim + mi * sre
    return re, im


def _scan_rows(bufs, lvl, base, ntiles, sc_ref, p_ref, p_base):
    re_ref, im_ref = bufs[lvl]
    fwd = lax.broadcasted_iota(jnp.int32, (SUBLANES, LANES), 1) < S5_STATE
    row = lax.broadcasted_iota(jnp.int32, (SUBLANES, LANES), 0)
    fwd1 = fwd[0:1, :]

    def carried(hre, him, cre, cim):
        pre = jnp.where(fwd, jnp.where(row == 0, cre, pltpu.roll(hre, 1, 0)),
                        jnp.where(row == SUBLANES - 1, cre, pltpu.roll(hre, SUBLANES - 1, 0)))
        pim = jnp.where(fwd, jnp.where(row == 0, cim, pltpu.roll(him, 1, 0)),
                        jnp.where(row == SUBLANES - 1, cim, pltpu.roll(him, SUBLANES - 1, 0)))
        return pre, pim

    if ntiles == 1:
        sl = pl.ds(base, SUBLANES)
        hre, him = _tile_scan(re_ref[sl, :], im_ref[sl, :], sc_ref, lvl, fwd)
        re_ref[sl, :] = hre
        im_ref[sl, :] = him
        if lvl == 0:
            zero = jnp.zeros((1, LANES), F32)
            pre, pim = carried(hre, him, zero, zero)
            p_ref[pl.ds(p_base, SUBLANES), pl.ds(0, LANES)] = pre
            p_ref[pl.ds(p_base, SUBLANES), pl.ds(LANES, LANES)] = pim
        return

    ere_ref, eim_ref = bufs[lvl + 1]
    ntn = (ntiles + SUBLANES - 1) // SUBLANES
    ere_ref[...] = jnp.zeros_like(ere_ref)
    eim_ref[...] = jnp.zeros_like(eim_ref)

    for i in range(ntiles):
        sl = pl.ds(base + i * SUBLANES, SUBLANES)
        hre, him = _tile_scan(re_ref[sl, :], im_ref[sl, :], sc_ref, lvl, fwd)
        re_ref[sl, :] = hre
        im_ref[sl, :] = him
        ere_ref[pl.ds(SUBLANES + i, 1), :] = jnp.where(fwd1, hre[SUBLANES - 1:SUBLANES, :], hre[0:1, :])
        eim_ref[pl.ds(SUBLANES + i, 1), :] = jnp.where(fwd1, him[SUBLANES - 1:SUBLANES, :], him[0:1, :])

    _scan_rows(bufs, lvl + 1, SUBLANES, ntn, sc_ref, None, 0)
    apr = sc_ref[0, lvl * 8 + 6]
    api = sc_ref[0, lvl * 8 + 7]

    for i in range(ntiles):
        sl = pl.ds(base + i * SUBLANES, SUBLANES)
        cre = jnp.where(fwd1, ere_ref[pl.ds(SUBLANES - 1 + i, 1), :], ere_ref[pl.ds(SUBLANES + 1 + i, 1), :])
        cim = jnp.where(fwd1, eim_ref[pl.ds(SUBLANES - 1 + i, 1), :], eim_ref[pl.ds(SUBLANES + 1 + i, 1), :])
        hre = re_ref[sl, :] + apr * cre - api * cim
        him = im_ref[sl, :] + apr * cim + api * cre
        if lvl == 0:
            psl = pl.ds(p_base + i * SUBLANES, SUBLANES)
            pre, pim = carried(hre, him, cre, cim)
            p_ref[psl, pl.ds(0, LANES)] = pre
            p_ref[psl, pl.ds(LANES, LANES)] = pim
        else:
            re_ref[sl, :] = hre
            im_ref[sl, :] = him


def _s5_core_kernel(nseq, u_ref, w1_ref, w2_ref, sc_ref, y_ref,
                    s_re, s_im, e1_re, e1_im, e2_re, e2_im, p_scr, yi_scr):
    rows = u_ref.shape[1]
    nk = rows // nseq
    bufs = [(s_re, s_im), (e1_re, e1_im), (e2_re, e2_im)]
    for gi in range(S5_CORE_GROUPS):
        u = u_ref[gi]
        st = _dot(u, w1_ref[gi, :, pl.ds(S5_CW, S5_NSTATE)])
        s_re[...] = st[:, :LANES]
        s_im[...] = st[:, LANES:]
        yi_scr[...] = _dot(u, w1_ref[gi, :, pl.ds(0, S5_CW)])
        sc_g = sc_ref.at[pl.ds(gi, 1)]
        for b in range(nseq):
            _scan_rows(bufs, 0, b * nk, nk // SUBLANES, sc_g, p_scr, b * nk)
        y_ref[gi] = (yi_scr[...] + _dot(p_scr[...].astype(BF16), w2_ref[gi])).astype(BF16)


def _s5_core(u, w1, w2, sc, nseq):
    g, rows, cw = u.shape
    gs = S5_CORE_GROUPS
    nk = rows // nseq
    nt0 = nk // SUBLANES
    e1_rows = ((nt0 + SUBLANES - 1) // SUBLANES) * SUBLANES + 2 * SUBLANES
    nt1 = (nt0 + SUBLANES - 1) // SUBLANES
    e2_rows = ((nt1 + SUBLANES - 1) // SUBLANES) * SUBLANES + 2 * SUBLANES
    assert nt1 <= SUBLANES * SUBLANES, "sequence too long for S5_LEVELS scan levels"
    return pl.pallas_call(
        functools.partial(_s5_core_kernel, nseq),
        grid=(g // gs,),
        in_specs=[
            pl.BlockSpec((gs, rows, cw), lambda i: (i, 0, 0)),
            pl.BlockSpec((gs, cw, cw + S5_NSTATE), lambda i: (i, 0, 0)),
            pl.BlockSpec((gs, S5_NSTATE, cw), lambda i: (i, 0, 0)),
            pl.BlockSpec((gs, S5_LEVELS * 8, SUBLANES, LANES), lambda i: (i, 0, 0, 0)),
        ],
        out_specs=pl.BlockSpec((gs, rows, cw), lambda i: (i, 0, 0)),
        out_shape=jax.ShapeDtypeStruct((g, rows, cw), BF16),
        scratch_shapes=[
            pltpu.VMEM((rows, LANES), F32), pltpu.VMEM((rows, LANES), F32),
            pltpu.VMEM((e1_rows, LANES), F32), pltpu.VMEM((e1_rows, LANES), F32),
            pltpu.VMEM((e2_rows, LANES), F32), pltpu.VMEM((e2_rows, LANES), F32),
            pltpu.VMEM((rows, 2 * LANES), F32),
            pltpu.VMEM((rows, cw), F32),
        ],
        compiler_params=_cparams(("parallel",)),
        name="s5_core",
    )(u, w1, w2, sc)


def _s5_post_kernel(x_ref, y_ref, gt_ref, wg_ref, o_ref, y_scr, buf_a, buf_b):
    nk = S5_TBK // S5_C
    nq = S5_C // SEGS
    seg = lax.broadcasted_iota(jnp.int32, (nk, LANES), 1) // S5_GROUP

    def load(n, gl):
        return y_ref[SEGS * (n // nq) + gl, :, pl.ds(LANES * (n % nq), LANES)]

    def store(n, i, v):
        y_scr[n // nq, pl.ds(SEGS * (n % nq) + i, nk, stride=S5_PITCH), :] = v.astype(F32)

    _seg_transpose8(D_MODEL // LANES * nq, load, store, buf_a, buf_b, seg)
    y = jnp.concatenate(
        [jnp.concatenate([y_scr[j, pl.ds(S5_PITCH * k, S5_C), :] for j in range(D_MODEL // LANES)], axis=-1)
         for k in range(nk)], axis=0)
    z = _dot(jax.nn.gelu(y).astype(BF16), wg_ref[...])
    m = z[:, :D_MODEL] * jax.nn.sigmoid(z[:, D_MODEL:])
    o_ref[0] = x_ref[0] + gt_ref[0] * m


def _s5_post(x, y, gate, w_glu):
    b, l, d = x.shape
    nb = l // S5_TBK
    nk = S5_TBK // S5_C
    mod = pl.BlockSpec((1, 1, d), lambda bi, i: (bi, 0, 0))
    return pl.pallas_call(
        _s5_post_kernel,
        grid=(b, nb),
        in_specs=[
            pl.BlockSpec((1, S5_TBK, d), lambda bi, i: (bi, i, 0)),
            pl.BlockSpec((S5_GROUPS, nk, S5_CW), lambda bi, i: (0, bi * nb + i, 0)),
            mod,
            pl.BlockSpec((d, 2 * d), lambda bi, i: (0, 0)),
        ],
        out_specs=pl.BlockSpec((1, S5_TBK, d), lambda bi, i: (bi, i, 0)),
        out_shape=jax.ShapeDtypeStruct((b, l, d), F32),
        scratch_shapes=[pltpu.VMEM((d // LANES, nk * S5_PITCH, LANES), F32)] + _seg_buffers(nk, BF16),
        compiler_params=_cparams(("parallel", "parallel")),
        name="s5_post",
    )(x, y, gate, w_glu)


def _mlp_kernel(x_ref, g_ref, sc_ref, sh_ref, gt_ref, w1_ref, w2_ref, o_ref, h_scr, acc_scr):
    j = pl.program_id(2)

    @pl.when(j == 0)
    def _():
        h_scr[...] = _norm_mod(x_ref[0], g_ref[...], sc_ref[0], sh_ref[0]).astype(BF16)
        acc_scr[...] = jnp.zeros_like(acc_scr)

    a = jnp.maximum(_dot(h_scr[...], w1_ref[...]), 0.0)
    acc_scr[...] += _dot((a * a).astype(BF16), w2_ref[...])

    @pl.when(j == pl.num_programs(2) - 1)
    def _():
        o_ref[0] = x_ref[0] + gt_ref[0] * acc_scr[...]


def _mlp(x, norm_g, scale, shift, gate, w1, w2):
    b, l, d = x.shape
    ff = w1.shape[1]
    tm = min(MLP_TM, l)
    tf = MLP_TF
    vec = pl.BlockSpec((1, d), lambda bi, i, j: (0, 0))
    mod = pl.BlockSpec((1, 1, d), lambda bi, i, j: (bi, 0, 0))
    return pl.pallas_call(
        _mlp_kernel,
        grid=(b, l // tm, ff // tf),
        in_specs=[
            pl.BlockSpec((1, tm, d), lambda bi, i, j: (bi, i, 0)),
            vec, mod, mod, mod,
            pl.BlockSpec((d, tf), lambda bi, i, j: (0, j)),
            pl.BlockSpec((tf, d), lambda bi, i, j: (j, 0)),
        ],
        out_specs=pl.BlockSpec((1, tm, d), lambda bi, i, j: (bi, i, 0)),
        out_shape=jax.ShapeDtypeStruct((b, l, d), F32),
        scratch_shapes=[pltpu.VMEM((tm, d), BF16), pltpu.VMEM((tm, d), F32)],
        compiler_params=_cparams(("parallel", "parallel", "arbitrary")),
        name="mlp",
    )(x, norm_g.reshape(1, d), scale, shift, gate, w1, w2)


def _gla_pre_kernel(x_ref, g_ref, sc_ref, sh_ref, win_ref, wa1_ref, wa2_ref, ba_ref,
                    q_ref, k_ref, v_ref, r_ref, gf_ref, gb_ref):
    h = _norm_mod(x_ref[0], g_ref[...], sc_ref[0], sh_ref[0]).astype(BF16)
    proj = _dot(h, win_ref[...])
    q_ref[0] = (proj[:, :GLA_DK] * (GLA_HEAD_K ** -0.5)).astype(BF16)
    k_ref[0] = proj[:, GLA_DK:2 * GLA_DK].astype(BF16)
    v_ref[0] = proj[:, 2 * GLA_DK:2 * GLA_DK + GLA_DV].astype(BF16)
    r_ref[0] = proj[:, 2 * GLA_DK + GLA_DV:].astype(BF16)
    a = _dot(h, wa1_ref[...]).astype(BF16)
    z = _dot(a, wa2_ref[...]) + ba_ref[...]
    lg = (jnp.minimum(z, 0.0) - jnp.log(1.0 + jnp.exp(-jnp.abs(z)))) * (1.0 / GLA_GATE_TAU)
    gf_ref[0] = lg[:, :GLA_DK]
    gb_ref[0] = lg[:, GLA_DK:]


def _gla_pre(x, norm_g, scale, shift, w_in, w_a1c, w_a2bd, b_ac):
    b, l, d = x.shape
    tm = GLA_TM
    vec =pl.BlockSpec((1, d), lambda bi, i: (0, 0))
    mod = pl.BlockSpec((1, 1, d), lambda bi, i: (bi, 0, 0))

    def full(a):
        return pl.BlockSpec(a.shape, lambda bi, i: (0,) * a.ndim)

    def tok(w):
        return pl.BlockSpec((1, tm, w), lambda bi, i: (bi, i, 0))

    return pl.pallas_call(
        _gla_pre_kernel,
        grid=(b, l // tm),
        in_specs=[tok(d), vec, mod, mod, full(w_in), full(w_a1c), full(w_a2bd), full(b_ac)],
        out_specs=[tok(GLA_DK), tok(GLA_DK), tok(GLA_DV), tok(GLA_DV), tok(GLA_DK), tok(GLA_DK)],
        out_shape=[
            jax.ShapeDtypeStruct((b, l, GLA_DK), BF16),
            jax.ShapeDtypeStruct((b, l, GLA_DK), BF16),
            jax.ShapeDtypeStruct((b, l, GLA_DV), BF16),
            jax.ShapeDtypeStruct((b, l, GLA_DV), BF16),
            jax.ShapeDtypeStruct((b, l, GLA_DK), F32),
            jax.ShapeDtypeStruct((b, l, GLA_DK), F32),
        ],
        compiler_params=_cparams(("parallel", "parallel")),
        name="gla_pre",
    )(x, norm_g.reshape(1, d), scale, shift, w_in, w_a1c, w_a2bd, b_ac)


def _gla_block(e, q_ref, k_ref, v_ref, g_ref, st_ref, bc_ref, o_ref, fwd):
    c = GLA_CHUNK
    nb = GLA_BLOCK
    n = nb // c
    row = lax.broadcasted_iota(jnp.int32, (nb, nb), 0)
    col = lax.broadcasted_iota(jnp.int32, (nb, nb), 1)
    causal = (row >= col) if fwd else (row <= col)
    cum_m = (causal & ((row // c) == (col // c))).astype(BF16)
    g = g_ref[e]
    g1 = g.astype(BF16)
    g2 = (g - g1.astype(F32)).astype(BF16)
    bc_ref[...] = _dot(cum_m, g1) + _dot(cum_m, g2)
    last = c - 1 if fwd else 0
    sub = [slice(c * j, c * (j + 1)) for j in range(n)]
    order = list(range(n)) if fwd else list(range(n - 1, -1, -1))
    pos = {j: a for a, j in enumerate(order)}
    for h in range(GLA_HEADS):
        kl = pl.ds(GLA_HEAD_K * h, GLA_HEAD_K)
        bcum = bc_ref[:, kl]
        blast = [bcum[c * j + last:c * j + last + 1, :] for j in range(n)]

        def total(select):
            terms = [blast[m] for m in range(n) if select(m)]
            return sum(terms[1:], terms[0]) if terms else None

        def between(j, i):
            return total(lambda m: pos[j] < pos[m] < pos[i])

        def scaled(x, e):
            return x if e is None else x * jnp.exp(e)

        qh = q_ref[e, :, kl].astype(F32)
        kh = k_ref[e, :, kl].astype(F32)
        qd = qh * jnp.exp(bcum)
        kd = kh * jnp.exp(-bcum)
        kt = jnp.concatenate([kd[sub[j], :] * jnp.exp(blast[j]) for j in range(n)], axis=0)
        qd_b = qd.astype(BF16)
        kd_b = kd.astype(BF16)
        kt_b = kt.astype(BF16)
        vh = v_ref[e, :, pl.ds(GLA_HEAD_V * h, GLA_HEAD_V)]
        dec = jnp.exp(total(lambda m: True))

        def before(j):
            return total(lambda m: pos[m] < pos[j])

        def after(j):
            return total(lambda m: pos[m] > pos[j])

        q_in = jnp.concatenate([scaled(qd[sub[j], :], before(j)) for j in range(n)], axis=0)
        k_out = jnp.concatenate([scaled(kt[sub[j], :], after(j)) for j in range(n)], axis=0)
        rows = []
        for i in range(n):
            keys = []
            for j in range(n):
                if j == i or pos[j] > pos[i]:
                    keys.append(kd_b[sub[j], :])
                elif between(j, i) is None:
                    keys.append(kt_b[sub[j], :])
                else:
                    keys.append(scaled(kt[sub[j], :], between(j, i)).astype(BF16))
            rows.append(_dot_nt(qd_b[sub[i], :], jnp.concatenate(keys, axis=0)))
        scores = jnp.where(causal, jnp.concatenate(rows, axis=0), 0.0)
        s_t = st_ref[e, h]
        o = _dot(scores.astype(BF16), vh) + _dot_nt(q_in.astype(BF16), s_t.astype(BF16))
        o_ref[e, :, pl.ds(GLA_HEAD_V * h, GLA_HEAD_V)] = o.astype(BF16)
        st_ref[e, h] = s_t * dec + _dot_tn(vh, k_out.astype(BF16))


def _gla_core_kernel(qf_ref, kf_ref, vf_ref, gf_ref, qb_ref, kb_ref, vb_ref, gb_ref,
                     of_ref, ob_ref, stf_scr, stb_scr, bcf_scr, bcb_scr):
    @pl.when(pl.program_id(1) == 0)
    def _():
        stf_scr[...] = jnp.zeros_like(stf_scr)
        stb_scr[...] = jnp.zeros_like(stb_scr)

    for e in range(GLA_SEQS):
        _gla_block(e, qf_ref, kf_ref, vf_ref, gf_ref, stf_scr, bcf_scr, of_ref, True)
        _gla_block(e, qb_ref, kb_ref, vb_ref, gb_ref, stb_scr, bcb_scr, ob_ref, False)


def _gla_core(q, k, v, gf, gb):
    b, l, _ = q.shape
    c = GLA_BLOCK
    n = l // c
    ns = GLA_SEQS
    assert b % ns == 0 and l % c == 0

    def fw(w):
        return pl.BlockSpec((ns, c, w), lambda bi, i: (bi, i, 0))

    def bw(w):
        return pl.BlockSpec((ns, c, w), lambda bi, i: (bi, n - 1 - i, 0))

    return pl.pallas_call(
        _gla_core_kernel,
        grid=(b // ns, n),
        in_specs=[fw(GLA_DK), fw(GLA_DK), fw(GLA_DV), fw(GLA_DK),
                  bw(GLA_DK), bw(GLA_DK), bw(GLA_DV), bw(GLA_DK)],
        out_specs=[fw(GLA_DV), bw(GLA_DV)],
        out_shape=[jax.ShapeDtypeStruct((b, l, GLA_DV), BF16),
                   jax.ShapeDtypeStruct((b, l, GLA_DV), BF16)],
        scratch_shapes=[pltpu.VMEM((ns, GLA_HEADS, GLA_HEAD_V, GLA_HEAD_K), F32),
                        pltpu.VMEM((ns, GLA_HEADS, GLA_HEAD_V, GLA_HEAD_K), F32),
                        pltpu.VMEM((c, GLA_DK), F32), pltpu.VMEM((c, GLA_DK), F32)],
        compiler_params=_cparams(("parallel", "arbitrary")),
        name="gla_core",
    )(q, k, v, gf, q, k, v, gb)


def _gla_out_mlp_kernel(x_ref, of_ref, ob_ref, r_ref, ng_ref, gt1_ref, wo_ref,
                        g_ref, sc_ref, sh_ref, gt2_ref, w1_ref, w2_ref, fg_ref, o_ref,
                        x_scr, h_scr, acc_scr):
    j = pl.program_id(2)

    @pl.when(j == 0)
    def _():
        o = of_ref[0].astype(F32) + ob_ref[0].astype(F32)
        parts = []
        for h in range(GLA_HEADS):
            oh = o[:, GLA_HEAD_V * h:GLA_HEAD_V * (h + 1)]
            parts.append(oh * lax.rsqrt(jnp.mean(oh * oh, axis=-1, keepdims=True) + EPS))
        on = jnp.concatenate(parts, axis=-1) * ng_ref[...]
        r = r_ref[0].astype(F32)
        gated = on * (r * jax.nn.sigmoid(r))
        x = x_ref[0] + gt1_ref[0] * _dot(gated.astype(BF16), wo_ref[...])
        x_scr[...] = x
        h_scr[...] = _norm_mod(x, g_ref[...], sc_ref[0], sh_ref[0]).astype(BF16)
        acc_scr[...] = jnp.zeros_like(acc_scr)

    a = jnp.maximum(_dot(h_scr[...], w1_ref[...]), 0.0)
    acc_scr[...] += _dot((a * a).astype(BF16), w2_ref[...])

    @pl.when(j == pl.num_programs(2) - 1)
    def _():
        out = x_scr[...] + gt2_ref[0] * acc_scr[...]
        o_ref[0] = out * lax.rsqrt(jnp.mean(out * out, axis=-1, keepdims=True) + EPS) * fg_ref[...]


def _gla_out_mlp(x, o_f, o_b, r, gla_norm_g, gate1, w_out, norm_g, scale, shift, gate2, w1, w2,
                 final_g):
    b, l, d = x.shape
    ff = w1.shape[1]
    tm = GLA_TM
    tf = MLP_TF
    tok = pl.BlockSpec((1, tm, d), lambda bi, i, j: (bi, i, 0))
    vec = pl.BlockSpec((1, d), lambda bi, i, j: (0, 0))
    mod = pl.BlockSpec((1, 1, d), lambda bi, i, j: (bi, 0, 0))
    return pl.pallas_call(
        _gla_out_mlp_kernel,
        grid=(b, l // tm, ff // tf),
        in_specs=[
            tok, tok, tok, tok, vec, mod, pl.BlockSpec((d, d), lambda bi, i, j: (0, 0)),
            vec, mod, mod, mod,
            pl.BlockSpec((d, tf), lambda bi, i, j: (0, j)),
            pl.BlockSpec((tf, d), lambda bi, i, j: (j, 0)),
            vec,
        ],
        out_specs=tok,
        out_shape=jax.ShapeDtypeStruct((b, l, d), F32),
        scratch_shapes=[pltpu.VMEM((tm, d), F32), pltpu.VMEM((tm, d), BF16),
                        pltpu.VMEM((tm, d), F32)],
        compiler_params=_cparams(("parallel", "parallel", "arbitrary")),
        name="gla_out_mlp",
    )(x, o_f, o_b, r, gla_norm_g.reshape(1, d), gate1, w_out, norm_g.reshape(1, d), scale, shift,
      gate2, w1, w2, final_g.reshape(1, d))


def _trunk(x, mod, wts):
    b = x.shape[0]

    def mods(layer):
        m = mod[layer].reshape(b, N_MOD, 1, D_MODEL)
        return [m[:, i] for i in range(N_MOD)]

    shift1, scale1, gate1, shift2, scale2, gate2 = mods(0)
    u = _s5_pre(x, wts["norm1_g"][0], scale1, shift1)
    y = _s5_core(u, wts["s5_w1"], wts["s5_w2"], wts["s5_sc"], b)
    x = _s5_post(x, y, gate1, wts["s5_w_glu"])
    x = _mlp(x, wts["norm2_g"][0], scale2, shift2, gate2, wts["mlp_w1"][0], wts["mlp_w2"][0])
    shift1, scale1, gate1, shift2, scale2, gate2 = mods(1)
    q, k, v, r, gf, gb = _gla_pre(x, wts["norm1_g"][1], scale1, shift1, wts["gla_w_in"],
                                  wts["gla_w_a1"], wts["gla_w_a2"], wts["gla_b_a"])
    o_f, o_b = _gla_core(q, k, v, gf, gb)
    return _gla_out_mlp(x, o_f, o_b, r, wts["gla_norm_g"], gate1, wts["gla_w_out"],
                        wts["norm2_g"][1], scale2, shift2, gate2, wts["mlp_w1"][1],
                        wts["mlp_w2"][1], wts["final_g"])


def kernel(x_prompt, x_sample, c_prompt, c_sample, ada_w, ada_b, norm1_g, norm2_g, s5_lam_re, s5_lam_im, s5_log_dt, s5_b_re, s5_b_im, s5_c_re, s5_c_im, s5_d, s5_w_glu, gla_w_in, gla_w_a1, gla_w_a2, gla_b_a, gla_norm_g, gla_w_out, mlp_w1, mlp_w2, final_g):
    bp, bs = c_prompt.shape[0], c_sample.shape[0]
    pad = (-(bp + bs)) % SUBLANES
    c_all = jnp.concatenate([c_prompt, c_sample, jnp.zeros((pad, D_MODEL), F32)], axis=0)
    mod_all = _modulation(c_all, ada_w, ada_b)

    s5_w1, s5_w2, s5_sc = _s5_build(s5_lam_re[0], s5_lam_im[0], s5_log_dt[0], s5_b_re[0],
                                    s5_b_im[0], s5_c_re[0], s5_c_im[0], s5_d[0])
    r = GLA_GATE_RANK
    w_a2bd = jnp.zeros((2 * r, 2 * GLA_DK), F32)
    w_a2bd = w_a2bd.at[:r, :GLA_DK].set(gla_w_a2[0, 0]).at[r:, GLA_DK:].set(gla_w_a2[0, 1])
    wts = {
        "norm1_g": norm1_g, "norm2_g": norm2_g, "final_g": final_g,
        "s5_w1": s5_w1, "s5_w2": s5_w2, "s5_sc": s5_sc,
        "s5_w_glu": s5_w_glu[0].astype(BF16),
        "mlp_w1": mlp_w1.astype(BF16), "mlp_w2": mlp_w2.astype(BF16),
        "gla_w_in": gla_w_in[0].astype(BF16),
        "gla_w_a1": jnp.concatenate([gla_w_a1[0, 0], gla_w_a1[0, 1]], axis=1).astype(BF16),
        "gla_w_a2": w_a2bd.astype(BF16),
        "gla_b_a": jnp.concatenate([gla_b_a[0, 0], gla_b_a[0, 1]], axis=0).reshape(1, 2 * GLA_DK),
        "gla_norm_g": gla_norm_g[0], "gla_w_out": gla_w_out[0].astype(BF16),
    }
    y_prompt = _trunk(x_prompt, mod_all[:, :bp], wts)
    y_sample = _trunk(x_sample, mod_all[:, bp:bp + bs], wts)
    return (y_prompt, y_sample)
```

```python
import functools

import jax
import jax.numpy as jnp
from jax import lax
from jax.experimental import pallas as pl
from jax.experimental.pallas import tpu as pltpu

F32 = jnp.float32
BF16 = jnp.bfloat16

D_MODEL = 1024
S5_GROUP = 16
S5_GROUPS = D_MODEL // S5_GROUP
S5_STATE = 64
GLA_HEADS = 4
GLA_DK = D_MODEL // 2
GLA_DV = D_MODEL
GLA_HEAD_K = GLA_DK // GLA_HEADS
GLA_HEAD_V = GLA_DV // GLA_HEADS
GLA_GATE_RANK = 16
GLA_GATE_TAU = 16.0
GLA_CHUNK = 64
GLA_BLOCK = 256
GLA_SEQS = 2
N_MOD = 6
EPS = 1e-6

LANES = 128
SUBLANES = 8
SEGS = LANES // S5_GROUP

S5_C = 32
S5_CW = S5_C * S5_GROUP
S5_NSTATE = 4 * S5_STATE
S5_CORE_GROUPS = 2
S5_LEVELS = 3
S5_GLU_COLS = 256
S5_TBK = 1024
S5_PITCH = S5_C + 4

MOD_TN = 1536
MLP_TM = 1024
MLP_TF = 2048
GLA_TM = 512

VMEM_LIMIT = 56 * 1024 * 1024


def _cparams(sem):
    return pltpu.CompilerParams(dimension_semantics=sem, vmem_limit_bytes=VMEM_LIMIT)


def _dot(a, b, precision=None):
    return jnp.dot(a, b, preferred_element_type=F32, precision=precision)


def _dot_nt(a, b, precision=None):
    return lax.dot_general(a, b, (((1,), (1,)), ((), ())),
                           preferred_element_type=F32, precision=precision)


def _dot_tn(a, b, precision=None):
    return lax.dot_general(a, b, (((0,), (0,)), ((), ())),
                           preferred_element_type=F32, precision=precision)


def _norm_mod(x, g, scale, shift):
    y = x * lax.rsqrt(jnp.mean(x * x, axis=-1, keepdims=True) + EPS)
    return y * (g * (1.0 + scale)) + shift


def _mod_kernel(c_ref, w_ref, b_ref, o_ref):
    c = c_ref[...]
    s = c * jax.nn.sigmoid(c)
    o_ref[0] = _dot(s.astype(BF16), w_ref[0].astype(BF16)) + b_ref[0]


def _modulation(c_all, ada_w, ada_b):
    depth, d, n = ada_w.shape
    rows = c_all.shape[0]
    tn = MOD_TN
    return pl.pallas_call(
        _mod_kernel,
        grid=(depth, n // tn),
        in_specs=[
            pl.BlockSpec((rows, d), lambda l, j: (0, 0)),
            pl.BlockSpec((1, d, tn), lambda l, j: (l, 0, j)),
            pl.BlockSpec((1, 1, tn), lambda l, j: (l, 0, j)),
        ],
        out_specs=pl.BlockSpec((1, rows, tn), lambda l, j: (l, 0, j)),
        out_shape=jax.ShapeDtypeStruct((depth, rows, n), F32),
        compiler_params=_cparams(("parallel", "parallel")),
        name="adaln_mod",
    )(c_all, ada_w, ada_b.reshape(depth, 1, n))


def _seg_exchange(lo, hi, d, seg):
    keep = (seg & d) == 0
    return (jnp.where(keep, lo, pltpu.roll(hi, S5_GROUP * d, 1)),
            jnp.where(keep, pltpu.roll(lo, LANES - S5_GROUP * d, 1), hi))


def _seg_transpose8(ngroups, load, store, buf_a, buf_b, seg):
    def put_a(n, i, v):
        buf_a[n, i] = v

    def put_b(n, i, v):
        buf_b[n, i] = v

    stages = ((4, load, put_a),
              (2, lambda n, i: buf_a[n, i], put_b),
              (1, lambda n, i: buf_b[n, i], store))
    for d, get, put in stages:
        for n in range(ngroups):
            for i in range(SEGS):
                if i & d == 0:
                    lo, hi = _seg_exchange(get(n, i), get(n, i + d), d, seg)
                    put(n, i, lo)
                    put(n, i + d, hi)


def _seg_buffers(rows, dtype):
    ngroups = D_MODEL // LANES * (S5_C // SEGS)
    return [pltpu.VMEM((ngroups, SEGS, rows, LANES), dtype) for _ in range(2)]


def _s5_pre_kernel(x_ref, g_ref, sc_ref, sh_ref, u_ref, h_scr, buf_a, buf_b):
    h = _norm_mod(x_ref[0], g_ref[...], sc_ref[0], sh_ref[0])
    nk = S5_TBK // S5_C
    nq = S5_C // SEGS
    for k in range(nk):
        for j in range(D_MODEL // LANES):
            h_scr[j, pl.ds(S5_PITCH * k, S5_C), :] = h[S5_C * k:S5_C * (k + 1), LANES * j:LANES * (j + 1)]
    seg = lax.broadcasted_iota(jnp.int32, (nk, LANES), 1) // S5_GROUP

    def load(n, i):
        return h_scr[n // nq, pl.ds(SEGS * (n % nq) + i, nk, stride=S5_PITCH), :]

    def store(n, gl, v):
        u_ref[SEGS * (n // nq) + gl, :, pl.ds(LANES * (n % nq), LANES)] = v.astype(BF16)

    _seg_transpose8(D_MODEL // LANES * nq, load, store, buf_a, buf_b, seg)


def _s5_pre(x, norm_g, scale, shift):
    b, l, d = x.shape
    nb = l // S5_TBK
    nk = S5_TBK // S5_C
    return pl.pallas_call(
        _s5_pre_kernel,
        grid=(b, nb),
        in_specs=[
            pl.BlockSpec((1, S5_TBK, d), lambda bi, i: (bi, i, 0)),
            pl.BlockSpec((1, d), lambda bi, i: (0, 0)),
            pl.BlockSpec((1, 1, d), lambda bi, i: (bi, 0, 0)),
            pl.BlockSpec((1, 1, d), lambda bi, i: (bi, 0, 0)),
        ],
        out_specs=pl.BlockSpec((S5_GROUPS, nk, S5_CW), lambda bi, i: (0, bi * nb + i, 0)),
        out_shape=jax.ShapeDtypeStruct((S5_GROUPS, b * l // S5_C, S5_CW), BF16),
        scratch_shapes=[pltpu.VMEM((d // LANES, nk * S5_PITCH, LANES), F32)] + _seg_buffers(nk, F32),
        compiler_params=_cparams(("parallel", "parallel")),
        name="s5_pre",
    )(x, norm_g.reshape(1, d), scale, shift)


def _s5_build_kernel(rows_ref, rowsfb_ref, bcat_ref, bswp_ref, ccat_ref, cswp_ref, dsk_ref,
                     w1_ref, w2_ref, sc_ref, ex_scr, mat_scr, kt_scr):
    c = S5_C
    cw = S5_CW
    hi = lax.Precision.HIGHEST
    nt = ((c + 1 + SUBLANES - 1) // SUBLANES) * SUBLANES
    lane1 = lax.broadcasted_iota(jnp.int32, (1, LANES), 1)
    sgn = jnp.where(lane1 < S5_STATE, -1.0, 1.0).astype(F32)
    nrow = lax.broadcasted_iota(jnp.int32, (nt, LANES), 0).astype(F32)

    for d in range(2):
        lr = rows_ref[0, d, 0:1, :]
        li = rows_ref[0, d, 1:2, :]
        dt = jnp.exp(rows_ref[0, d, 2:3, :])
        mag = jnp.exp(nrow * (lr * dt))
        ang = nrow * (li * dt)
        p_re = mag * jnp.cos(ang)
        p_im = mag * jnp.sin(ang)
        ab_re = p_re[1:2, :]
        ab_im = p_im[1:2, :]
        den = lr * lr + li * li
        z_re = ((ab_re - 1.0) * lr + ab_im * li) / den
        z_im = (ab_im * lr - (ab_re - 1.0) * li) / den
        bbar = z_re * bcat_ref[0, d] + (z_im * sgn) * bswp_ref[0, d]
        p_is = p_im * sgn
        for n in range(c + 1):
            blk = pl.ds(S5_GROUP * n, S5_GROUP)
            ex_scr[0, blk, :] = jnp.broadcast_to(p_re[n:n + 1, :], (S5_GROUP, LANES))
            ex_scr[1, blk, :] = jnp.broadcast_to(p_is[n:n + 1, :], (S5_GROUP, LANES))
            ex_scr[2, blk, :] = jnp.broadcast_to(p_re[c - n:c - n + 1, :], (S5_GROUP, LANES))
            ex_scr[3, blk, :] = jnp.broadcast_to(p_is[c - n:c - n + 1, :], (S5_GROUP, LANES))
        c_t = jnp.concatenate([ccat_ref[0, d]] * c, axis=0)
        c_s = jnp.concatenate([cswp_ref[0, d]] * c, axis=0)
        bswap = z_re * bswp_ref[0, d] - (z_im * sgn) * bcat_ref[0, d]
        b_t = jnp.concatenate([bbar] * c, axis=0)
        b_s = jnp.concatenate([bswap] * c, axis=0)
        lo = pl.ds(0, cw)
        up = pl.ds(S5_GROUP, cw)
        bneg = bbar * (-sgn)
        if d == 0:
            ct = ex_scr[0, lo, :] * c_t + ex_scr[1, lo, :] * c_s
            kt_scr[0] = _dot_nt(bneg, ct, hi)
            mat_scr[0] = ex_scr[2, up, :] * b_t + ex_scr[3, up, :] * b_s
            mat_scr[2] = ex_scr[0, up, :] * c_t + ex_scr[1, up, :] * c_s
        else:
            ct = ex_scr[2, up, :] * c_t + ex_scr[3, up, :] * c_s
            kt_scr[1] = _dot_nt(bneg, ct, hi)
            mat_scr[1] = ex_scr[0, lo, :] * b_t + ex_scr[1, lo, :] * b_s
            mat_scr[3] = ex_scr[2, lo, :] * c_t + ex_scr[3, lo, :] * c_s

    lane_cw = lax.broadcasted_iota(jnp.int32, (S5_GROUP, cw), 1)
    chan = lax.broadcasted_iota(jnp.int32, (S5_GROUP, cw), 0)
    dsk = dsk_ref[0]
    kf = kt_scr[0]
    kb = kt_scr[1]
    for s in range(c):
        rf = kf if s == 0 else pltpu.roll(kf, S5_GROUP * s, 1)
        rb = kb if s == c - 1 else pltpu.roll(kb, S5_GROUP * (s + 1), 1)
        blk = (jnp.where(lane_cw >= S5_GROUP * s, rf, 0.0)
               + jnp.where(lane_cw < S5_GROUP * (s + 1), rb, 0.0)
               + jnp.where(lane_cw == S5_GROUP * s + chan, dsk, 0.0))
        w1_ref[0, pl.ds(S5_GROUP * s, S5_GROUP), pl.ds(0, cw)] = blk.astype(BF16)

    lane_m = lax.broadcasted_iota(jnp.int32, (cw, LANES), 1) < S5_STATE
    bmf = mat_scr[0]
    bmb = mat_scr[1]
    w1_ref[0, :, pl.ds(cw, LANES)] = jnp.where(
        lane_m, bmf, pltpu.roll(bmb, S5_STATE, 1)).astype(BF16)
    w1_ref[0, :, pl.ds(cw + LANES, LANES)] = jnp.where(
        lane_m, pltpu.roll(bmf, S5_STATE, 1), bmb).astype(BF16)
    caf = mat_scr[2]
    cab = mat_scr[3]
    w2t_re = jnp.where(lane_m, caf, pltpu.roll(cab, S5_STATE, 1))
    w2t_im = -jnp.where(lane_m, pltpu.roll(caf, S5_STATE, 1), cab)
    w2_ref[0, pl.ds(0, LANES), :] = w2t_re.T.astype(BF16)
    w2_ref[0, pl.ds(LANES, LANES), :] = w2t_im.T.astype(BF16)

    lrfb = rowsfb_ref[0, 0:1, :]
    lifb = rowsfb_ref[0, 1:2, :]
    dtfb = jnp.exp(rowsfb_ref[0, 2:3, :])
    row8 = lax.broadcasted_iota(jnp.int32, (SUBLANES, LANES), 0)
    fwd8 = lax.broadcasted_iota(jnp.int32, (SUBLANES, LANES), 1) < S5_STATE
    mag_c = jnp.exp(float(c) * (lrfb * dtfb))
    ang_c = float(c) * (lifb * dtfb)
    base = (mag_c * jnp.cos(ang_c), mag_c * jnp.sin(ang_c))

    def cmul(a, b):
        return a[0] * b[0] - a[1] * b[1], a[0] * b[1] + a[1] * b[0]

    for lvl in range(S5_LEVELS):
        pw = [base]
        for _ in range(SUBLANES - 1):
            pw.append(cmul(pw[-1], base))
        for k, dd in enumerate((1, 2, 4)):
            ok = (fwd8 & (row8 >= dd)) | (jnp.logical_not(fwd8) & (row8 < SUBLANES - dd))
            sc_ref[0, lvl * 8 + 2 * k] = jnp.where(ok, pw[dd - 1][0], 0.0)
            sc_ref[0, lvl * 8 + 2 * k + 1] = jnp.where(ok, pw[dd - 1][1], 0.0)
        ap_re = jnp.zeros((SUBLANES, LANES), F32)
        ap_im = jnp.zeros((SUBLANES, LANES), F32)
        for n in range(1, SUBLANES + 1):
            here = (fwd8 & (row8 == n - 1)) | (jnp.logical_not(fwd8) & (row8 == SUBLANES - n))
            ap_re = jnp.where(here, pw[n - 1][0], ap_re)
            ap_im = jnp.where(here, pw[n - 1][1], ap_im)
        sc_ref[0, lvl * 8 + 6] = ap_re
        sc_ref[0, lvl * 8 + 7] = ap_im
        base = pw[SUBLANES - 1]


def _s5_build(lam_re, lam_im, log_dt, b_re, b_im, c_re, c_im, d_skip):
    g, p = S5_GROUPS, S5_STATE
    dsk = jnp.tile(d_skip.reshape(g, 1, S5_GROUP), (1, 1, S5_C))

    def dup(a):
        return jnp.concatenate([a, a], axis=-1).transpose(1, 0, 2)[:, :, None, :]

    ldt = jnp.broadcast_to(log_dt.T[:, :, None, None], (g, 2, 1, 2 * p))
    rows = jnp.concatenate(
        [dup(lam_re), dup(lam_im), ldt, jnp.zeros((g, 2, SUBLANES - 3, 2 * p), F32)], axis=2)

    def fb(a):
        return jnp.concatenate([a[0], a[1]], axis=-1)[:, None, :]

    ldt2 = jnp.broadcast_to(log_dt[:, :, None], (2, g, p))
    rowsfb = jnp.concatenate(
        [fb(lam_re), fb(lam_im), fb(ldt2), jnp.zeros((g, SUBLANES - 3, 2 * p), F32)], axis=1)
    brt = b_re.transpose(1, 0, 3, 2)
    bit = b_im.transpose(1, 0, 3, 2)
    crt = c_re.transpose(1, 0, 2, 3)
    cit = c_im.transpose(1, 0, 2, 3)
    bcat = jnp.concatenate([brt, bit], axis=-1)
    bswp = jnp.concatenate([bit, brt], axis=-1)
    ccat = jnp.concatenate([crt, cit], axis=-1)
    cswp = jnp.concatenate([cit, crt], axis=-1)
    nsc = S5_LEVELS * 8
    spec4 = pl.BlockSpec((1, 2, S5_GROUP, LANES), lambda i: (i, 0, 0, 0))
    return pl.pallas_call(
        _s5_build_kernel,
        grid=(g,),
        in_specs=[
            pl.BlockSpec((1, 2, SUBLANES, LANES), lambda i: (i, 0, 0, 0)),
            pl.BlockSpec((1, SUBLANES, LANES), lambda i: (i, 0, 0)),
            spec4, spec4, spec4, spec4,
            pl.BlockSpec((1, 1, S5_CW), lambda i: (i, 0, 0)),
        ],
        out_specs=[
            pl.BlockSpec((1, S5_CW, S5_CW + S5_NSTATE), lambda i: (i, 0, 0)),
            pl.BlockSpec((1, S5_NSTATE, S5_CW), lambda i: (i, 0, 0)),
            pl.BlockSpec((1, nsc, SUBLANES, LANES), lambda i: (i, 0, 0, 0)),
        ],
        out_shape=[
            jax.ShapeDtypeStruct((g, S5_CW, S5_CW + S5_NSTATE), BF16),
            jax.ShapeDtypeStruct((g, S5_NSTATE, S5_CW), BF16),
            jax.ShapeDtypeStruct((g, nsc, SUBLANES, LANES), F32),
        ],
        scratch_shapes=[
            pltpu.VMEM((4, (S5_C + 1) * S5_GROUP, LANES), F32),
            pltpu.VMEM((4, S5_CW, LANES), F32),
            pltpu.VMEM((2, S5_GROUP, S5_CW), F32),
        ],
        compiler_params=_cparams(("parallel",)),
        name="s5_build",
    )(rows, rowsfb, bcat, bswp, ccat, cswp, dsk)


def _tile_scan(re, im, sc_ref, lvl, fwd):
    for k, dd in enumerate((1, 2, 4)):
        mr = sc_ref[0, lvl * 8 + 2 * k]
        mi = sc_ref[0, lvl * 8 + 2 * k + 1]
        sre = jnp.where(fwd, pltpu.roll(re, dd, 0), pltpu.roll(re, SUBLANES - dd, 0))
        sim = jnp.where(fwd, pltpu.roll(im, dd, 0), pltpu.roll(im, SUBLANES - dd, 0))
        re, im = re + mr * sre - mi * sim, im + mr * sim + mi * sre
    return re, im


def _scan_rows(bufs, lvl, base, ntiles, sc_ref, p_ref, p_base):
    re_ref, im_ref = bufs[lvl]
    fwd = lax.broadcasted_iota(jnp.int32, (SUBLANES, LANES), 1) < S5_STATE
    row = lax.broadcasted_iota(jnp.int32, (SUBLANES, LANES), 0)
    fwd1 = fwd[0:1, :]

    def carried(hre, him, cre, cim):
        pre = jnp.where(fwd, jnp.where(row == 0, cre, pltpu.roll(hre, 1, 0)),
                        jnp.where(row == SUBLANES - 1, cre, pltpu.roll(hre, SUBLANES - 1, 0)))
        pim = jnp.where(fwd, jnp.where(row == 0, cim, pltpu.roll(him, 1, 0)),
                        jnp.where(row == SUBLANES - 1, cim, pltpu.roll(him, SUBLANES - 1, 0)))
        return pre, pim

    if ntiles == 1:
        sl = pl.ds(base, SUBLANES)
        hre, him = _tile_scan(re_ref[sl, :], im_ref[sl, :], sc_ref, lvl, fwd)
        re_ref[sl, :] = hre
        im_ref[sl, :] = him
        if lvl == 0:
            zero = jnp.zeros((1, LANES), F32)
            pre, pim = carried(hre, him, zero, zero)
            p_ref[pl.ds(p_base, SUBLANES), pl.ds(0, LANES)] = pre
            p_ref[pl.ds(p_base, SUBLANES), pl.ds(LANES, LANES)] = pim
        return

    ere_ref, eim_ref = bufs[lvl + 1]
    ntn = (ntiles + SUBLANES - 1) // SUBLANES
    ere_ref[...] = jnp.zeros_like(ere_ref)
    eim_ref[...] = jnp.zeros_like(eim_ref)

    for i in range(ntiles):
        sl = pl.ds(base + i * SUBLANES, SUBLANES)
        hre, him = _tile_scan(re_ref[sl, :], im_ref[sl, :], sc_ref, lvl, fwd)
        re_ref[sl, :] = hre
        im_ref[sl, :] = him
        ere_ref[pl.ds(SUBLANES + i, 1), :] = jnp.where(fwd1, hre[SUBLANES - 1:SUBLANES, :], hre[0:1, :])
        eim_ref[pl.ds(SUBLANES + i, 1), :] = jnp.where(fwd1, him[SUBLANES - 1:SUBLANES, :], him[0:1, :])

    _scan_rows(bufs, lvl + 1, SUBLANES, ntn, sc_ref, None, 0)
    apr = sc_ref[0, lvl * 8 + 6]
    api = sc_ref[0, lvl * 8 + 7]

    for i in range(ntiles):
        sl = pl.ds(base + i * SUBLANES, SUBLANES)
        cre = jnp.where(fwd1, ere_ref[pl.ds(SUBLANES - 1 + i, 1), :], ere_ref[pl.ds(SUBLANES + 1 + i, 1), :])
        cim = jnp.where(fwd1, eim_ref[pl.ds(SUBLANES - 1 + i, 1), :], eim_ref[pl.ds(SUBLANES + 1 + i, 1), :])
        hre = re_ref[sl, :] + apr * cre - api * cim
        him = im_ref[sl, :] + apr * cim + api * cre
        if lvl == 0:
            psl = pl.ds(p_base + i * SUBLANES, SUBLANES)
            pre, pim = carried(hre, him, cre, cim)
            p_ref[psl, pl.ds(0, LANES)] = pre
            p_ref[psl, pl.ds(LANES, LANES)] = pim
        else:
            re_ref[sl, :] = hre
            im_ref[sl, :] = him


def _s5_core_kernel(nseq, u_ref, w1_ref, w2_ref, sc_ref, y_ref,
                    s_re, s_im, e1_re, e1_im, e2_re, e2_im, p_scr, yi_scr):
    rows = u_ref.shape[1]
    nk = rows // nseq
    bufs = [(s_re, s_im), (e1_re, e1_im), (e2_re, e2_im)]
    for gi in range(S5_CORE_GROUPS):
        u = u_ref[gi]
        st = _dot(u, w1_ref[gi, :, pl.ds(S5_CW, S5_NSTATE)])
        s_re[...] = st[:, :LANES]
        s_im[...] = st[:, LANES:]
        yi_scr[...] = _dot(u, w1_ref[gi, :, pl.ds(0, S5_CW)])
        sc_g = sc_ref.at[pl.ds(gi, 1)]
        for b in range(nseq):
            _scan_rows(bufs, 0, b * nk, nk // SUBLANES, sc_g, p_scr, b * nk)
        y_ref[gi] = (yi_scr[...] + _dot(p_scr[...].astype(BF16), w2_ref[gi])).astype(BF16)


def _s5_core(u, w1, w2, sc, nseq):
    g, rows, cw = u.shape
    gs = S5_CORE_GROUPS
    nk = rows // nseq
    nt0 = nk // SUBLANES
    e1_rows = ((nt0 + SUBLANES - 1) // SUBLANES) * SUBLANES + 2 * SUBLANES
    nt1 = (nt0 + SUBLANES - 1) // SUBLANES
    e2_rows = ((nt1 + SUBLANES - 1) // SUBLANES) * SUBLANES + 2 * SUBLANES
    assert nt1 <= SUBLANES * SUBLANES, "sequence too long for S5_LEVELS scan levels"
    return pl.pallas_call(
        functools.partial(_s5_core_kernel, nseq),
        grid=(g // gs,),
        in_specs=[
            pl.BlockSpec((gs, rows, cw), lambda i: (i, 0, 0)),
            pl.BlockSpec((gs, cw, cw + S5_NSTATE), lambda i: (i, 0, 0)),
            pl.BlockSpec((gs, S5_NSTATE, cw), lambda i: (i, 0, 0)),
            pl.BlockSpec((gs, S5_LEVELS * 8, SUBLANES, LANES), lambda i: (i, 0, 0, 0)),
        ],
        out_specs=pl.BlockSpec((gs, rows, cw), lambda i: (i, 0, 0)),
        out_shape=jax.ShapeDtypeStruct((g, rows, cw), BF16),
        scratch_shapes=[
            pltpu.VMEM((rows, LANES), F32), pltpu.VMEM((rows, LANES), F32),
            pltpu.VMEM((e1_rows, LANES), F32), pltpu.VMEM((e1_rows, LANES), F32),
            pltpu.VMEM((e2_rows, LANES), F32), pltpu.VMEM((e2_rows, LANES), F32),
            pltpu.VMEM((rows, 2 * LANES), F32),
            pltpu.VMEM((rows, cw), F32),
        ],
        compiler_params=_cparams(("parallel",)),
        name="s5_core",
    )(u, w1, w2, sc)


def _s5_post_kernel(x_ref, y_ref, gt_ref, wg_ref, o_ref, y_scr, buf_a, buf_b):
    nk = S5_TBK // S5_C
    nq = S5_C // SEGS
    seg = lax.broadcasted_iota(jnp.int32, (nk, LANES), 1) // S5_GROUP

    def load(n, gl):
        return y_ref[SEGS * (n // nq) + gl, :, pl.ds(LANES * (n % nq), LANES)]

    def store(n, i, v):
        y_scr[n // nq, pl.ds(SEGS * (n % nq) + i, nk, stride=S5_PITCH), :] = v.astype(F32)

    _seg_transpose8(D_MODEL // LANES * nq, load, store, buf_a, buf_b, seg)
    y = jnp.concatenate(
        [jnp.concatenate([y_scr[j, pl.ds(S5_PITCH * k, S5_C), :] for j in range(D_MODEL // LANES)], axis=-1)
         for k in range(nk)], axis=0)
    g = jax.nn.gelu(y).astype(BF16)
    for cb in range(D_MODEL // S5_GLU_COLS):
        cols = pl.ds(cb * S5_GLU_COLS, S5_GLU_COLS)
        val = _dot(g, wg_ref[:, cols])
        gate = _dot(g, wg_ref[:, pl.ds(D_MODEL + cb * S5_GLU_COLS, S5_GLU_COLS)])
        o_ref[0, :, cols] = x_ref[0, :, cols] + gt_ref[0, :, cols] * (val * jax.nn.sigmoid(gate))


def _s5_post(x, y, gate, w_glu):
    b, l, d = x.shape
    nb = l // S5_TBK
    nk = S5_TBK // S5_C
    mod = pl.BlockSpec((1, 1, d), lambda bi, i: (bi, 0, 0))
    return pl.pallas_call(
        _s5_post_kernel,
        grid=(b, nb),
        in_specs=[
            pl.BlockSpec((1, S5_TBK, d), lambda bi, i: (bi, i, 0)),
            pl.BlockSpec((S5_GROUPS, nk, S5_CW), lambda bi, i: (0, bi * nb + i, 0)),
            mod,
            pl.BlockSpec((d, 2 * d), lambda bi, i: (0, 0)),
        ],
        out_specs=pl.BlockSpec((1, S5_TBK, d), lambda bi, i: (bi, i, 0)),
        out_shape=jax.ShapeDtypeStruct((b, l, d), F32),
        scratch_shapes=[pltpu.VMEM((d // LANES, nk * S5_PITCH, LANES), F32)] + _seg_buffers(nk, BF16),
        compiler_params=_cparams(("parallel", "parallel")),
        name="s5_post",
    )(x, y, gate, w_glu)


def _mlp_kernel(x_ref, g_ref, sc_ref, sh_ref, gt_ref, w1_ref, w2_ref, o_ref, h_scr, acc_scr):
    j = pl.program_id(2)

    @pl.when(j == 0)
    def _():
        h_scr[...] = _norm_mod(x_ref[0], g_ref[...], sc_ref[0], sh_ref[0]).astype(BF16)
        acc_scr[...] = jnp.zeros_like(acc_scr)

    a = jnp.maximum(_dot(h_scr[...], w1_ref[...]), 0.0)
    acc_scr[...] += _dot((a * a).astype(BF16), w2_ref[...])

    @pl.when(j == pl.num_programs(2) - 1)
    def _():
        o_ref[0] = x_ref[0] + gt_ref[0] * acc_scr[...]


def _mlp(x, norm_g, scale, shift, gate, w1, w2):
    b, l, d = x.shape
    ff = w1.shape[1]
    tm = min(MLP_TM, l)
    tf = MLP_TF
    vec = pl.BlockSpec((1, d), lambda bi, i, j: (0, 0))
    mod = pl.BlockSpec((1, 1, d), lambda bi, i, j: (bi, 0, 0))
    return pl.pallas_call(
        _mlp_kernel,
        grid=(b, l // tm, ff // tf),
        in_specs=[
            pl.BlockSpec((1, tm, d), lambda bi, i, j: (bi, i, 0)),
            vec, mod, mod, mod,
            pl.BlockSpec((d, tf), lambda bi, i, j: (0, j)),
            pl.BlockSpec((tf, d), lambda bi, i, j: (j, 0)),
        ],
        out_specs=pl.BlockSpec((1, tm, d), lambda bi, i, j: (bi, i, 0)),
        out_shape=jax.ShapeDtypeStruct((b, l, d), F32),
        scratch_shapes=[pltpu.VMEM((tm, d), BF16), pltpu.VMEM((tm, d), F32)],
        compiler_params=_cparams(("parallel", "parallel", "arbitrary")),
        name="mlp",
    )(x, norm_g.reshape(1, d), scale, shift, gate, w1, w2)


def _gla_pre_kernel(x_ref, g_ref, sc_ref, sh_ref, win_ref, wa1_ref, wa2_ref, ba_ref,
                    q_ref, k_ref, v_ref, r_ref, gf_ref, gb_ref):
    h = _norm_mod(x_ref[0], g_ref[...], sc_ref[0], sh_ref[0]).astype(BF16)
    proj = _dot(h, win_ref[...])
    q_ref[0] = proj[:, :GLA_DK] * (GLA_HEAD_K ** -0.5)
    k_ref[0] = proj[:, GLA_DK:2 * GLA_DK]
    v_ref[0] = proj[:, 2 * GLA_DK:2 * GLA_DK + GLA_DV].astype(BF16)
    r_ref[0] = proj[:, 2 * GLA_DK + GLA_DV:].astype(BF16)
    a = _dot(h, wa1_ref[...]).astype(BF16)
    z = _dot(a, wa2_ref[...]) + ba_ref[...]
    lg = (jnp.minimum(z, 0.0) - jnp.log(1.0 + jnp.exp(-jnp.abs(z)))) * (1.0 / GLA_GATE_TAU)
    gf_ref[0] = lg[:, :GLA_DK]
    gb_ref[0] = lg[:, GLA_DK:]


def _gla_pre(x, norm_g, scale, shift, w_in, w_a1c, w_a2bd, b_ac):
    b, l, d = x.shape
    tm = GLA_TM
    vec =pl.BlockSpec((1, d), lambda bi, i: (0, 0))
    mod = pl.BlockSpec((1, 1, d), lambda bi, i: (bi, 0, 0))

    def full(a):
        return pl.BlockSpec(a.shape, lambda bi, i: (0,) * a.ndim)

    def tok(w):
        return pl.BlockSpec((1, tm, w), lambda bi, i: (bi, i, 0))

    return pl.pallas_call(
        _gla_pre_kernel,
        grid=(b, l // tm),
        in_specs=[tok(d), vec, mod, mod, full(w_in), full(w_a1c), full(w_a2bd), full(b_ac)],
        out_specs=[tok(GLA_DK), tok(GLA_DK), tok(GLA_DV), tok(GLA_DV), tok(GLA_DK), tok(GLA_DK)],
        out_shape=[
            jax.ShapeDtypeStruct((b, l, GLA_DK), F32),
            jax.ShapeDtypeStruct((b, l, GLA_DK), F32),
            jax.ShapeDtypeStruct((b, l, GLA_DV), BF16),
            jax.ShapeDtypeStruct((b, l, GLA_DV), BF16),
            jax.ShapeDtypeStruct((b, l, GLA_DK), F32),
            jax.ShapeDtypeStruct((b, l, GLA_DK), F32),
        ],
        compiler_params=_cparams(("parallel", "parallel")),
        name="gla_pre",
    )(x, norm_g.reshape(1, d), scale, shift, w_in, w_a1c, w_a2bd, b_ac)


def _gla_block(e, q_ref, k_ref, v_ref, g_ref, st_ref, bc_ref, o_ref, fwd):
    c = GLA_CHUNK
    nb = GLA_BLOCK
    n = nb // c
    row = lax.broadcasted_iota(jnp.int32, (nb, nb), 0)
    col = lax.broadcasted_iota(jnp.int32, (nb, nb), 1)
    causal = (row >= col) if fwd else (row <= col)
    cum_m = (causal & ((row // c) == (col // c))).astype(BF16)
    g = g_ref[e]
    g1 = g.astype(BF16)
    g2 = (g - g1.astype(F32)).astype(BF16)
    bc_ref[...] = _dot(cum_m, g1) + _dot(cum_m, g2)
    last = c - 1 if fwd else 0
    sub = [slice(c * j, c * (j + 1)) for j in range(n)]
    order = list(range(n)) if fwd else list(range(n - 1, -1, -1))
    pos = {j: a for a, j in enumerate(order)}
    for h in range(GLA_HEADS):
        kl = pl.ds(GLA_HEAD_K * h, GLA_HEAD_K)
        bcum = bc_ref[:, kl]
        blast = [bcum[c * j + last:c * j + last + 1, :] for j in range(n)]

        def total(select):
            terms = [blast[m] for m in range(n) if select(m)]
            return sum(terms[1:], terms[0]) if terms else None

        def between(j, i):
            return total(lambda m: pos[j] < pos[m] < pos[i])

        def scaled(x, e):
            return x if e is None else x * jnp.exp(e)

        qh = q_ref[e, :, kl]
        kh = k_ref[e, :, kl]
        qd = qh * jnp.exp(bcum)
        kd = kh * jnp.exp(-bcum)
        kt = jnp.concatenate([kd[sub[j], :] * jnp.exp(blast[j]) for j in range(n)], axis=0)
        qd_b = qd.astype(BF16)
        kd_b = kd.astype(BF16)
        kt_b = kt.astype(BF16)
        vh = v_ref[e, :, pl.ds(GLA_HEAD_V * h, GLA_HEAD_V)]
        dec = jnp.exp(total(lambda m: True))

        def before(j):
            return total(lambda m: pos[m] < pos[j])

        def after(j):
            return total(lambda m: pos[m] > pos[j])

        q_in = jnp.concatenate([scaled(qd[sub[j], :], before(j)) for j in range(n)], axis=0)
        k_out = jnp.concatenate([scaled(kt[sub[j], :], after(j)) for j in range(n)], axis=0)
        rows = []
        for i in range(n):
            keys = []
            for j in range(n):
                if j == i or pos[j] > pos[i]:
                    keys.append(kd_b[sub[j], :])
                elif between(j, i) is None:
                    keys.append(kt_b[sub[j], :])
                else:
                    keys.append(scaled(kt[sub[j], :], between(j, i)).astype(BF16))
            rows.append(_dot_nt(qd_b[sub[i], :], jnp.concatenate(keys, axis=0)))
        scores = jnp.where(causal, jnp.concatenate(rows, axis=0), 0.0)
        s_t = st_ref[e, h]
        o = _dot(scores.astype(BF16), vh) + _dot_nt(q_in.astype(BF16), s_t.astype(BF16))
        o_ref[e, :, pl.ds(GLA_HEAD_V * h, GLA_HEAD_V)] = o.astype(BF16)
        st_ref[e, h] = s_t * dec + _dot_tn(vh, k_out.astype(BF16))


def _gla_core_kernel(qf_ref, kf_ref, vf_ref, gf_ref, qb_ref, kb_ref, vb_ref, gb_ref,
                     of_ref, ob_ref, stf_scr, stb_scr, bcf_scr, bcb_scr):
    @pl.when(pl.program_id(1) == 0)
    def _():
        stf_scr[...] = jnp.zeros_like(stf_scr)
        stb_scr[...] = jnp.zeros_like(stb_scr)

    for e in range(GLA_SEQS):
        _gla_block(e, qf_ref, kf_ref, vf_ref, gf_ref, stf_scr, bcf_scr, of_ref, True)
        _gla_block(e, qb_ref, kb_ref, vb_ref, gb_ref, stb_scr, bcb_scr, ob_ref, False)


def _gla_core(q, k, v, gf, gb):
    b, l, _ = q.shape
    c = GLA_BLOCK
    n = l // c
    ns = GLA_SEQS
    assert b % ns == 0 and l % c == 0

    def fw(w):
        return pl.BlockSpec((ns, c, w), lambda bi, i: (bi, i, 0))

    def bw(w):
        return pl.BlockSpec((ns, c, w), lambda bi, i: (bi, n - 1 - i, 0))

    return pl.pallas_call(
        _gla_core_kernel,
        grid=(b // ns, n),
        in_specs=[fw(GLA_DK), fw(GLA_DK), fw(GLA_DV), fw(GLA_DK),
                  bw(GLA_DK), bw(GLA_DK), bw(GLA_DV), bw(GLA_DK)],
        out_specs=[fw(GLA_DV), bw(GLA_DV)],
        out_shape=[jax.ShapeDtypeStruct((b, l, GLA_DV), BF16),
                   jax.ShapeDtypeStruct((b, l, GLA_DV), BF16)],
        scratch_shapes=[pltpu.VMEM((ns, GLA_HEADS, GLA_HEAD_V, GLA_HEAD_K), F32),
                        pltpu.VMEM((ns, GLA_HEADS, GLA_HEAD_V, GLA_HEAD_K), F32),
                        pltpu.VMEM((c, GLA_DK), F32), pltpu.VMEM((c, GLA_DK), F32)],
        compiler_params=_cparams(("parallel", "arbitrary")),
        name="gla_core",
    )(q, k, v, gf, q, k, v, gb)


def _gla_out_mlp_kernel(x_ref, of_ref, ob_ref, r_ref, ng_ref, gt1_ref, wo_ref,
                        g_ref, sc_ref, sh_ref, gt2_ref, w1_ref, w2_ref, fg_ref, o_ref,
                        x_scr, h_scr, acc_scr):
    j = pl.program_id(2)

    @pl.when(j == 0)
    def _():
        o = of_ref[0].astype(F32) + ob_ref[0].astype(F32)
        parts = []
        for h in range(GLA_HEADS):
            oh = o[:, GLA_HEAD_V * h:GLA_HEAD_V * (h + 1)]
            parts.append(oh * lax.rsqrt(jnp.mean(oh * oh, axis=-1, keepdims=True) + EPS))
        on = jnp.concatenate(parts, axis=-1) * ng_ref[...]
        r = r_ref[0].astype(F32)
        gated = on * (r * jax.nn.sigmoid(r))
        x = x_ref[0] + gt1_ref[0] * _dot(gated.astype(BF16), wo_ref[...])
        x_scr[...] = x
        h_scr[...] = _norm_mod(x, g_ref[...], sc_ref[0], sh_ref[0]).astype(BF16)
        acc_scr[...] = jnp.zeros_like(acc_scr)

    a = jnp.maximum(_dot(h_scr[...], w1_ref[...]), 0.0)
    acc_scr[...] += _dot((a * a).astype(BF16), w2_ref[...])

    @pl.when(j == pl.num_programs(2) - 1)
    def _():
        out = x_scr[...] + gt2_ref[0] * acc_scr[...]
        o_ref[0] = out * lax.rsqrt(jnp.mean(out * out, axis=-1, keepdims=True) + EPS) * fg_ref[...]


def _gla_out_mlp(x, o_f, o_b, r, gla_norm_g, gate1, w_out, norm_g, scale, shift, gate2, w1, w2,
                 final_g):
    b, l, d = x.shape
    ff = w1.shape[1]
    tm = GLA_TM
    tf = MLP_TF
    tok = pl.BlockSpec((1, tm, d), lambda bi, i, j: (bi, i, 0))
    vec = pl.BlockSpec((1, d), lambda bi, i, j: (0, 0))
    mod = pl.BlockSpec((1, 1, d), lambda bi, i, j: (bi, 0, 0))
    return pl.pallas_call(
        _gla_out_mlp_kernel,
        grid=(b, l // tm, ff // tf),
        in_specs=[
            tok, tok, tok, tok, vec, mod, pl.BlockSpec((d, d), lambda bi, i, j: (0, 0)),
            vec, mod, mod, mod,
            pl.BlockSpec((d, tf), lambda bi, i, j: (0, j)),
            pl.BlockSpec((tf, d), lambda bi, i, j: (j, 0)),
            vec,
        ],
        out_specs=tok,
        out_shape=jax.ShapeDtypeStruct((b, l, d), F32),
        scratch_shapes=[pltpu.VMEM((tm, d), F32), pltpu.VMEM((tm, d), BF16),
                        pltpu.VMEM((tm, d), F32)],
        compiler_params=_cparams(("parallel", "parallel", "arbitrary")),
        name="gla_out_mlp",
    )(x, o_f, o_b, r, gla_norm_g.reshape(1, d), gate1, w_out, norm_g.reshape(1, d), scale, shift,
      gate2, w1, w2, final_g.reshape(1, d))


def _trunk(x, mod, wts):
    b = x.shape[0]

    def mods(layer):
        m = mod[layer].reshape(b, N_MOD, 1, D_MODEL)
        return [m[:, i] for i in range(N_MOD)]

    shift1, scale1, gate1, shift2, scale2, gate2 = mods(0)
    u = _s5_pre(x, wts["norm1_g"][0], scale1, shift1)
    y = _s5_core(u, wts["s5_w1"], wts["s5_w2"], wts["s5_sc"], b)
    x = _s5_post(x, y, gate1, wts["s5_w_glu"])
    x = _mlp(x, wts["norm2_g"][0], scale2, shift2, gate2, wts["mlp_w1"][0], wts["mlp_w2"][0])
    shift1, scale1, gate1, shift2, scale2, gate2 = mods(1)
    q, k, v, r, gf, gb = _gla_pre(x, wts["norm1_g"][1], scale1, shift1, wts["gla_w_in"],
                                  wts["gla_w_a1"], wts["gla_w_a2"], wts["gla_b_a"])
    o_f, o_b = _gla_core(q, k, v, gf, gb)
    return _gla_out_mlp(x, o_f, o_b, r, wts["gla_norm_g"], gate1, wts["gla_w_out"],
                        wts["norm2_g"][1], scale2, shift2, gate2, wts["mlp_w1"][1],
                        wts["mlp_w2"][1], wts["final_g"])


def kernel(x_prompt, x_sample, c_prompt, c_sample, ada_w, ada_b, norm1_g, norm2_g, s5_lam_re, s5_lam_im, s5_log_dt, s5_b_re, s5_b_im, s5_c_re, s5_c_im, s5_d, s5_w_glu, gla_w_in, gla_w_a1, gla_w_a2, gla_b_a, gla_norm_g, gla_w_out, mlp_w1, mlp_w2, final_g):
    bp, bs = c_prompt.shape[0], c_sample.shape[0]
    pad = (-(bp + bs)) % SUBLANES
    c_all = jnp.concatenate([c_prompt, c_sample, jnp.zeros((pad, D_MODEL), F32)], axis=0)
    mod_all = _modulation(c_all, ada_w, ada_b)

    s5_w1, s5_w2, s5_sc = _s5_build(s5_lam_re[0], s5_lam_im[0], s5_log_dt[0], s5_b_re[0],
                                    s5_b_im[0], s5_c_re[0], s5_c_im[0], s5_d[0])
    r = GLA_GATE_RANK
    w_a2bd = jnp.zeros((2 * r, 2 * GLA_DK), F32)
    w_a2bd = w_a2bd.at[:r, :GLA_DK].set(gla_w_a2[0, 0]).at[r:, GLA_DK:].set(gla_w_a2[0, 1])
    wts = {
        "norm1_g": norm1_g, "norm2_g": norm2_g, "final_g": final_g,
        "s5_w1": s5_w1, "s5_w2": s5_w2, "s5_sc": s5_sc,
        "s5_w_glu": s5_w_glu[0].astype(BF16),
        "mlp_w1": mlp_w1.astype(BF16), "mlp_w2": mlp_w2.astype(BF16),
        "gla_w_in": gla_w_in[0].astype(BF16),
        "gla_w_a1": jnp.concatenate([gla_w_a1[0, 0], gla_w_a1[0, 1]], axis=1).astype(BF16),
        "gla_w_a2": w_a2bd.astype(BF16),
        "gla_b_a": jnp.concatenate([gla_b_a[0, 0], gla_b_a[0, 1]], axis=0).reshape(1, 2 * GLA_DK),
        "gla_norm_g": gla_norm_g[0], "gla_w_out": gla_w_out[0].astype(BF16),
    }
    y_prompt = _trunk(x_prompt, mod_all[:, :bp], wts)
    y_sample = _trunk(x_sample, mod_all[:, bp:bp + bs], wts)
    return (y_prompt, y_sample)
```
